```python
import jax, jax.numpy as jnp
from jax import lax
import numpy as np

D_MODEL = 1024
BATCH = 16
SEQ = 2048
DEPTH = 1
DEC_BATCH = 128
DEC_SEQ = 4
PAST_LEN = 8192
PAGE_SIZE = 128

HEAD_DIM = 64
ATT_HEADS = 8
ATT_WIDTH = ATT_HEADS * HEAD_DIM
CONV_WIDTH = D_MODEL - ATT_WIDTH
CONV_K = 3
DILATED_PATTERNS = ((128, 1), (512, 4), (2048, 16))
MAX_SPAN = max(w for w, _ in DILATED_PATTERNS)
D_FF = 4 * D_MODEL
ROPE_THETA = 10000.0
NORM_EPS = 1e-6
Q_BLOCK = 128
PROJ_COLS = 3 * ATT_WIDTH + 3 * CONV_WIDTH
PROJ_SPLITS = (ATT_WIDTH, 2 * ATT_WIDTH, 3 * ATT_WIDTH,
               3 * ATT_WIDTH + CONV_WIDTH, 3 * ATT_WIDTH + 2 * CONV_WIDTH)

kernel_name = "hymba_dilated_attn_shortconv_decoder_step"


def rmsnorm(x, g):
    xf = x.astype(jnp.float32)
    y = xf * lax.rsqrt(jnp.mean(xf * xf, axis=-1, keepdims=True) + NORM_EPS)
    return (y * g.astype(jnp.float32)).astype(x.dtype)


def rope(x, pos):
    half = x.shape[-1] // 2
    inv = ROPE_THETA ** (-jnp.arange(half, dtype=jnp.float32) * 2.0 / x.shape[-1])
    ang = pos.astype(jnp.float32)[:, None] * inv[None, :]
    cos = jnp.cos(ang)[None, :, None, :]
    sin = jnp.sin(ang)[None, :, None, :]
    xf = x.astype(jnp.float32)
    x1, x2 = xf[..., :half], xf[..., half:]
    return jnp.concatenate([x1 * cos - x2 * sin, x2 * cos + x1 * sin], axis=-1).astype(x.dtype)


def dilated_attention_block(q, pos, k_all, v_all, row_offset):
    neg = jnp.finfo(jnp.float32).min
    n_rows = k_all.shape[1]
    qf = q.astype(jnp.float32) * (HEAD_DIM ** -0.5)
    outs, lses = [], []
    for window, dil in DILATED_PATTERNS:
        dist = jnp.arange(window // dil + 1) * dil
        kpos = pos[:, None] - dist[None, :]
        valid = kpos >= 0
        rows = jnp.clip(kpos - row_offset, 0, n_rows - 1)
        kg = jnp.take(k_all, rows, axis=1).astype(jnp.float32)
        vg = jnp.take(v_all, rows, axis=1).astype(jnp.float32)
        s = jnp.einsum('bqhd,bqjhd->bqhj', qf, kg)
        s = jnp.where(valid[None, :, None, :], s, neg)
        m = jnp.max(s, axis=-1, keepdims=True)
        p = jnp.exp(s - m)
        den = jnp.sum(p, axis=-1, keepdims=True)
        outs.append(jnp.einsum('bqhj,bqjhd->bqhd', p, vg) / den)
        lses.append(m + jnp.log(den))
    w = jax.nn.softmax(jnp.concatenate(lses, axis=-1), axis=-1)
    out = w[..., 0:1] * outs[0] + w[..., 1:2] * outs[1] + w[..., 2:3] * outs[2]
    return out.astype(q.dtype)


def parallel_mixers(h, start, k_past, v_past, conv_past, w_in, conv_w, g_att, g_conv, w_out):
    B, T, _ = h.shape
    pos = start + jnp.arange(T)
    proj = h @ w_in
    q, k, v, gate_b, gate_c, u = jnp.split(proj, PROJ_SPLITS, axis=-1)
    q = rope(q.reshape(B, T, ATT_HEADS, HEAD_DIM), pos)
    k = rope(k.reshape(B, T, ATT_HEADS, HEAD_DIM), pos)
    v = v.reshape(B, T, ATT_HEADS, HEAD_DIM)

    if k_past is None:
        n_blk = T // Q_BLOCK
        qb = q.reshape(B, n_blk, Q_BLOCK, ATT_HEADS, HEAD_DIM).swapaxes(0, 1)
        pb = pos.reshape(n_blk, Q_BLOCK)
        ob = lax.map(lambda a: dilated_attention_block(a[0], a[1], k, v, 0), (qb, pb))
        attn = ob.swapaxes(0, 1).reshape(B, T, ATT_WIDTH)
        conv_pad = jnp.zeros((B, CONV_K - 1, CONV_WIDTH), h.dtype)
    else:
        k_all = jnp.concatenate([k_past, k], axis=1)
        v_all = jnp.concatenate([v_past, v], axis=1)
        attn = dilated_attention_block(q, pos, k_all, v_all, start - k_past.shape[1])
        attn = attn.reshape(B, T, ATT_WIDTH)
        conv_pad = conv_past

    gated_in = gate_c * u
    u_pad = jnp.concatenate([conv_pad.astype(gated_in.dtype), gated_in], axis=1)
    conv = conv_w[0] * u_pad[:, 0:T]
    for tap in range(1, CONV_K):
        conv = conv + conv_w[tap] * u_pad[:, tap:tap + T]
    conv_out = gate_b * conv
    conv_state = u_pad[:, -(CONV_K - 1):]

    mixed = jnp.concatenate([rmsnorm(attn, g_att), rmsnorm(conv_out, g_conv)], axis=-1) @ w_out
    return mixed, k, v, conv_state


def decoder_layer(x, start, k_past, v_past, conv_past, n_att_pre, n_att_post, w_in, conv_w,
                  g_att, g_conv, w_out, n_mlp_pre, n_mlp_post, w_up, w_down):
    h = rmsnorm(x, n_att_pre)
    mixed, k, v, conv_state = parallel_mixers(h, start, k_past, v_past, conv_past,
                                              w_in, conv_w, g_att, g_conv, w_out)
    x = x + rmsnorm(mixed, n_att_post)
    h = rmsnorm(x, n_mlp_pre)
    f = jnp.square(jax.nn.relu(h @ w_up)) @ w_down
    x = x + rmsnorm(f, n_mlp_post)
    return x, k, v, conv_state


def setup_inputs(seed: int = 0) -> dict:
    key = jax.random.key(seed)
    ks = jax.random.split(key, 20)
    lw = min(MAX_SPAN, PAST_LEN)
    f32 = jnp.float32
    nrm = lambda k, shape, s: jax.random.normal(k, shape, f32) * s
    gain = lambda k, shape: 1.0 + 0.05 * jax.random.normal(k, shape, f32)
    return {
        "x_prompt": nrm(ks[0], (BATCH, SEQ, D_MODEL), 1.0),
        "x_sample": nrm(ks[1], (DEC_BATCH, DEC_SEQ, D_MODEL), 1.0),
        "cache_k": nrm(ks[2], (DEPTH, DEC_BATCH, lw, ATT_HEADS, HEAD_DIM), 1.0),
        "cache_v": nrm(ks[3], (DEPTH, DEC_BATCH, lw, ATT_HEADS, HEAD_DIM), 1.0),
        "state_conv": nrm(ks[4], (DEPTH, DEC_BATCH, CONV_K - 1, CONV_WIDTH), 1.0),
        "n_att_pre": gain(ks[5], (DEPTH, D_MODEL)),
        "n_att_post": gain(ks[6], (DEPTH, D_MODEL)),
        "w_in": nrm(ks[7], (DEPTH, D_MODEL, PROJ_COLS), D_MODEL ** -0.5),
        "conv_w": nrm(ks[8], (DEPTH, CONV_K, CONV_WIDTH), CONV_K ** -0.5),
        "g_att": gain(ks[9], (DEPTH, ATT_WIDTH)),
        "g_conv": gain(ks[10], (DEPTH, CONV_WIDTH)),
        "w_out": nrm(ks[11], (DEPTH, D_MODEL, D_MODEL), D_MODEL ** -0.5),
        "n_mlp_pre": gain(ks[12], (DEPTH, D_MODEL)),
        "n_mlp_post": gain(ks[13], (DEPTH, D_MODEL)),
        "w_up": nrm(ks[14], (DEPTH, D_MODEL, D_FF), D_MODEL ** -0.5),
        "w_down": nrm(ks[15], (DEPTH, D_FF, D_MODEL), D_FF ** -0.5),
    }


def reference(x_prompt, x_sample, cache_k, cache_v, state_conv, n_att_pre, n_att_post, w_in,
              conv_w, g_att, g_conv, w_out, n_mlp_pre, n_mlp_post, w_up, w_down):
    keep_p = min(MAX_SPAN, x_prompt.shape[1])
    y_p, y_s = x_prompt, x_sample
    kp_l, vp_l, cp_l, ks_l, vs_l, cs_l = [], [], [], [], [], []
    for l in range(DEPTH):
        params = (n_att_pre[l], n_att_post[l], w_in[l], conv_w[l], g_att[l], g_conv[l],
                  w_out[l], n_mlp_pre[l], n_mlp_post[l], w_up[l], w_down[l])
        y_p, kp, vp, cp = decoder_layer(y_p, 0, None, None, None, *params)
        y_s, ks, vs, cs = decoder_layer(y_s, PAST_LEN, cache_k[l], cache_v[l], state_conv[l], *params)
        kp_l.append(kp[:, -keep_p:])
        vp_l.append(vp[:, -keep_p:])
        cp_l.append(cp)
        ks_l.append(ks)
        vs_l.append(vs)
        cs_l.append(cs)
    return (y_p, y_s, jnp.stack(kp_l), jnp.stack(vp_l), jnp.stack(cp_l),
            jnp.stack(ks_l), jnp.stack(vs_l), jnp.stack(cs_l))
```

```python
import functools

import numpy as np
import jax
import jax.numpy as jnp
from jax import lax
from jax.experimental import pallas as pl
from jax.experimental.pallas import tpu as pltpu

HEAD_DIM = 64
ATT_HEADS = 8
ATT_WIDTH = ATT_HEADS * HEAD_DIM
CONV_K = 3
DILATED_PATTERNS = ((128, 1), (512, 4), (2048, 16))
MAX_SPAN = max(w for w, _ in DILATED_PATTERNS)
PAST_LEN = 8192
ROPE_THETA = 10000.0
NORM_EPS = 1e-6

LANES = 128
HEADS_PER_SLAB = LANES // HEAD_DIM
N_SLABS = ATT_WIDTH // LANES
LSE_LANES_PER_HEAD = LANES // ATT_HEADS
MASK_VALUE = -1e30
VMEM_LIMIT_BYTES = 56 * 1024 * 1024

ROW_BLOCK = 512
Q_BLOCK = 128
FF_CHUNK = 1024

BF16 = jnp.bfloat16
F32 = jnp.float32


def _rmsnorm(x, g):
    return x * lax.rsqrt(jnp.mean(x * x, axis=-1, keepdims=True) + NORM_EPS) * g


def _rope(x, cos, sin_lo, sin_hi):
    half = HEAD_DIM // 2
    outs = []
    for c in range(N_SLABS):
        xs = x[:, c * LANES:(c + 1) * LANES]
        outs.append(xs * cos + pltpu.roll(xs, LANES - half, 1) * sin_lo
                    + pltpu.roll(xs, half, 1) * sin_hi)
    return jnp.concatenate(outs, axis=1)


def _project(x, gpre, win_ref, cos, sin_lo, sin_hi):
    h = _rmsnorm(x, gpre).astype(BF16)

    def cols(c):
        return jnp.dot(h, win_ref[:, c * ATT_WIDTH:(c + 1) * ATT_WIDTH],
                       preferred_element_type=F32)

    q = _rope(cols(0), cos, sin_lo, sin_hi) * (HEAD_DIM ** -0.5)
    k = _rope(cols(1), cos, sin_lo, sin_hi)
    v = cols(2)
    gate_b = cols(3)
    gated = cols(4) * cols(5)
    return q, k, v, gate_b, gated


def _proj_prompt_kernel(x_ref, gpre_ref, win_ref, cos_ref, slo_ref, shi_ref, convw_ref,
                        gconv_ref, q_ref, k_ref, v_ref, kb_ref, vb_ref, cn_ref, cs_ref,
                        hist_ref, *, tm, blocks_per_seq):
    j = pl.program_id(0) % blocks_per_seq
    row0 = pl.multiple_of(j * tm, tm)
    q, k, v, gate_b, gated = _project(
        x_ref[...], gpre_ref[...], win_ref, cos_ref[pl.ds(row0, tm), :],
        slo_ref[pl.ds(row0, tm), :], shi_ref[pl.ds(row0, tm), :])
    q_ref[...] = q.astype(BF16)
    k_ref[...] = k
    v_ref[...] = v
    kb_ref[...] = k.astype(BF16)
    vb_ref[...] = v.astype(BF16)

    @pl.when(j == 0)
    def _():
        hist_ref[0:8, :] = jnp.zeros((8, gated.shape[1]), F32)

    @pl.when(j != 0)
    def _():
        hist_ref[0:8, :] = hist_ref[tm:tm + 8, :]

    hist_ref[8:8 + tm, :] = gated
    conv = (convw_ref[0:1, :] * hist_ref[6:6 + tm, :] + convw_ref[1:2, :] * hist_ref[7:7 + tm, :]
            + convw_ref[2:3, :] * gated)
    cn_ref[...] = _rmsnorm(gate_b * conv, gconv_ref[...]).astype(BF16)
    cs_ref[...] = gated[tm - (CONV_K - 1):tm, :]


def _proj_sample_kernel(x_ref, gpre_ref, win_ref, cos_ref, slo_ref, shi_ref, convw_ref,
                        gconv_ref, past_ref, q_ref, k_ref, v_ref, cn_ref, cs_ref, hist_ref,
                        *, rows, stride):
    q, k, v, gate_b, gated = _project(x_ref[...], gpre_ref[...], win_ref, cos_ref[...],
                                      slo_ref[...], shi_ref[...])
    q_ref[...] = q
    k_ref[...] = k
    v_ref[...] = v
    npast = (CONV_K - 1) * stride
    hist_ref[0:npast, :] = past_ref[...]
    hist_ref[npast:npast + rows, :] = gated
    conv = (convw_ref[0:1, :] * hist_ref[0:rows, :]
            + convw_ref[1:2, :] * hist_ref[stride:stride + rows, :]
            + convw_ref[2:3, :] * gated)
    cn_ref[...] = _rmsnorm(gate_b * conv, gconv_ref[...]).astype(BF16)
    cs_ref[...] = hist_ref[rows:rows + npast, :]


def _window_attn_kernel(q_ref, k_ref, v_ref, o_ref, l_ref, *, seq, tq, window):
    i = pl.program_id(2)
    kv = min(2 * tq, seq)
    if seq > tq:
        start = pl.multiple_of(jnp.maximum(i - 1, 0) * tq, tq)
    else:
        start = 0
    qpos = i * tq + lax.broadcasted_iota(jnp.int32, (tq, kv), 0)
    kpos = start + lax.broadcasted_iota(jnp.int32, (tq, kv), 1)
    dist = qpos - kpos
    bias = jnp.where((dist >= 0) & (dist <= window), 0.0, MASK_VALUE).astype(F32)
    lane = lax.broadcasted_iota(jnp.int32, (tq, LANES), 1)
    first_head = lane < HEAD_DIM
    lse_tile = jnp.zeros((tq, LANES), F32)
    for c in range(N_SLABS):
        q2 = q_ref[:, c * LANES:(c + 1) * LANES]
        k2 = k_ref[pl.ds(start, kv), c * LANES:(c + 1) * LANES]
        v2 = v_ref[pl.ds(start, kv), c * LANES:(c + 1) * LANES]
        outs = []
        for hh in range(HEADS_PER_SLAB):
            qm = jnp.where(first_head if hh == 0 else ~first_head, q2, jnp.zeros_like(q2))
            s = lax.dot_general(qm, k2, (((1,), (1,)), ((), ())),
                                preferred_element_type=F32) + bias
            m = jnp.max(s, axis=1, keepdims=True)
            p = jnp.exp(s - m)
            den = jnp.sum(p, axis=1, keepdims=True)
            o = jnp.dot(p.astype(BF16), v2, preferred_element_type=F32)
            outs.append(o * (1.0 / den))
            head = c * HEADS_PER_SLAB + hh
            lse_tile = jnp.where(lane // LSE_LANES_PER_HEAD == head, m + jnp.log(den), lse_tile)
        o_ref[:, c * LANES:(c + 1) * LANES] = jnp.where(first_head, outs[0], outs[1]).astype(BF16)
    l_ref[...] = lse_tile


def _sample_attn_kernel(q_ref, kn_ref, vn_ref, ka_ref, kb_ref, va_ref, vb_ref, bias_ref,
                        o_ref, kall_ref, vall_ref, *, n_t, groups, tail, rows_pad):
    width = ATT_WIDTH
    for t in range(n_t):
        kall_ref[t * groups:(t + 1) * groups, :] = ka_ref[:, t * width:(t + 1) * width].astype(BF16)
        vall_ref[t * groups:(t + 1) * groups, :] = va_ref[:, t * width:(t + 1) * width].astype(BF16)
    base = n_t * groups
    kall_ref[base:base + tail, :] = kb_ref[...].astype(BF16)
    vall_ref[base:base + tail, :] = vb_ref[...].astype(BF16)
    new0 = base + tail
    npad = rows_pad - new0
    zpad = jnp.zeros((npad - n_t, width), F32)
    kall_ref[new0:rows_pad, :] = jnp.concatenate([kn_ref[:, 0, :], zpad], axis=0).astype(BF16)
    vall_ref[new0:rows_pad, :] = jnp.concatenate([vn_ref[:, 0, :], zpad], axis=0).astype(BF16)

    sub = lax.broadcasted_iota(jnp.int32, (ATT_HEADS, width), 0)
    lane = lax.broadcasted_iota(jnp.int32, (ATT_HEADS, width), 1)
    own_head = sub == lane // HEAD_DIM
    qf = q_ref[:, 0, :]
    q_tiles = [jnp.where(own_head, jnp.broadcast_to(qf[t:t + 1, :], (ATT_HEADS, width)),
                         jnp.zeros((ATT_HEADS, width), qf.dtype)) for t in range(n_t)]
    n_pat = len(DILATED_PATTERNS)
    qbd = jnp.concatenate(q_tiles * n_pat, axis=0).astype(BF16)
    s = lax.dot_general(qbd, kall_ref[...], (((1,), (1,)), ((), ())),
                        preferred_element_type=F32) + bias_ref[...]
    m = jnp.max(s, axis=1, keepdims=True)
    p = jnp.exp(s - m)
    den = jnp.sum(p, axis=1, keepdims=True)
    lse = m + jnp.log(den)
    per = n_t * ATT_HEADS
    lses = [lse[i * per:(i + 1) * per, :] for i in range(n_pat)]
    top = functools.reduce(jnp.maximum, lses)
    es = [jnp.exp(l - top) for l in lses]
    z = functools.reduce(lambda a, b: a + b, es)
    scale = jnp.concatenate([e / z for e in es], axis=0) / den
    o = jnp.dot((p * scale).astype(BF16), vall_ref[...], preferred_element_type=F32)
    for t in range(n_t):
        acc = o[t * ATT_HEADS:(t + 1) * ATT_HEADS, :]
        for i in range(1, n_pat):
            acc = acc + o[i * per + t * ATT_HEADS:i * per + (t + 1) * ATT_HEADS, :]
        o_ref[t, :, :] = jnp.sum(jnp.where(own_head, acc, 0.0), axis=0, keepdims=True)


def _post_kernel(*refs, n_parts, ff_chunk):
    x_ref = refs[0]
    part_refs = refs[1:1 + n_parts]
    pos = 1 + n_parts
    if n_parts > 1:
        lse_refs = refs[pos:pos + n_parts]
        expand_ref = refs[pos + n_parts]
        pos += n_parts + 1
    (cn_ref, gatt_ref, wout_ref, npost_ref, npre_ref, wup_ref, wdown_ref, nmlp_ref,
     y_ref) = refs[pos:]

    if n_parts > 1:
        lses = [r[...] for r in lse_refs]
        top = functools.reduce(jnp.maximum, lses)
        es = [jnp.exp(l - top) for l in lses]
        inv = 1.0 / functools.reduce(lambda a, b: a + b, es)
        attn = None
        for e, o_ref in zip(es, part_refs):
            wide = jnp.dot((e * inv).astype(BF16), expand_ref[...], preferred_element_type=F32)
            term = wide * o_ref[...].astype(F32)
            attn = term if attn is None else attn + term
    else:
        attn = part_refs[0][...]
    an = _rmsnorm(attn, gatt_ref[...]).astype(BF16)
    aw = an.shape[1]
    mixed = (jnp.dot(an, wout_ref[0:aw, :], preferred_element_type=F32)
             + jnp.dot(cn_ref[...], wout_ref[aw:, :], preferred_element_type=F32))
    x1 = x_ref[...] + _rmsnorm(mixed, npost_ref[...])
    h = _rmsnorm(x1, npre_ref[...]).astype(BF16)
    d_ff = wup_ref.shape[1]
    f = None
    for c in range(d_ff // ff_chunk):
        u = jnp.dot(h, wup_ref[:, c * ff_chunk:(c + 1) * ff_chunk], preferred_element_type=F32)
        a = jnp.square(jnp.maximum(u, 0.0)).astype(BF16)
        part = jnp.dot(a, wdown_ref[c * ff_chunk:(c + 1) * ff_chunk, :], preferred_element_type=F32)
        f = part if f is None else f + part
    y_ref[...] = x1 + _rmsnorm(f, nmlp_ref[...])


def _const_spec(shape):
    nd = len(shape)
    return pl.BlockSpec(shape, lambda *_: (0,) * nd, pipeline_mode=pl.Buffered(1))


def _rope_tables(positions):
    half = HEAD_DIM // 2
    inv = ROPE_THETA ** (-jnp.arange(half, dtype=F32) * 2.0 / HEAD_DIM)
    ang = positions.astype(F32)[:, None] * inv[None, :]
    cos, sin, zero = jnp.cos(ang), jnp.sin(ang), jnp.zeros_like(ang)
    reps = LANES // HEAD_DIM
    cos_t = jnp.tile(jnp.concatenate([cos, cos], axis=1), (1, reps))
    sin_lo = jnp.tile(jnp.concatenate([-sin, zero], axis=1), (1, reps))
    sin_hi = jnp.tile(jnp.concatenate([zero, sin], axis=1), (1, reps))
    return cos_t, sin_lo, sin_hi


def _params(*semantics):
    return pltpu.CompilerParams(dimension_semantics=semantics, vmem_limit_bytes=VMEM_LIMIT_BYTES)


def _proj_prompt(x2d, seq, gpre, win, convw, gconv):
    n, d = x2d.shape
    tm = min(ROW_BLOCK, seq)
    assert seq % tm == 0 and tm % 8 == 0
    bps = seq // tm
    batch = n // seq
    cw = convw.shape[1]
    tables = _rope_tables(jnp.arange(seq))
    row = lambda w: pl.BlockSpec((tm, w), lambda i: (i, 0))
    out_shape = (
        jax.ShapeDtypeStruct((n, ATT_WIDTH), BF16), jax.ShapeDtypeStruct((n, ATT_WIDTH), F32),
        jax.ShapeDtypeStruct((n, ATT_WIDTH), F32), jax.ShapeDtypeStruct((n, ATT_WIDTH), BF16),
        jax.ShapeDtypeStruct((n, ATT_WIDTH), BF16), jax.ShapeDtypeStruct((n, cw), BF16),
        jax.ShapeDtypeStruct((batch, CONV_K - 1, cw), F32))
    return pl.pallas_call(
        functools.partial(_proj_prompt_kernel, tm=tm, blocks_per_seq=bps),
        grid=(n // tm,),
        in_specs=[row(d), _const_spec((1, d)), _const_spec(win.shape)]
        + [_const_spec((seq, LANES))] * 3 + [_const_spec(convw.shape), _const_spec((1, cw))],
        out_specs=[row(ATT_WIDTH)] * 5 + [row(cw),
                   pl.BlockSpec((None, CONV_K - 1, cw), lambda i: (i // bps, 0, 0))],
        out_shape=out_shape,
        scratch_shapes=[pltpu.VMEM((tm + 8, cw), F32)],
        compiler_params=_params("arbitrary"),
        name="proj_prompt",
    )(x2d, gpre, win, *tables, convw, gconv)


def _proj_sample(x2d, positions, stride, past2d, gpre, win, convw, gconv):
    rows, d = x2d.shape
    cw = convw.shape[1]
    tables = [jnp.repeat(t, stride, axis=0) for t in _rope_tables(positions)]
    npast = (CONV_K - 1) * stride
    full = lambda shape: pl.BlockSpec(shape, lambda i: (0,) * len(shape))
    out_shape = (
        jax.ShapeDtypeStruct((rows, ATT_WIDTH), F32), jax.ShapeDtypeStruct((rows, ATT_WIDTH), F32),
        jax.ShapeDtypeStruct((rows, ATT_WIDTH), F32), jax.ShapeDtypeStruct((rows, cw), BF16),
        jax.ShapeDtypeStruct((npast, cw), F32))
    return pl.pallas_call(
        functools.partial(_proj_sample_kernel, rows=rows, stride=stride),
        grid=(1,),
        in_specs=[full((rows, d)), full((1, d)), full(win.shape)] + [full((rows, LANES))] * 3
        + [full(convw.shape), full((1, cw)), full((npast, cw))],
        out_specs=[full((rows, ATT_WIDTH))] * 3 + [full((rows, cw)), full((npast, cw))],
        out_shape=out_shape,
        scratch_shapes=[pltpu.VMEM((rows + npast, cw), F32)],
        compiler_params=_params("arbitrary"),
        name="proj_sample",
    )(x2d, gpre, win, *tables, convw, gconv, past2d)


def _window_attn(q, k, v, batch, seq, window, dil):
    sub = seq // dil
    tq = min(Q_BLOCK, sub)
    assert seq % dil == 0 and sub % tq == 0 and window % dil == 0 and window // dil <= tq
    view = lambda a: a.reshape(batch, sub, dil * ATT_WIDTH)
    blk = pl.BlockSpec((None, tq, ATT_WIDTH), lambda b, r, i: (b, i, r))
    kv_blk = pl.BlockSpec((None, sub, ATT_WIDTH), lambda b, r, i: (b, 0, r))
    o, lse = pl.pallas_call(
        functools.partial(_window_attn_kernel, seq=sub, tq=tq, window=window // dil),
        grid=(batch, dil, sub // tq),
        in_specs=[blk, kv_blk, kv_blk],
        out_specs=[blk, pl.BlockSpec((None, tq, LANES), lambda b, r, i: (b, i, r))],
        out_shape=(jax.ShapeDtypeStruct((batch, sub, dil * ATT_WIDTH), BF16),
                   jax.ShapeDtypeStruct((batch, sub, dil * LANES), F32)),
        compiler_params=_params("arbitrary", "arbitrary", "arbitrary"),
        name=f"window_attn_d{dil}",
    )(view(q), view(k), view(v))
    return o.reshape(batch * seq, ATT_WIDTH), lse.reshape(batch * seq, LANES)


def _sample_bias(n_t, past_len, span, groups, tail, rows_pad, stride16):
    rel = np.full((rows_pad,), 1 << 30, np.int64)
    for t in range(n_t):
        rel[t * groups:(t + 1) * groups] = -span + stride16 * np.arange(groups) + t
    base = n_t * groups
    rel[base:base + tail] = -tail + np.arange(tail)
    rel[base + tail:base + tail + n_t] = np.arange(n_t)
    rows = []
    for window, dil in DILATED_PATTERNS:
        for t in range(n_t):
            dist = t - rel
            ok = (dist >= 0) & (dist <= window) & (dist % dil == 0) & (past_len + rel >= 0)
            rows += [np.where(ok, 0.0, MASK_VALUE)] * ATT_HEADS
    return jnp.asarray(np.stack(rows), F32)


def _sample_attn(q, kn, vn, cache_k, cache_v, past_len):
    n_t, batch = q.shape[0], q.shape[1]
    span = cache_k.shape[1]
    far_dil = DILATED_PATTERNS[-1][1]
    tail = DILATED_PATTERNS[-2][0]
    assert span == MAX_SPAN and past_len >= span and span % far_dil == 0 and n_t <= far_dil
    assert all(w <= tail for w, _ in DILATED_PATTERNS[:-1]) and tail % far_dil == 0
    groups = (span - tail) // far_dil
    new0 = n_t * groups + tail
    rows_pad = -(-(new0 + n_t) // LANES) * LANES
    bias = _sample_bias(n_t, past_len, span, groups, tail, rows_pad, far_dil)
    ck = cache_k.reshape(batch, span // far_dil, far_dil * ATT_WIDTH)
    cv = cache_v.reshape(batch, span // far_dil, far_dil * ATT_WIDTH)
    ck2 = cache_k.reshape(batch, span, ATT_WIDTH)
    cv2 = cache_v.reshape(batch, span, ATT_WIDTH)
    new_blk = pl.BlockSpec((n_t, None, 1, ATT_WIDTH), lambda b: (0, b, 0, 0))
    far_blk = pl.BlockSpec((None, groups, n_t * ATT_WIDTH), lambda b: (b, 0, 0))
    tail_blk = pl.BlockSpec((None, tail, ATT_WIDTH), lambda b: (b, span // tail - 1, 0))
    n_score = len(DILATED_PATTERNS) * n_t * ATT_HEADS
    return pl.pallas_call(
        functools.partial(_sample_attn_kernel, n_t=n_t, groups=groups, tail=tail, rows_pad=rows_pad),
        grid=(batch,),
        in_specs=[new_blk, new_blk, new_blk, far_blk, tail_blk, far_blk, tail_blk,
                  _const_spec((n_score, rows_pad))],
        out_specs=new_blk,
        out_shape=jax.ShapeDtypeStruct((n_t, batch, 1, ATT_WIDTH), F32),
        scratch_shapes=[pltpu.VMEM((rows_pad, ATT_WIDTH), BF16)] * 2,
        compiler_params=_params("arbitrary"),
        name="sample_attn",
    )(q, kn, vn, ck, ck2, cv, cv2, bias)


def _post(x2d, parts, lses, cn, gatt, wout, npost, npre, wup, wdown, nmlp):
    n, d = x2d.shape
    tm = min(ROW_BLOCK, n)
    assert n % tm == 0
    row = lambda w: pl.BlockSpec((tm, w), lambda i: (i, 0))
    n_parts = len(parts)
    args = [x2d, *parts]
    specs = [row(d)] + [row(ATT_WIDTH)] * n_parts
    if n_parts > 1:
        col_head = np.arange(ATT_WIDTH) // HEAD_DIM
        expand = (np.arange(LANES)[:, None] == col_head[None, :] * LSE_LANES_PER_HEAD)
        args += [*lses, jnp.asarray(expand, BF16)]
        specs += [row(LANES)] * n_parts + [_const_spec((LANES, ATT_WIDTH))]
    args += [cn, gatt, wout, npost, npre, wup, wdown, nmlp]
    specs += [row(cn.shape[1]), _const_spec((1, ATT_WIDTH)), _const_spec(wout.shape),
              _const_spec((1, d)), _const_spec((1, d)), _const_spec(wup.shape),
              _const_spec(wdown.shape), _const_spec((1, d))]
    return pl.pallas_call(
        functools.partial(_post_kernel, n_parts=n_parts, ff_chunk=min(FF_CHUNK, wup.shape[1])),
        grid=(n // tm,),
        in_specs=specs,
        out_specs=row(d),
        out_shape=jax.ShapeDtypeStruct((n, d), F32),
        compiler_params=_params("arbitrary"),
        name=f"post_{n_parts}",
    )(*args)


def kernel(x_prompt, x_sample, cache_k, cache_v, state_conv, n_att_pre, n_att_post, w_in, conv_w,
           g_att, g_conv, w_out, n_mlp_pre, n_mlp_post, w_up, w_down):
    depth = w_in.shape[0]
    batch, seq, d = x_prompt.shape
    dec_batch, dec_seq, _ = x_sample.shape
    past_len = PAST_LEN
    keep = min(MAX_SPAN, seq)
    assert keep == seq, "the prompt's window buffer is its whole key/value sequence"
    cw = conv_w.shape[2]

    yp = x_prompt.reshape(batch * seq, d)
    ys = x_sample.swapaxes(0, 1).reshape(dec_seq * dec_batch, d)
    outs = [[] for _ in range(6)]
    for l in range(depth):
        win, wout = w_in[l].astype(BF16), w_out[l].astype(BF16)
        wup, wdown = w_up[l].astype(BF16), w_down[l].astype(BF16)
        gpre, npost = n_att_pre[l][None, :], n_att_post[l][None, :]
        npre, nmlp = n_mlp_pre[l][None, :], n_mlp_post[l][None, :]
        gatt, gconv = g_att[l][None, :], g_conv[l][None, :]

        q, k, v, kb, vb, cn, cs = _proj_prompt(yp, seq, gpre, win, conv_w[l], gconv)
        parts, lses = zip(*[_window_attn(q, kb, vb, batch, seq, w, dil)
                            for w, dil in DILATED_PATTERNS])
        yp = _post(yp, parts, lses, cn, gatt, wout, npost, npre, wup, wdown, nmlp)
        outs[0].append(k.reshape(batch, seq, ATT_HEADS, HEAD_DIM))
        outs[1].append(v.reshape(batch, seq, ATT_HEADS, HEAD_DIM))
        outs[2].append(cs)

        past = state_conv[l].swapaxes(0, 1).reshape((CONV_K - 1) * dec_batch, cw)
        qs, ks, vs, cns, css = _proj_sample(ys, past_len + jnp.arange(dec_seq), dec_batch, past,
                                            gpre, win, conv_w[l], gconv)
        tm4 = lambda a: a.reshape(dec_seq, dec_batch, 1, ATT_WIDTH)
        attn = _sample_attn(tm4(qs), tm4(ks), tm4(vs), cache_k[l], cache_v[l], past_len)
        ys = _post(ys, [attn.reshape(dec_seq * dec_batch, ATT_WIDTH)], None, cns, gatt, wout,
                   npost, npre, wup, wdown, nmlp)
        bm = lambda a: a.reshape(dec_seq, dec_batch, ATT_HEADS, HEAD_DIM).swapaxes(0, 1)
        outs[3].append(bm(ks))
        outs[4].append(bm(vs))
        outs[5].append(css.reshape(CONV_K - 1, dec_batch, cw).swapaxes(0, 1))

    y_prompt = yp.reshape(batch, seq, d)
    y_sample = ys.reshape(dec_seq, dec_batch, d).swapaxes(0, 1)
    return (y_prompt, y_sample) + tuple(jnp.stack(o) for o in outs)
```

```python
import functools

import numpy as np
import jax
import jax.numpy as jnp
from jax import lax
from jax.experimental import pallas as pl
from jax.experimental.pallas import tpu as pltpu

HEAD_DIM = 64
ATT_HEADS = 8
ATT_WIDTH = ATT_HEADS * HEAD_DIM
CONV_K = 3
DILATED_PATTERNS = ((128, 1), (512, 4), (2048, 16))
MAX_SPAN = max(w for w, _ in DILATED_PATTERNS)
PAST_LEN = 8192
ROPE_THETA = 10000.0
NORM_EPS = 1e-6

LANES = 128
HEADS_PER_SLAB = LANES // HEAD_DIM
N_SLABS = ATT_WIDTH // LANES
LSE_LANES_PER_HEAD = LANES // ATT_HEADS
MASK_VALUE = -1e30
VMEM_LIMIT_BYTES = 56 * 1024 * 1024

ROW_BLOCK = 512
Q_BLOCK = 128
FF_CHUNK = 1024

BF16 = jnp.bfloat16
F32 = jnp.float32


def _rmsnorm(x, g):
    return x * lax.rsqrt(jnp.mean(x * x, axis=-1, keepdims=True) + NORM_EPS) * g


def _rope(x, cos, sin_lo, sin_hi):
    half = HEAD_DIM // 2
    outs = []
    for c in range(N_SLABS):
        xs = x[:, c * LANES:(c + 1) * LANES]
        outs.append(xs * cos + pltpu.roll(xs, LANES - half, 1) * sin_lo
                    + pltpu.roll(xs, half, 1) * sin_hi)
    return jnp.concatenate(outs, axis=1)


def _project(x, gpre, win_ref, cos, sin_lo, sin_hi):
    h = _rmsnorm(x, gpre).astype(BF16)

    def cols(c):
        return jnp.dot(h, win_ref[:, c * ATT_WIDTH:(c + 1) * ATT_WIDTH],
                       preferred_element_type=F32)

    q = _rope(cols(0), cos, sin_lo, sin_hi) * (HEAD_DIM ** -0.5)
    k = _rope(cols(1), cos, sin_lo, sin_hi)
    v = cols(2)
    gate_b = cols(3)
    gated = cols(4) * cols(5)
    return q, k, v, gate_b, gated


def _proj_prompt_kernel(x_ref, gpre_ref, win_ref, cos_ref, slo_ref, shi_ref, convw_ref,
                        gconv_ref, q_ref, k_ref, v_ref, kb_ref, vb_ref, cn_ref, cs_ref,
                        hist_ref, *, tm, blocks_per_seq):
    j = pl.program_id(0) % blocks_per_seq
    row0 = pl.multiple_of(j * tm, tm)
    q, k, v, gate_b, gated = _project(
        x_ref[...], gpre_ref[...], win_ref, cos_ref[pl.ds(row0, tm), :],
        slo_ref[pl.ds(row0, tm), :], shi_ref[pl.ds(row0, tm), :])
    q_ref[...] = q.astype(BF16)
    k_ref[...] = k
    v_ref[...] = v
    kb_ref[...] = k.astype(BF16)
    vb_ref[...] = v.astype(BF16)

    @pl.when(j == 0)
    def _():
        hist_ref[0:8, :] = jnp.zeros((8, gated.shape[1]), F32)

    @pl.when(j != 0)
    def _():
        hist_ref[0:8, :] = hist_ref[tm:tm + 8, :]

    hist_ref[8:8 + tm, :] = gated
    conv = (convw_ref[0:1, :] * hist_ref[6:6 + tm, :] + convw_ref[1:2, :] * hist_ref[7:7 + tm, :]
            + convw_ref[2:3, :] * gated)
    cn_ref[...] = _rmsnorm(gate_b * conv, gconv_ref[...]).astype(BF16)
    cs_ref[...] = gated[tm - (CONV_K - 1):tm, :]


def _proj_sample_kernel(x_ref, gpre_ref, win_ref, cos_ref, slo_ref, shi_ref, convw_ref,
                        gconv_ref, past_ref, q_ref, k_ref, v_ref, cn_ref, cs_ref, hist_ref,
                        *, rows, stride):
    q, k, v, gate_b, gated = _project(x_ref[...], gpre_ref[...], win_ref, cos_ref[...],
                                      slo_ref[...], shi_ref[...])
    q_ref[...] = q
    k_ref[...] = k
    v_ref[...] = v
    npast = (CONV_K - 1) * stride
    hist_ref[0:npast, :] = past_ref[...]
    hist_ref[npast:npast + rows, :] = gated
    conv = (convw_ref[0:1, :] * hist_ref[0:rows, :]
            + convw_ref[1:2, :] * hist_ref[stride:stride + rows, :]
            + convw_ref[2:3, :] * gated)
    cn_ref[...] = _rmsnorm(gate_b * conv, gconv_ref[...]).astype(BF16)
    cs_ref[...] = hist_ref[rows:rows + npast, :]


def _window_attn_kernel(q_ref, k_ref, v_ref, o_ref, l_ref, *, seq, tq, window):
    i = pl.program_id(2)
    kv = min(2 * tq, seq)
    if seq > tq:
        start = pl.multiple_of(jnp.maximum(i - 1, 0) * tq, tq)
    else:
        start = 0
    qpos = i * tq + lax.broadcasted_iota(jnp.int32, (tq, kv), 0)
    kpos = start + lax.broadcasted_iota(jnp.int32, (tq, kv), 1)
    dist = qpos - kpos
    bias = jnp.where((dist >= 0) & (dist <= window), 0.0, MASK_VALUE).astype(F32)
    lane = lax.broadcasted_iota(jnp.int32, (tq, LANES), 1)
    first_head = lane < HEAD_DIM
    lse_tile = jnp.zeros((tq, LANES), F32)
    for c in range(N_SLABS):
        q2 = q_ref[:, c * LANES:(c + 1) * LANES]
        k2 = k_ref[pl.ds(start, kv), c * LANES:(c + 1) * LANES]
        v2 = v_ref[pl.ds(start, kv), c * LANES:(c + 1) * LANES]
        outs = []
        for hh in range(HEADS_PER_SLAB):
            qm = jnp.where(first_head if hh == 0 else ~first_head, q2, jnp.zeros_like(q2))
            s = lax.dot_general(qm, k2, (((1,), (1,)), ((), ())),
                                preferred_element_type=F32) + bias
            m = jnp.max(s, axis=1, keepdims=True)
            p = jnp.exp(s - m)
            den = jnp.sum(p, axis=1, keepdims=True)
            o = jnp.dot(p.astype(BF16), v2, preferred_element_type=F32)
            outs.append(o * (1.0 / den))
            head = c * HEADS_PER_SLAB + hh
            lse_tile = jnp.where(lane // LSE_LANES_PER_HEAD == head, m + jnp.log(den), lse_tile)
        o_ref[:, c * LANES:(c + 1) * LANES] = jnp.where(first_head, outs[0], outs[1]).astype(BF16)
    l_ref[...] = lse_tile


def _sample_attn_kernel(q_ref, kn_ref, vn_ref, kf_ref, kt_ref, vf_ref, vt_ref, bias_ref,
                        o_ref, *, n_t, new_pad):
    n_pat = len(DILATED_PATTERNS)
    per = n_t * ATT_HEADS

    def rows2d(ref):
        return ref[...].reshape(-1, HEAD_DIM)

    def with_pad(ref):
        pad = jnp.zeros((new_pad - per, HEAD_DIM), F32)
        return jnp.concatenate([rows2d(ref), pad], axis=0)

    qmat = jnp.concatenate([rows2d(q_ref)] * n_pat, axis=0).astype(BF16)
    k_parts = [rows2d(kf_ref), rows2d(kt_ref), with_pad(kn_ref)]
    v_parts = [rows2d(vf_ref), rows2d(vt_ref), with_pad(vn_ref)]
    s = jnp.concatenate(
        [lax.dot_general(qmat, kp.astype(BF16), (((1,), (1,)), ((), ())),
                         preferred_element_type=F32) for kp in k_parts], axis=1) + bias_ref[...]
    m = jnp.max(s, axis=1, keepdims=True)
    p = jnp.exp(s - m)
    den = jnp.sum(p, axis=1, keepdims=True)
    lse = m + jnp.log(den)
    lses = [lse[i * per:(i + 1) * per, :] for i in range(n_pat)]
    top = functools.reduce(jnp.maximum, lses)
    es = [jnp.exp(l - top) for l in lses]
    z = functools.reduce(lambda a, b: a + b, es)
    scale = jnp.concatenate([e / z for e in es], axis=0) / den
    pw = (p * scale).astype(BF16)
    o = None
    col = 0
    for vp in v_parts:
        part = jnp.dot(pw[:, col:col + vp.shape[0]], vp.astype(BF16), preferred_element_type=F32)
        o = part if o is None else o + part
        col += vp.shape[0]
    for t in range(n_t):
        acc = o[t * ATT_HEADS:(t + 1) * ATT_HEADS, :]
        for i in range(1, n_pat):
            acc = acc + o[i * per + t * ATT_HEADS:i * per + (t + 1) * ATT_HEADS, :]
        o_ref[t, :, :] = acc


def _post_kernel(*refs, n_parts, ff_chunk):
    x_ref = refs[0]
    part_refs = refs[1:1 + n_parts]
    pos = 1 + n_parts
    if n_parts > 1:
        lse_refs = refs[pos:pos + n_parts]
        expand_ref = refs[pos + n_parts]
        pos += n_parts + 1
    (cn_ref, gatt_ref, wout_ref, npost_ref, npre_ref, wup_ref, wdown_ref, nmlp_ref,
     y_ref) = refs[pos:]

    if n_parts > 1:
        lses = [r[...] for r in lse_refs]
        top = functools.reduce(jnp.maximum, lses)
        es = [jnp.exp(l - top) for l in lses]
        inv = 1.0 / functools.reduce(lambda a, b: a + b, es)
        attn = None
        for e, o_ref in zip(es, part_refs):
            wide = jnp.dot((e * inv).astype(BF16), expand_ref[...], preferred_element_type=F32)
            term = wide * o_ref[...].astype(F32)
            attn = term if attn is None else attn + term
    else:
        attn = part_refs[0][...]
    an = _rmsnorm(attn, gatt_ref[...]).astype(BF16)
    aw = an.shape[1]
    mixed = (jnp.dot(an, wout_ref[0:aw, :], preferred_element_type=F32)
             + jnp.dot(cn_ref[...], wout_ref[aw:, :], preferred_element_type=F32))
    x1 = x_ref[...] + _rmsnorm(mixed, npost_ref[...])
    h = _rmsnorm(x1, npre_ref[...]).astype(BF16)
    d_ff = wup_ref.shape[1]
    f = None
    for c in range(d_ff // ff_chunk):
        u = jnp.dot(h, wup_ref[:, c * ff_chunk:(c + 1) * ff_chunk], preferred_element_type=F32)
        a = jnp.square(jnp.maximum(u, 0.0)).astype(BF16)
        part = jnp.dot(a, wdown_ref[c * ff_chunk:(c + 1) * ff_chunk, :], preferred_element_type=F32)
        f = part if f is None else f + part
    y_ref[...] = x1 + _rmsnorm(f, nmlp_ref[...])


def _const_spec(shape):
    nd = len(shape)
    return pl.BlockSpec(shape, lambda *_: (0,) * nd, pipeline_mode=pl.Buffered(1))


def _rope_tables(positions):
    half = HEAD_DIM // 2
    inv = ROPE_THETA ** (-jnp.arange(half, dtype=F32) * 2.0 / HEAD_DIM)
    ang = positions.astype(F32)[:, None] * inv[None, :]
    cos, sin, zero = jnp.cos(ang), jnp.sin(ang), jnp.zeros_like(ang)
    reps = LANES // HEAD_DIM
    cos_t = jnp.tile(jnp.concatenate([cos, cos], axis=1), (1, reps))
    sin_lo = jnp.tile(jnp.concatenate([-sin, zero], axis=1), (1, reps))
    sin_hi = jnp.tile(jnp.concatenate([zero, sin], axis=1), (1, reps))
    return cos_t, sin_lo, sin_hi


def _params(*semantics):
    return pltpu.CompilerParams(dimension_semantics=semantics, vmem_limit_bytes=VMEM_LIMIT_BYTES)


def _proj_prompt(x2d, seq, gpre, win, convw, gconv):
    n, d = x2d.shape
    tm = min(ROW_BLOCK, seq)
    assert seq % tm == 0 and tm % 8 == 0
    bps = seq // tm
    batch = n // seq
    cw = convw.shape[1]
    tables = _rope_tables(jnp.arange(seq))
    row = lambda w: pl.BlockSpec((tm, w), lambda i: (i, 0))
    out_shape = (
        jax.ShapeDtypeStruct((n, ATT_WIDTH), BF16), jax.ShapeDtypeStruct((n, ATT_WIDTH), F32),
        jax.ShapeDtypeStruct((n, ATT_WIDTH), F32), jax.ShapeDtypeStruct((n, ATT_WIDTH), BF16),
        jax.ShapeDtypeStruct((n, ATT_WIDTH), BF16), jax.ShapeDtypeStruct((n, cw), BF16),
        jax.ShapeDtypeStruct((batch, CONV_K - 1, cw), F32))
    return pl.pallas_call(
        functools.partial(_proj_prompt_kernel, tm=tm, blocks_per_seq=bps),
        grid=(n // tm,),
        in_specs=[row(d), _const_spec((1, d)), _const_spec(win.shape)]
        + [_const_spec((seq, LANES))] * 3 + [_const_spec(convw.shape), _const_spec((1, cw))],
        out_specs=[row(ATT_WIDTH)] * 5 + [row(cw),
                   pl.BlockSpec((None, CONV_K - 1, cw), lambda i: (i // bps, 0, 0))],
        out_shape=out_shape,
        scratch_shapes=[pltpu.VMEM((tm + 8, cw), F32)],
        compiler_params=_params("arbitrary"),
        name="proj_prompt",
    )(x2d, gpre, win, *tables, convw, gconv)


def _proj_sample(x2d, positions, stride, past2d, gpre, win, convw, gconv):
    rows, d = x2d.shape
    cw = convw.shape[1]
    tables = [jnp.repeat(t, stride, axis=0) for t in _rope_tables(positions)]
    npast = (CONV_K - 1) * stride
    full = lambda shape: pl.BlockSpec(shape, lambda i: (0,) * len(shape))
    out_shape = (
        jax.ShapeDtypeStruct((rows, ATT_WIDTH), F32), jax.ShapeDtypeStruct((rows, ATT_WIDTH), F32),
        jax.ShapeDtypeStruct((rows, ATT_WIDTH), F32), jax.ShapeDtypeStruct((rows, cw), BF16),
        jax.ShapeDtypeStruct((npast, cw), F32))
    return pl.pallas_call(
        functools.partial(_proj_sample_kernel, rows=rows, stride=stride),
        grid=(1,),
        in_specs=[full((rows, d)), full((1, d)), full(win.shape)] + [full((rows, LANES))] * 3
        + [full(convw.shape), full((1, cw)), full((npast, cw))],
        out_specs=[full((rows, ATT_WIDTH))] * 3 + [full((rows, cw)), full((npast, cw))],
        out_shape=out_shape,
        scratch_shapes=[pltpu.VMEM((rows + npast, cw), F32)],
        compiler_params=_params("arbitrary"),
        name="proj_sample",
    )(x2d, gpre, win, *tables, convw, gconv, past2d)


def _window_attn(q, k, v, batch, seq, window, dil):
    sub = seq // dil
    tq = min(Q_BLOCK, sub)
    assert seq % dil == 0 and sub % tq == 0 and window % dil == 0 and window // dil <= tq
    view = lambda a: a.reshape(batch, sub, dil * ATT_WIDTH)
    blk = pl.BlockSpec((None, tq, ATT_WIDTH), lambda b, r, i: (b, i, r))
    kv_blk = pl.BlockSpec((None, sub, ATT_WIDTH), lambda b, r, i: (b, 0, r))
    o, lse = pl.pallas_call(
        functools.partial(_window_attn_kernel, seq=sub, tq=tq, window=window // dil),
        grid=(batch, dil, sub // tq),
        in_specs=[blk, kv_blk, kv_blk],
        out_specs=[blk, pl.BlockSpec((None, tq, LANES), lambda b, r, i: (b, i, r))],
        out_shape=(jax.ShapeDtypeStruct((batch, sub, dil * ATT_WIDTH), BF16),
                   jax.ShapeDtypeStruct((batch, sub, dil * LANES), F32)),
        compiler_params=_params("arbitrary", "arbitrary", "arbitrary"),
        name=f"window_attn_d{dil}",
    )(view(q), view(k), view(v))
    return o.reshape(batch * seq, ATT_WIDTH), lse.reshape(batch * seq, LANES)


def _sample_bias(n_t, past_len, span, groups, tail, new_pad, far_dil):
    big = 1 << 30
    rel_far = (-span + far_dil * np.arange(groups)[:, None] + np.arange(n_t)[None, :]).reshape(-1)
    rel_tail = -tail + np.arange(tail)
    rel_new = np.concatenate([np.arange(n_t), np.full((new_pad // ATT_HEADS - n_t,), -big)])
    rel = np.repeat(np.concatenate([rel_far, rel_tail, rel_new]), ATT_HEADS)
    col_head = np.tile(np.arange(ATT_HEADS), rel.shape[0] // ATT_HEADS)
    rows = []
    for window, dil in DILATED_PATTERNS:
        for t in range(n_t):
            dist = t - rel
            ok = (dist >= 0) & (dist <= window) & (dist % dil == 0) & (past_len + rel >= 0)
            rows += [np.where(ok & (col_head == h), 0.0, MASK_VALUE) for h in range(ATT_HEADS)]
    return jnp.asarray(np.stack(rows), F32)


def _sample_attn(q, kn, vn, cache_k, cache_v, past_len):
    batch, n_t = q.shape[0], q.shape[1]
    span = cache_k.shape[1]
    far_dil = DILATED_PATTERNS[-1][1]
    tail = DILATED_PATTERNS[-2][0]
    assert span == MAX_SPAN and past_len >= span and span % far_dil == 0 and n_t <= far_dil
    assert all(w <= tail for w, _ in DILATED_PATTERNS[:-1]) and tail % far_dil == 0
    assert span % tail == 0
    groups = (span - tail) // far_dil
    new_pad = -(-n_t * ATT_HEADS // LANES) * LANES
    bias = _sample_bias(n_t, past_len, span, groups, tail, new_pad, far_dil)
    grouped = (batch, span // far_dil, far_dil, ATT_HEADS, HEAD_DIM)
    new_blk = pl.BlockSpec((None, n_t, ATT_HEADS, HEAD_DIM), lambda b: (b, 0, 0, 0))
    far_blk = pl.BlockSpec((None, groups, n_t, ATT_HEADS, HEAD_DIM), lambda b: (b, 0, 0, 0, 0))
    tail_blk = pl.BlockSpec((None, tail, ATT_HEADS, HEAD_DIM), lambda b: (b, span // tail - 1, 0, 0))
    return pl.pallas_call(
        functools.partial(_sample_attn_kernel, n_t=n_t, new_pad=new_pad),
        grid=(batch,),
        in_specs=[new_blk, new_blk, new_blk, far_blk, tail_blk, far_blk, tail_blk,
                  _const_spec(bias.shape)],
        out_specs=new_blk,
        out_shape=jax.ShapeDtypeStruct((batch, n_t, ATT_HEADS, HEAD_DIM), F32),
        compiler_params=_params("arbitrary"),
        name="sample_attn",
    )(q, kn, vn, cache_k.reshape(grouped), cache_k, cache_v.reshape(grouped), cache_v, bias)


def _post(x2d, parts, lses, cn, gatt, wout, npost, npre, wup, wdown, nmlp):
    n, d = x2d.shape
    tm = min(ROW_BLOCK, n)
    assert n % tm == 0
    row = lambda w: pl.BlockSpec((tm, w), lambda i: (i, 0))
    n_parts = len(parts)
    args = [x2d, *parts]
    specs = [row(d)] + [row(ATT_WIDTH)] * n_parts
    if n_parts > 1:
        col_head = np.arange(ATT_WIDTH) // HEAD_DIM
        expand = (np.arange(LANES)[:, None] == col_head[None, :] * LSE_LANES_PER_HEAD)
        args += [*lses, jnp.asarray(expand, BF16)]
        specs += [row(LANES)] * n_parts + [_const_spec((LANES, ATT_WIDTH))]
    args += [cn, gatt, wout, npost, npre, wup, wdown, nmlp]
    specs += [row(cn.shape[1]), _const_spec((1, ATT_WIDTH)), _const_spec(wout.shape),
              _const_spec((1, d)), _const_spec((1, d)), _const_spec(wup.shape),
              _const_spec(wdown.shape), _const_spec((1, d))]
    return pl.pallas_call(
        functools.partial(_post_kernel, n_parts=n_parts, ff_chunk=min(FF_CHUNK, wup.shape[1])),
        grid=(n // tm,),
        in_specs=specs,
        out_specs=row(d),
        out_shape=jax.ShapeDtypeStruct((n, d), F32),
        compiler_params=_params("arbitrary"),
        name=f"post_{n_parts}",
    )(*args)


def kernel(x_prompt, x_sample, cache_k, cache_v, state_conv, n_att_pre, n_att_post, w_in, conv_w,
           g_att, g_conv, w_out, n_mlp_pre, n_mlp_post, w_up, w_down):
    depth = w_in.shape[0]
    batch, seq, d = x_prompt.shape
    dec_batch, dec_seq, _ = x_sample.shape
    past_len = PAST_LEN
    keep = min(MAX_SPAN, seq)
    assert keep == seq, "the prompt's window buffer is its whole key/value sequence"
    cw = conv_w.shape[2]

    yp = x_prompt.reshape(batch * seq, d)
    ys = x_sample.swapaxes(0, 1).reshape(dec_seq * dec_batch, d)
    outs = [[] for _ in range(6)]
    for l in range(depth):
        win, wout = w_in[l].astype(BF16), w_out[l].astype(BF16)
        wup, wdown = w_up[l].astype(BF16), w_down[l].astype(BF16)
        gpre, npost = n_att_pre[l][None, :], n_att_post[l][None, :]
        npre, nmlp = n_mlp_pre[l][None, :], n_mlp_post[l][None, :]
        gatt, gconv = g_att[l][None, :], g_conv[l][None, :]

        q, k, v, kb, vb, cn, cs = _proj_prompt(yp, seq, gpre, win, conv_w[l], gconv)
        parts, lses = zip(*[_window_attn(q, kb, vb, batch, seq, w, dil)
                            for w, dil in DILATED_PATTERNS])
        yp = _post(yp, parts, lses, cn, gatt, wout, npost, npre, wup, wdown, nmlp)
        outs[0].append(k.reshape(batch, seq, ATT_HEADS, HEAD_DIM))
        outs[1].append(v.reshape(batch, seq, ATT_HEADS, HEAD_DIM))
        outs[2].append(cs)

        past = state_conv[l].swapaxes(0, 1).reshape((CONV_K - 1) * dec_batch, cw)
        qs, ks, vs, cns, css = _proj_sample(ys, past_len + jnp.arange(dec_seq), dec_batch, past,
                                            gpre, win, conv_w[l], gconv)
        bm = lambda a: a.reshape(dec_seq, dec_batch, ATT_HEADS, HEAD_DIM).swapaxes(0, 1)
        ks_b, vs_b = bm(ks), bm(vs)
        attn = _sample_attn(bm(qs), ks_b, vs_b, cache_k[l], cache_v[l], past_len)
        attn = attn.swapaxes(0, 1).reshape(dec_seq * dec_batch, ATT_WIDTH)
        ys = _post(ys, [attn], None, cns, gatt, wout, npost, npre, wup, wdown, nmlp)
        outs[3].append(ks_b)
        outs[4].append(vs_b)
        outs[5].append(css.reshape(CONV_K - 1, dec_batch, cw).swapaxes(0, 1))

    y_prompt = yp.reshape(batch, seq, d)
    y_sample = ys.reshape(dec_seq, dec_batch, d).swapaxes(0, 1)
    return (y_prompt, y_sample) + tuple(jnp.stack(o) for o in outs)
```

```python
import functools

import numpy as np
import jax
import jax.numpy as jnp
from jax import lax
from jax.experimental import pallas as pl
from jax.experimental.pallas import tpu as pltpu

HEAD_DIM = 64
ATT_HEADS = 8
ATT_WIDTH = ATT_HEADS * HEAD_DIM
CONV_K = 3
DILATED_PATTERNS = ((128, 1), (512, 4), (2048, 16))
MAX_SPAN = max(w for w, _ in DILATED_PATTERNS)
PAST_LEN = 8192
ROPE_THETA = 10000.0
NORM_EPS = 1e-6

LANES = 128
HEADS_PER_SLAB = LANES // HEAD_DIM
N_SLABS = ATT_WIDTH // LANES
LSE_LANES_PER_HEAD = LANES // ATT_HEADS
MASK_VALUE = -1e30
VMEM_LIMIT_BYTES = 56 * 1024 * 1024

ROW_BLOCK = 512
Q_BLOCK = 128
FF_CHUNK = 1024

BF16 = jnp.bfloat16
F32 = jnp.float32


def _rmsnorm(x, g):
    return x * lax.rsqrt(jnp.mean(x * x, axis=-1, keepdims=True) + NORM_EPS) * g


def _rope(x, cos, sin_lo, sin_hi):
    half = HEAD_DIM // 2
    outs = []
    for c in range(N_SLABS):
        xs = x[:, c * LANES:(c + 1) * LANES]
        outs.append(xs * cos + pltpu.roll(xs, LANES - half, 1) * sin_lo
                    + pltpu.roll(xs, half, 1) * sin_hi)
    return jnp.concatenate(outs, axis=1)


def _project(x, gpre, win_ref, cos, sin_lo, sin_hi):
    h = _rmsnorm(x, gpre).astype(BF16)

    def cols(c):
        return jnp.dot(h, win_ref[:, c * ATT_WIDTH:(c + 1) * ATT_WIDTH],
                       preferred_element_type=F32)

    q = _rope(cols(0), cos, sin_lo, sin_hi) * (HEAD_DIM ** -0.5)
    k = _rope(cols(1), cos, sin_lo, sin_hi)
    v = cols(2)
    gate_b = cols(3)
    gated = cols(4) * cols(5)
    return q, k, v, gate_b, gated


def _proj_prompt_kernel(x_ref, gpre_ref, win_ref, cos_ref, slo_ref, shi_ref, convw_ref,
                        gconv_ref, q_ref, k_ref, v_ref, kb_ref, vb_ref, cn_ref, cs_ref,
                        hist_ref, *, tm, blocks_per_seq):
    j = pl.program_id(0) % blocks_per_seq
    row0 = pl.multiple_of(j * tm, tm)
    q, k, v, gate_b, gated = _project(
        x_ref[...], gpre_ref[...], win_ref, cos_ref[pl.ds(row0, tm), :],
        slo_ref[pl.ds(row0, tm), :], shi_ref[pl.ds(row0, tm), :])
    q_ref[...] = q.astype(BF16)
    k_ref[...] = k.T
    v_ref[...] = v.T
    kb_ref[...] = k.astype(BF16)
    vb_ref[...] = v.astype(BF16)

    @pl.when(j == 0)
    def _():
        hist_ref[0:8, :] = jnp.zeros((8, gated.shape[1]), F32)

    @pl.when(j != 0)
    def _():
        hist_ref[0:8, :] = hist_ref[tm:tm + 8, :]

    hist_ref[8:8 + tm, :] = gated
    conv = (convw_ref[0:1, :] * hist_ref[6:6 + tm, :] + convw_ref[1:2, :] * hist_ref[7:7 + tm, :]
            + convw_ref[2:3, :] * gated)
    cn_ref[...] = _rmsnorm(gate_b * conv, gconv_ref[...]).astype(BF16)
    cs_ref[...] = gated[tm - (CONV_K - 1):tm, :]


def _proj_sample_kernel(x_ref, gpre_ref, win_ref, cos_ref, slo_ref, shi_ref, convw_ref,
                        gconv_ref, past_ref, q_ref, k_ref, v_ref, cn_ref, cs_ref, hist_ref,
                        *, rows, stride):
    q, k, v, gate_b, gated = _project(x_ref[...], gpre_ref[...], win_ref, cos_ref[...],
                                      slo_ref[...], shi_ref[...])
    q_ref[...] = q
    k_ref[...] = k
    v_ref[...] = v
    npast = (CONV_K - 1) * stride
    hist_ref[0:npast, :] = past_ref[...]
    hist_ref[npast:npast + rows, :] = gated
    conv = (convw_ref[0:1, :] * hist_ref[0:rows, :]
            + convw_ref[1:2, :] * hist_ref[stride:stride + rows, :]
            + convw_ref[2:3, :] * gated)
    cn_ref[...] = _rmsnorm(gate_b * conv, gconv_ref[...]).astype(BF16)
    cs_ref[...] = hist_ref[rows:rows + npast, :]


def _window_attn_kernel(q_ref, k_ref, v_ref, o_ref, l_ref, *, seq, tq, window):
    i = pl.program_id(2)
    kv = min(2 * tq, seq)
    if seq > tq:
        start = pl.multiple_of(jnp.maximum(i - 1, 0) * tq, tq)
    else:
        start = 0
    qpos = i * tq + lax.broadcasted_iota(jnp.int32, (tq, kv), 0)
    kpos = start + lax.broadcasted_iota(jnp.int32, (tq, kv), 1)
    dist = qpos - kpos
    bias = jnp.where((dist >= 0) & (dist <= window), 0.0, MASK_VALUE).astype(F32)
    lane = lax.broadcasted_iota(jnp.int32, (tq, LANES), 1)
    first_head = lane < HEAD_DIM
    lse_tile = jnp.zeros((tq, LANES), F32)
    for c in range(N_SLABS):
        q2 = q_ref[:, c * LANES:(c + 1) * LANES]
        k2 = k_ref[pl.ds(start, kv), c * LANES:(c + 1) * LANES]
        v2 = v_ref[pl.ds(start, kv), c * LANES:(c + 1) * LANES]
        outs = []
        for hh in range(HEADS_PER_SLAB):
            qm = jnp.where(first_head if hh == 0 else ~first_head, q2, jnp.zeros_like(q2))
            s = lax.dot_general(qm, k2, (((1,), (1,)), ((), ())),
                                preferred_element_type=F32) + bias
            m = jnp.max(s, axis=1, keepdims=True)
            p = jnp.exp(s - m)
            den = jnp.sum(p, axis=1, keepdims=True)
            o = jnp.dot(p.astype(BF16), v2, preferred_element_type=F32)
            outs.append(o * (1.0 / den))
            head = c * HEADS_PER_SLAB + hh
            lse_tile = jnp.where(lane // LSE_LANES_PER_HEAD == head, m + jnp.log(den), lse_tile)
        o_ref[:, c * LANES:(c + 1) * LANES] = jnp.where(first_head, outs[0], outs[1]).astype(BF16)
    l_ref[...] = lse_tile


def _sample_attn_kernel(q_ref, kn_ref, vn_ref, kt_ref, vt_ref, bias_ref, o_ref, *, n_t, new_pad):
    width = ATT_WIDTH
    n_pat = len(DILATED_PATTERNS)
    per = n_t * ATT_HEADS
    nt_dims = (((1,), (1,)), ((), ()))
    sub = lax.broadcasted_iota(jnp.int32, (ATT_HEADS, width), 0)
    lane = lax.broadcasted_iota(jnp.int32, (ATT_HEADS, width), 1)
    own_head = sub == lane // HEAD_DIM
    qf = q_ref[:, 0, :]
    q_tiles = [jnp.where(own_head, jnp.broadcast_to(qf[t:t + 1, :], (ATT_HEADS, width)), 0.0)
               for t in range(n_t)]
    qbd = jnp.concatenate(q_tiles * n_pat, axis=0).astype(BF16)
    span = kt_ref.shape[-1]
    kt = kt_ref[...].reshape(width, span).astype(BF16)
    vt = vt_ref[...].reshape(width, span).astype(BF16)
    pad = jnp.zeros((new_pad - n_t, width), F32)
    kn = jnp.concatenate([kn_ref[:, 0, :], pad], axis=0).astype(BF16)
    vn = jnp.concatenate([vn_ref[:, 0, :], pad], axis=0).astype(BF16)
    s = jnp.concatenate(
        [jnp.dot(qbd, kt, preferred_element_type=F32),
         lax.dot_general(qbd, kn, nt_dims, preferred_element_type=F32)], axis=1) + bias_ref[...]
    m = jnp.max(s, axis=1, keepdims=True)
    p = jnp.exp(s - m)
    den = jnp.sum(p, axis=1, keepdims=True)
    lse = m + jnp.log(den)
    lses = [lse[i * per:(i + 1) * per, :] for i in range(n_pat)]
    top = functools.reduce(jnp.maximum, lses)
    es = [jnp.exp(l - top) for l in lses]
    z = functools.reduce(lambda a, b: a + b, es)
    scale = jnp.concatenate([e / z for e in es], axis=0) / den
    pw = (p * scale).astype(BF16)
    o = (lax.dot_general(pw[:, :span], vt, nt_dims, preferred_element_type=F32)
         + jnp.dot(pw[:, span:], vn, preferred_element_type=F32))
    for t in range(n_t):
        acc = o[t * ATT_HEADS:(t + 1) * ATT_HEADS, :]
        for i in range(1, n_pat):
            acc = acc + o[i * per + t * ATT_HEADS:i * per + (t + 1) * ATT_HEADS, :]
        o_ref[t, :, :] = jnp.sum(jnp.where(own_head, acc, 0.0), axis=0, keepdims=True)


def _post_kernel(*refs, n_parts, ff_chunk):
    x_ref = refs[0]
    part_refs = refs[1:1 + n_parts]
    pos = 1 + n_parts
    if n_parts > 1:
        lse_refs = refs[pos:pos + n_parts]
        expand_ref = refs[pos + n_parts]
        pos += n_parts + 1
    (cn_ref, gatt_ref, wout_ref, npost_ref, npre_ref, wup_ref, wdown_ref, nmlp_ref,
     y_ref) = refs[pos:]

    if n_parts > 1:
        lses = [r[...] for r in lse_refs]
        top = functools.reduce(jnp.maximum, lses)
        es = [jnp.exp(l - top) for l in lses]
        inv = 1.0 / functools.reduce(lambda a, b: a + b, es)
        attn = None
        for e, o_ref in zip(es, part_refs):
            wide = jnp.dot((e * inv).astype(BF16), expand_ref[...], preferred_element_type=F32)
            term = wide * o_ref[...].astype(F32)
            attn = term if attn is None else attn + term
    else:
        attn = part_refs[0][...]
    an = _rmsnorm(attn, gatt_ref[...]).astype(BF16)
    aw = an.shape[1]
    mixed = (jnp.dot(an, wout_ref[0:aw, :], preferred_element_type=F32)
             + jnp.dot(cn_ref[...], wout_ref[aw:, :], preferred_element_type=F32))
    x1 = x_ref[...] + _rmsnorm(mixed, npost_ref[...])
    h = _rmsnorm(x1, npre_ref[...]).astype(BF16)
    d_ff = wup_ref.shape[1]
    f = None
    for c in range(d_ff // ff_chunk):
        u = jnp.dot(h, wup_ref[:, c * ff_chunk:(c + 1) * ff_chunk], preferred_element_type=F32)
        a = jnp.square(jnp.maximum(u, 0.0)).astype(BF16)
        part = jnp.dot(a, wdown_ref[c * ff_chunk:(c + 1) * ff_chunk, :], preferred_element_type=F32)
        f = part if f is None else f + part
    y_ref[...] = x1 + _rmsnorm(f, nmlp_ref[...])


def _const_spec(shape):
    nd = len(shape)
    return pl.BlockSpec(shape, lambda *_: (0,) * nd, pipeline_mode=pl.Buffered(1))


def _rope_tables(positions):
    half = HEAD_DIM // 2
    inv = ROPE_THETA ** (-jnp.arange(half, dtype=F32) * 2.0 / HEAD_DIM)
    ang = positions.astype(F32)[:, None] * inv[None, :]
    cos, sin, zero = jnp.cos(ang), jnp.sin(ang), jnp.zeros_like(ang)
    reps = LANES // HEAD_DIM
    cos_t = jnp.tile(jnp.concatenate([cos, cos], axis=1), (1, reps))
    sin_lo = jnp.tile(jnp.concatenate([-sin, zero], axis=1), (1, reps))
    sin_hi = jnp.tile(jnp.concatenate([zero, sin], axis=1), (1, reps))
    return cos_t, sin_lo, sin_hi


def _params(*semantics):
    return pltpu.CompilerParams(dimension_semantics=semantics, vmem_limit_bytes=VMEM_LIMIT_BYTES)


def _proj_prompt(x2d, seq, gpre, win, convw, gconv):
    n, d = x2d.shape
    tm = min(ROW_BLOCK, seq)
    assert seq % tm == 0 and tm % 8 == 0
    bps = seq // tm
    batch = n // seq
    cw = convw.shape[1]
    tables = _rope_tables(jnp.arange(seq))
    row = lambda w: pl.BlockSpec((tm, w), lambda i: (i, 0))
    transposed = pl.BlockSpec((None, ATT_WIDTH, tm), lambda i: (i // bps, 0, i % bps))
    out_shape = (
        jax.ShapeDtypeStruct((n, ATT_WIDTH), BF16), jax.ShapeDtypeStruct((batch, ATT_WIDTH, seq), F32),
        jax.ShapeDtypeStruct((batch, ATT_WIDTH, seq), F32), jax.ShapeDtypeStruct((n, ATT_WIDTH), BF16),
        jax.ShapeDtypeStruct((n, ATT_WIDTH), BF16), jax.ShapeDtypeStruct((n, cw), BF16),
        jax.ShapeDtypeStruct((batch, CONV_K - 1, cw), F32))
    return pl.pallas_call(
        functools.partial(_proj_prompt_kernel, tm=tm, blocks_per_seq=bps),
        grid=(n // tm,),
        in_specs=[row(d), _const_spec((1, d)), _const_spec(win.shape)]
        + [_const_spec((seq, LANES))] * 3 + [_const_spec(convw.shape), _const_spec((1, cw))],
        out_specs=[row(ATT_WIDTH), transposed, transposed, row(ATT_WIDTH), row(ATT_WIDTH), row(cw),
                   pl.BlockSpec((None, CONV_K - 1, cw), lambda i: (i // bps, 0, 0))],
        out_shape=out_shape,
        scratch_shapes=[pltpu.VMEM((tm + 8, cw), F32)],
        compiler_params=_params("arbitrary"),
        name="proj_prompt",
    )(x2d, gpre, win, *tables, convw, gconv)


def _proj_sample(x2d, positions, stride, past2d, gpre, win, convw, gconv):
    rows, d = x2d.shape
    cw = convw.shape[1]
    tables = [jnp.repeat(t, stride, axis=0) for t in _rope_tables(positions)]
    npast = (CONV_K - 1) * stride
    full = lambda shape: pl.BlockSpec(shape, lambda i: (0,) * len(shape))
    out_shape = (
        jax.ShapeDtypeStruct((rows, ATT_WIDTH), F32), jax.ShapeDtypeStruct((rows, ATT_WIDTH), F32),
        jax.ShapeDtypeStruct((rows, ATT_WIDTH), F32), jax.ShapeDtypeStruct((rows, cw), BF16),
        jax.ShapeDtypeStruct((npast, cw), F32))
    return pl.pallas_call(
        functools.partial(_proj_sample_kernel, rows=rows, stride=stride),
        grid=(1,),
        in_specs=[full((rows, d)), full((1, d)), full(win.shape)] + [full((rows, LANES))] * 3
        + [full(convw.shape), full((1, cw)), full((npast, cw))],
        out_specs=[full((rows, ATT_WIDTH))] * 3 + [full((rows, cw)), full((npast, cw))],
        out_shape=out_shape,
        scratch_shapes=[pltpu.VMEM((rows + npast, cw), F32)],
        compiler_params=_params("arbitrary"),
        name="proj_sample",
    )(x2d, gpre, win, *tables, convw, gconv, past2d)


def _window_attn(q, k, v, batch, seq, window, dil):
    sub = seq // dil
    tq = min(Q_BLOCK, sub)
    assert seq % dil == 0 and sub % tq == 0 and window % dil == 0 and window // dil <= tq
    view = lambda a: a.reshape(batch, sub, dil * ATT_WIDTH)
    blk = pl.BlockSpec((None, tq, ATT_WIDTH), lambda b, r, i: (b, i, r))
    kv_blk = pl.BlockSpec((None, sub, ATT_WIDTH), lambda b, r, i: (b, 0, r))
    o, lse = pl.pallas_call(
        functools.partial(_window_attn_kernel, seq=sub, tq=tq, window=window // dil),
        grid=(batch, dil, sub // tq),
        in_specs=[blk, kv_blk, kv_blk],
        out_specs=[blk, pl.BlockSpec((None, tq, LANES), lambda b, r, i: (b, i, r))],
        out_shape=(jax.ShapeDtypeStruct((batch, sub, dil * ATT_WIDTH), BF16),
                   jax.ShapeDtypeStruct((batch, sub, dil * LANES), F32)),
        compiler_params=_params("arbitrary", "arbitrary", "arbitrary"),
        name=f"window_attn_d{dil}",
    )(view(q), view(k), view(v))
    return o.reshape(batch * seq, ATT_WIDTH), lse.reshape(batch * seq, LANES)


def _sample_bias(n_t, past_len, span, new_pad):
    big = 1 << 30
    rel = np.concatenate([np.arange(span) - span, np.arange(n_t),
                          np.full((new_pad - n_t,), -big)])
    rows = []
    for window, dil in DILATED_PATTERNS:
        for t in range(n_t):
            dist = t - rel
            ok = (dist >= 0) & (dist <= window) & (dist % dil == 0) & (past_len + rel >= 0)
            rows += [np.where(ok, 0.0, MASK_VALUE)] * ATT_HEADS
    return jnp.asarray(np.stack(rows), F32)


def _sample_attn(q, kn, vn, cache_k, cache_v, past_len):
    n_t, batch = q.shape[0], q.shape[1]
    span = cache_k.shape[1]
    assert span == min(MAX_SPAN, past_len)
    new_pad = -(-n_t // LANES) * LANES
    bias = _sample_bias(n_t, past_len, span, new_pad)
    kt = jnp.transpose(cache_k, (0, 2, 3, 1))
    vt = jnp.transpose(cache_v, (0, 2, 3, 1))
    new_blk = pl.BlockSpec((n_t, None, 1, ATT_WIDTH), lambda b: (0, b, 0, 0))
    buf_blk = pl.BlockSpec((None, ATT_HEADS, HEAD_DIM, span), lambda b: (b, 0, 0, 0))
    return pl.pallas_call(
        functools.partial(_sample_attn_kernel, n_t=n_t, new_pad=new_pad),
        grid=(batch,),
        in_specs=[new_blk, new_blk, new_blk, buf_blk, buf_blk, _const_spec(bias.shape)],
        out_specs=new_blk,
        out_shape=jax.ShapeDtypeStruct((n_t, batch, 1, ATT_WIDTH), F32),
        compiler_params=_params("arbitrary"),
        name="sample_attn",
    )(q, kn, vn, kt, vt, bias)


def _post(x2d, parts, lses, cn, gatt, wout, npost, npre, wup, wdown, nmlp):
    n, d = x2d.shape
    tm = min(ROW_BLOCK, n)
    assert n % tm == 0
    row = lambda w: pl.BlockSpec((tm, w), lambda i: (i, 0))
    n_parts = len(parts)
    args = [x2d, *parts]
    specs = [row(d)] + [row(ATT_WIDTH)] * n_parts
    if n_parts > 1:
        col_head = np.arange(ATT_WIDTH) // HEAD_DIM
        expand = (np.arange(LANES)[:, None] == col_head[None, :] * LSE_LANES_PER_HEAD)
        args += [*lses, jnp.asarray(expand, BF16)]
        specs += [row(LANES)] * n_parts + [_const_spec((LANES, ATT_WIDTH))]
    args += [cn, gatt, wout, npost, npre, wup, wdown, nmlp]
    specs += [row(cn.shape[1]), _const_spec((1, ATT_WIDTH)), _const_spec(wout.shape),
              _const_spec((1, d)), _const_spec((1, d)), _const_spec(wup.shape),
              _const_spec(wdown.shape), _const_spec((1, d))]
    return pl.pallas_call(
        functools.partial(_post_kernel, n_parts=n_parts, ff_chunk=min(FF_CHUNK, wup.shape[1])),
        grid=(n // tm,),
        in_specs=specs,
        out_specs=row(d),
        out_shape=jax.ShapeDtypeStruct((n, d), F32),
        compiler_params=_params("arbitrary"),
        name=f"post_{n_parts}",
    )(*args)


def kernel(x_prompt, x_sample, cache_k, cache_v, state_conv, n_att_pre, n_att_post, w_in, conv_w,
           g_att, g_conv, w_out, n_mlp_pre, n_mlp_post, w_up, w_down):
    depth = w_in.shape[0]
    batch, seq, d = x_prompt.shape
    dec_batch, dec_seq, _ = x_sample.shape
    past_len = PAST_LEN
    keep = min(MAX_SPAN, seq)
    assert keep == seq, "the prompt's window buffer is its whole key/value sequence"
    cw = conv_w.shape[2]

    yp = x_prompt.reshape(batch * seq, d)
    ys = x_sample.swapaxes(0, 1).reshape(dec_seq * dec_batch, d)
    outs = [[] for _ in range(6)]
    for l in range(depth):
        win, wout = w_in[l].astype(BF16), w_out[l].astype(BF16)
        wup, wdown = w_up[l].astype(BF16), w_down[l].astype(BF16)
        gpre, npost = n_att_pre[l][None, :], n_att_post[l][None, :]
        npre, nmlp = n_mlp_pre[l][None, :], n_mlp_post[l][None, :]
        gatt, gconv = g_att[l][None, :], g_conv[l][None, :]

        q, k, v, kb, vb, cn, cs = _proj_prompt(yp, seq, gpre, win, conv_w[l], gconv)
        parts, lses = zip(*[_window_attn(q, kb, vb, batch, seq, w, dil)
                            for w, dil in DILATED_PATTERNS])
        yp = _post(yp, parts, lses, cn, gatt, wout, npost, npre, wup, wdown, nmlp)
        per_head = lambda a: a.reshape(batch, ATT_HEADS, HEAD_DIM, seq).transpose(0, 3, 1, 2)
        outs[0].append(per_head(k))
        outs[1].append(per_head(v))
        outs[2].append(cs)

        past = state_conv[l].swapaxes(0, 1).reshape((CONV_K - 1) * dec_batch, cw)
        qs, ks, vs, cns, css = _proj_sample(ys, past_len + jnp.arange(dec_seq), dec_batch, past,
                                            gpre, win, conv_w[l], gconv)
        tm4 = lambda a: a.reshape(dec_seq, dec_batch, 1, ATT_WIDTH)
        attn = _sample_attn(tm4(qs), tm4(ks), tm4(vs), cache_k[l], cache_v[l], past_len)
        ys = _post(ys, [attn.reshape(dec_seq * dec_batch, ATT_WIDTH)], None, cns, gatt, wout,
                   npost, npre, wup, wdown, nmlp)
        bm = lambda a: a.reshape(dec_seq, dec_batch, ATT_HEADS, HEAD_DIM).swapaxes(0, 1)
        outs[3].append(bm(ks))
        outs[4].append(bm(vs))
        outs[5].append(css.reshape(CONV_K - 1, dec_batch, cw).swapaxes(0, 1))

    y_prompt = yp.reshape(batch, seq, d)
    y_sample = ys.reshape(dec_seq, dec_batch, d).swapaxes(0, 1)
    return (y_prompt, y_sample) + tuple(jnp.stack(o) for o in outs)
```

```python
import functools

import numpy as np
import jax
import jax.numpy as jnp
from jax import lax
from jax.experimental import pallas as pl
from jax.experimental.pallas import tpu as pltpu

HEAD_DIM = 64
ATT_HEADS = 8
ATT_WIDTH = ATT_HEADS * HEAD_DIM
CONV_K = 3
DILATED_PATTERNS = ((128, 1), (512, 4), (2048, 16))
MAX_SPAN = max(w for w, _ in DILATED_PATTERNS)
PAST_LEN = 8192
ROPE_THETA = 10000.0
NORM_EPS = 1e-6

LANES = 128
HEADS_PER_SLAB = LANES // HEAD_DIM
N_SLABS = ATT_WIDTH // LANES
LSE_LANES_PER_HEAD = LANES // ATT_HEADS
MASK_VALUE = -1e30
LOG2_E = 1.4426950408889634
VMEM_LIMIT_BYTES = 56 * 1024 * 1024

ROW_BLOCK = 512
Q_BLOCK = 128
FF_CHUNK = 1024

BF16 = jnp.bfloat16
F32 = jnp.float32
NT_DIMS = (((1,), (1,)), ((), ()))


def _rmsnorm(x, g):
    return x * lax.rsqrt(jnp.mean(x * x, axis=-1, keepdims=True) + NORM_EPS) * g


def _rope(x, cos, sin_lo, sin_hi):
    half = HEAD_DIM // 2
    outs = []
    for c in range(N_SLABS):
        xs = x[:, c * LANES:(c + 1) * LANES]
        outs.append(xs * cos + pltpu.roll(xs, LANES - half, 1) * sin_lo
                    + pltpu.roll(xs, half, 1) * sin_hi)
    return jnp.concatenate(outs, axis=1)


def _project(x, gpre, win_ref, cos, sin_lo, sin_hi, q_scale):
    h = _rmsnorm(x, gpre).astype(BF16)

    def cols(c):
        return jnp.dot(h, win_ref[:, c * ATT_WIDTH:(c + 1) * ATT_WIDTH],
                       preferred_element_type=F32)

    q = _rope(cols(0), cos, sin_lo, sin_hi) * q_scale
    k = _rope(cols(1), cos, sin_lo, sin_hi)
    v = cols(2)
    gate_b = cols(3)
    gated = cols(4) * cols(5)
    return q, k, v, gate_b, gated


def _proj_prompt_kernel(x_ref, gpre_ref, win_ref, cos_ref, slo_ref, shi_ref, convw_ref,
                        gconv_ref, q_ref, k_ref, v_ref, kt_ref, vt_ref, cn_ref, cs_ref,
                        hist_ref, *, tm, blocks_per_seq):
    j = pl.program_id(0) % blocks_per_seq
    row0 = pl.multiple_of(j * tm, tm)
    q, k, v, gate_b, gated = _project(
        x_ref[...], gpre_ref[...], win_ref, cos_ref[pl.ds(row0, tm), :],
        slo_ref[pl.ds(row0, tm), :], shi_ref[pl.ds(row0, tm), :], HEAD_DIM ** -0.5 * LOG2_E)
    for c in range(N_SLABS):
        q_ref[c] = q[:, c * LANES:(c + 1) * LANES]
        k_ref[c] = k[:, c * LANES:(c + 1) * LANES]
        v_ref[c] = v[:, c * LANES:(c + 1) * LANES]
    kt_ref[...] = k.T
    vt_ref[...] = v.T

    @pl.when(j == 0)
    def _():
        hist_ref[0:8, :] = jnp.zeros((8, gated.shape[1]), F32)

    @pl.when(j != 0)
    def _():
        hist_ref[0:8, :] = hist_ref[tm:tm + 8, :]

    hist_ref[8:8 + tm, :] = gated
    conv = (convw_ref[0:1, :] * hist_ref[6:6 + tm, :] + convw_ref[1:2, :] * hist_ref[7:7 + tm, :]
            + convw_ref[2:3, :] * gated)
    cn_ref[...] = _rmsnorm(gate_b * conv, gconv_ref[...]).astype(BF16)
    cs_ref[...] = gated[tm - (CONV_K - 1):tm, :]


def _proj_sample_kernel(x_ref, gpre_ref, win_ref, cos_ref, slo_ref, shi_ref, convw_ref,
                        gconv_ref, past_ref, q_ref, k_ref, v_ref, cn_ref, cs_ref, hist_ref,
                        *, rows, stride):
    q, k, v, gate_b, gated = _project(x_ref[...], gpre_ref[...], win_ref, cos_ref[...],
                                      slo_ref[...], shi_ref[...], HEAD_DIM ** -0.5)
    q_ref[...] = q
    k_ref[...] = k
    v_ref[...] = v
    npast = (CONV_K - 1) * stride
    hist_ref[0:npast, :] = past_ref[...]
    hist_ref[npast:npast + rows, :] = gated
    conv = (convw_ref[0:1, :] * hist_ref[0:rows, :]
            + convw_ref[1:2, :] * hist_ref[stride:stride + rows, :]
            + convw_ref[2:3, :] * gated)
    cn_ref[...] = _rmsnorm(gate_b * conv, gconv_ref[...]).astype(BF16)
    cs_ref[...] = hist_ref[rows:rows + npast, :]


def _attend(load, store_o, store_l, i, start, kv, reach, tq):
    lane = lax.broadcasted_iota(jnp.int32, (tq, LANES), 1)
    first_head = lane < HEAD_DIM
    qpos = i * tq + lax.broadcasted_iota(jnp.int32, (tq, kv), 0)
    kpos = start + lax.broadcasted_iota(jnp.int32, (tq, kv), 1)
    dist = qpos - kpos
    bias = jnp.where((dist >= 0) & (dist <= reach), 0.0, MASK_VALUE).astype(F32)
    lse_tile = jnp.zeros((tq, LANES), F32)
    key_lane = lax.broadcasted_iota(jnp.int32, (kv, LANES), 1)
    key_first = key_lane < HEAD_DIM
    for c in range(N_SLABS):
        q2 = load(0, c, i * tq, tq).astype(BF16)
        k2 = load(1, c, start, kv).astype(BF16)
        v2 = load(2, c, start, kv).astype(BF16)
        def softmax(s, hh, lse_tile):
            m = jnp.max(s, axis=1, keepdims=True)
            p = jnp.exp2(s - m)
            den = jnp.sum(p, axis=1, keepdims=True)
            head = c * HEADS_PER_SLAB + hh
            lse_tile = jnp.where(lane // LSE_LANES_PER_HEAD == head, m + jnp.log2(den), lse_tile)
            return p.astype(BF16), 1.0 / den, lse_tile

        if kv == tq:
            zero = jnp.zeros_like(k2)
            k_both = jnp.concatenate([jnp.where(key_first, k2, zero), jnp.where(key_first, zero, k2)], 0)
            v_both = jnp.concatenate([jnp.where(key_first, v2, zero), jnp.where(key_first, zero, v2)], 0)
            s_both = lax.dot_general(q2, k_both, NT_DIMS, preferred_element_type=F32)
            p0, inv0, lse_tile = softmax(s_both[:, :kv] + bias, 0, lse_tile)
            p1, inv1, lse_tile = softmax(s_both[:, kv:] + bias, 1, lse_tile)
            o = jnp.dot(jnp.concatenate([p0, p1], axis=1), v_both, preferred_element_type=F32)
            store_o(c, o * jnp.where(first_head, inv0, inv1))
        else:
            outs = []
            for hh in range(HEADS_PER_SLAB):
                qm = jnp.where(first_head if hh == 0 else ~first_head, q2, jnp.zeros_like(q2))
                s = lax.dot_general(qm, k2, NT_DIMS, preferred_element_type=F32) + bias
                p, inv, lse_tile = softmax(s, hh, lse_tile)
                outs.append(jnp.dot(p, v2, preferred_element_type=F32) * inv)
            store_o(c, jnp.where(first_head, outs[0], outs[1]))
    store_l(lse_tile)


def _prompt_attn_kernel(q_ref, k_ref, v_ref, expand_ref, gatt_ref, an_ref, o_scr, l_scr,
                        cls_scr, oc_scr, lc_scr, *, seq, tq):
    srcs = (q_ref, k_ref, v_ref)
    (near_w, near_d), (mid_w, mid_d), (far_w, far_d) = DILATED_PATTERNS
    inner = far_d // mid_d
    sub_mid, sub_far = seq // mid_d, seq // far_d
    nq_far = sub_far // tq

    def aligned(first, count):
        if isinstance(first, int):
            return pl.ds(first, count)
        return pl.ds(pl.multiple_of(first, tq), count)

    def near_unit(i, carry):
        start = jnp.maximum(i - 1, 0) * tq
        rows = aligned(i * tq, tq)

        def store_o(c, val):
            o_scr[0, c, rows, :] = val

        def store_l(val):
            l_scr[0, rows, :] = val

        _attend(lambda a, c, first, count: srcs[a][c, aligned(first, count), :],
                store_o, store_l, i, start, min(2 * tq, seq), near_w // near_d, tq)
        return carry

    lax.fori_loop(0, seq // tq, near_unit, 0)

    def mid_class(r, carry):
        for a in range(3):
            for c in range(N_SLABS):
                cls_scr[a, c] = srcs[a][c, pl.ds(r, sub_mid, stride=mid_d), :]

        def mid_unit(i, carry):
            start = jnp.maximum(i - 1, 0) * tq if sub_mid > tq else 0
            rows = pl.ds(r + mid_d * (i * tq), tq, stride=mid_d)

            def store_o(c, val):
                o_scr[1, c, rows, :] = val

            def store_l(val):
                l_scr[1, rows, :] = val

            _attend(lambda a, c, first, count: cls_scr[a, c, aligned(first, count), :],
                    store_o, store_l, i, start, min(2 * tq, sub_mid), mid_w // mid_d, tq)
            return carry

        lax.fori_loop(0, sub_mid // tq, mid_unit, 0)

        def far_unit(n, carry):
            m = n >> (nq_far.bit_length() - 1)
            i = n & (nq_far - 1)
            start = jnp.maximum(i - 1, 0) * tq if sub_far > tq else 0
            rows = pl.ds(m + inner * (i * tq), tq, stride=inner)

            def store_o(c, val):
                oc_scr[c, rows, :] = val

            def store_l(val):
                lc_scr[rows, :] = val

            _attend(lambda a, c, first, count:
                    cls_scr[a, c, pl.ds(m + inner * first, count, stride=inner), :],
                    store_o, store_l, i, start, min(2 * tq, sub_far), far_w // far_d, tq)
            return carry

        lax.fori_loop(0, inner * nq_far, far_unit, 0)
        back = pl.ds(r, sub_mid, stride=mid_d)
        for c in range(N_SLABS):
            o_scr[2, c, back, :] = oc_scr[c]
        l_scr[2, back, :] = lc_scr[...]
        return carry

    lax.fori_loop(0, mid_d, mid_class, 0)

    n_pat = len(DILATED_PATTERNS)

    def merge(b, carry):
        r = pl.ds(pl.multiple_of(b * tq, tq), tq)
        lses = [l_scr[ip, r, :] for ip in range(n_pat)]
        top = functools.reduce(jnp.maximum, lses)
        es = [jnp.exp2(l - top) for l in lses]
        inv = 1.0 / functools.reduce(lambda x, y: x + y, es)
        slabs = [None] * N_SLABS
        for ip, e in enumerate(es):
            wide = jnp.dot((e * inv).astype(BF16), expand_ref[...], preferred_element_type=F32)
            for c in range(N_SLABS):
                term = wide[:, c * LANES:(c + 1) * LANES] * o_scr[ip, c, r, :]
                slabs[c] = term if slabs[c] is None else slabs[c] + term
        attn = jnp.concatenate(slabs, axis=1)
        an_ref[r, :] = _rmsnorm(attn, gatt_ref[...]).astype(BF16)
        return carry

    lax.fori_loop(0, seq // tq, merge, 0)


def _sample_attn_kernel(q_ref, kn_ref, vn_ref, kt_ref, vt_ref, bias_ref, gatt_ref, o_ref,
                        *, n_t, new_pad):
    width = ATT_WIDTH
    n_pat = len(DILATED_PATTERNS)
    per = n_t * ATT_HEADS
    sub = lax.broadcasted_iota(jnp.int32, (ATT_HEADS, width), 0)
    lane = lax.broadcasted_iota(jnp.int32, (ATT_HEADS, width), 1)
    own_head = sub == lane // HEAD_DIM
    qf = q_ref[:, 0, :]
    q_tiles = [jnp.where(own_head, jnp.broadcast_to(qf[t:t + 1, :], (ATT_HEADS, width)), 0.0)
               for t in range(n_t)]
    qbd = jnp.concatenate(q_tiles * n_pat, axis=0).astype(BF16)
    span = kt_ref.shape[-1]
    kt = kt_ref[...].reshape(width, span).astype(BF16)
    vt = vt_ref[...].reshape(width, span).astype(BF16)
    pad = jnp.zeros((new_pad - n_t, width), F32)
    kn = jnp.concatenate([kn_ref[:, 0, :], pad], axis=0).astype(BF16)
    vn = jnp.concatenate([vn_ref[:, 0, :], pad], axis=0).astype(BF16)
    s = jnp.concatenate(
        [jnp.dot(qbd, kt, preferred_element_type=F32),
         lax.dot_general(qbd, kn, NT_DIMS, preferred_element_type=F32)], axis=1) + bias_ref[...]
    m = jnp.max(s, axis=1, keepdims=True)
    p = jnp.exp(s - m)
    den = jnp.sum(p, axis=1, keepdims=True)
    lse = m + jnp.log(den)
    lses = [lse[i * per:(i + 1) * per, :] for i in range(n_pat)]
    top = functools.reduce(jnp.maximum, lses)
    es = [jnp.exp(l - top) for l in lses]
    z = functools.reduce(lambda a, b: a + b, es)
    scale = jnp.concatenate([e / z for e in es], axis=0) / den
    pw = (p * scale).astype(BF16)
    o = (lax.dot_general(pw[:, :span], vt, NT_DIMS, preferred_element_type=F32)
         + jnp.dot(pw[:, span:], vn, preferred_element_type=F32))
    for t in range(n_t):
        acc = o[t * ATT_HEADS:(t + 1) * ATT_HEADS, :]
        for i in range(1, n_pat):
            acc = acc + o[i * per + t * ATT_HEADS:i * per + (t + 1) * ATT_HEADS, :]
        attn = jnp.sum(jnp.where(own_head, acc, 0.0), axis=0, keepdims=True)
        o_ref[t, :, :] = _rmsnorm(attn, gatt_ref[...])


def _post_kernel(x_ref, an_ref, cn_ref, wout_ref, npost_ref, npre_ref, wup_ref, wdown_ref,
                 nmlp_ref, y_ref, *, ff_chunk):
    an = an_ref[...].astype(BF16)
    aw = an.shape[1]
    mixed = (jnp.dot(an, wout_ref[0:aw, :], preferred_element_type=F32)
             + jnp.dot(cn_ref[...], wout_ref[aw:, :], preferred_element_type=F32))
    x1 = x_ref[...] + _rmsnorm(mixed, npost_ref[...])
    h = _rmsnorm(x1, npre_ref[...]).astype(BF16)
    d_ff = wup_ref.shape[1]
    f = None
    for c in range(d_ff // ff_chunk):
        u = jnp.dot(h, wup_ref[:, c * ff_chunk:(c + 1) * ff_chunk], preferred_element_type=F32)
        a = jnp.square(jnp.maximum(u, 0.0)).astype(BF16)
        part = jnp.dot(a, wdown_ref[c * ff_chunk:(c + 1) * ff_chunk, :], preferred_element_type=F32)
        f = part if f is None else f + part
    y_ref[...] = x1 + _rmsnorm(f, nmlp_ref[...])


def _const_spec(shape):
    nd = len(shape)
    return pl.BlockSpec(shape, lambda *_: (0,) * nd, pipeline_mode=pl.Buffered(1))


def _rope_tables(positions):
    half = HEAD_DIM // 2
    inv = ROPE_THETA ** (-jnp.arange(half, dtype=F32) * 2.0 / HEAD_DIM)
    ang = positions.astype(F32)[:, None] * inv[None, :]
    cos, sin, zero = jnp.cos(ang), jnp.sin(ang), jnp.zeros_like(ang)
    reps = LANES // HEAD_DIM
    cos_t = jnp.tile(jnp.concatenate([cos, cos], axis=1), (1, reps))
    sin_lo = jnp.tile(jnp.concatenate([-sin, zero], axis=1), (1, reps))
    sin_hi = jnp.tile(jnp.concatenate([zero, sin], axis=1), (1, reps))
    return cos_t, sin_lo, sin_hi


def _params(*semantics):
    return pltpu.CompilerParams(dimension_semantics=semantics, vmem_limit_bytes=VMEM_LIMIT_BYTES)


def _proj_prompt(x2d, seq, gpre, win, convw, gconv):
    n, d = x2d.shape
    tm = min(ROW_BLOCK, seq)
    assert seq % tm == 0 and tm % 8 == 0
    bps = seq // tm
    batch = n // seq
    cw = convw.shape[1]
    tables = _rope_tables(jnp.arange(seq))
    row = lambda w: pl.BlockSpec((tm, w), lambda i: (i, 0))
    slabbed = pl.BlockSpec((None, N_SLABS, tm, LANES), lambda i: (i // bps, 0, i % bps, 0))
    transposed = pl.BlockSpec((None, ATT_WIDTH, tm), lambda i: (i // bps, 0, i % bps))
    slab_shape = jax.ShapeDtypeStruct((batch, N_SLABS, seq, LANES), F32)
    t_shape = jax.ShapeDtypeStruct((batch, ATT_WIDTH, seq), F32)
    out_shape = (slab_shape, slab_shape, slab_shape, t_shape, t_shape,
                 jax.ShapeDtypeStruct((n, cw), BF16),
                 jax.ShapeDtypeStruct((batch, CONV_K - 1, cw), F32))
    return pl.pallas_call(
        functools.partial(_proj_prompt_kernel, tm=tm, blocks_per_seq=bps),
        grid=(n // tm,),
        in_specs=[row(d), _const_spec((1, d)), _const_spec(win.shape)]
        + [_const_spec((seq, LANES))] * 3 + [_const_spec(convw.shape), _const_spec((1, cw))],
        out_specs=[slabbed, slabbed, slabbed, transposed, transposed, row(cw),
                   pl.BlockSpec((None, CONV_K - 1, cw), lambda i: (i // bps, 0, 0))],
        out_shape=out_shape,
        scratch_shapes=[pltpu.VMEM((tm + 8, cw), F32)],
        compiler_params=_params("arbitrary"),
        name="proj_prompt",
    )(x2d, gpre, win, *tables, convw, gconv)


def _proj_sample(x2d, positions, stride, past2d, gpre, win, convw, gconv):
    rows, d = x2d.shape
    cw = convw.shape[1]
    tables = [jnp.repeat(t, stride, axis=0) for t in _rope_tables(positions)]
    npast = (CONV_K - 1) * stride
    full = lambda shape: pl.BlockSpec(shape, lambda i: (0,) * len(shape))
    out_shape = (
        jax.ShapeDtypeStruct((rows, ATT_WIDTH), F32), jax.ShapeDtypeStruct((rows, ATT_WIDTH), F32),
        jax.ShapeDtypeStruct((rows, ATT_WIDTH), F32), jax.ShapeDtypeStruct((rows, cw), BF16),
        jax.ShapeDtypeStruct((npast, cw), F32))
    return pl.pallas_call(
        functools.partial(_proj_sample_kernel, rows=rows, stride=stride),
        grid=(1,),
        in_specs=[full((rows, d)), full((1, d)), full(win.shape)] + [full((rows, LANES))] * 3
        + [full(convw.shape), full((1, cw)), full((npast, cw))],
        out_specs=[full((rows, ATT_WIDTH))] * 3 + [full((rows, cw)), full((npast, cw))],
        out_shape=out_shape,
        scratch_shapes=[pltpu.VMEM((rows + npast, cw), F32)],
        compiler_params=_params("arbitrary"),
        name="proj_sample",
    )(x2d, gpre, win, *tables, convw, gconv, past2d)


def _prompt_attn(q, k, v, gatt):
    batch, _, seq, _ = q.shape
    tq = Q_BLOCK
    (_, near_d), (_, mid_d), (_, far_d) = DILATED_PATTERNS
    assert near_d == 1 and far_d % mid_d == 0
    for window, dil in DILATED_PATTERNS:
        sub = seq // dil
        assert seq % dil == 0 and sub % tq == 0 and window % dil == 0 and window // dil <= tq
        assert (sub // tq) & (sub // tq - 1) == 0
    sub_mid = seq // mid_d
    col_head = np.arange(ATT_WIDTH) // HEAD_DIM
    expand = jnp.asarray(np.arange(LANES)[:, None] == col_head[None, :] * LSE_LANES_PER_HEAD, BF16)
    n_pat = len(DILATED_PATTERNS)
    blk = pl.BlockSpec((None, N_SLABS, seq, LANES), lambda b: (b, 0, 0, 0))
    return pl.pallas_call(
        functools.partial(_prompt_attn_kernel, seq=seq, tq=tq),
        grid=(batch,),
        in_specs=[blk, blk, blk, _const_spec(expand.shape), _const_spec(gatt.shape)],
        out_specs=pl.BlockSpec((seq, ATT_WIDTH), lambda b: (b, 0)),
        out_shape=jax.ShapeDtypeStruct((batch * seq, ATT_WIDTH), BF16),
        scratch_shapes=[pltpu.VMEM((n_pat, N_SLABS, seq, LANES), F32),
                        pltpu.VMEM((n_pat, seq, LANES), F32),
                        pltpu.VMEM((3, N_SLABS, sub_mid, LANES), F32),
                        pltpu.VMEM((N_SLABS, sub_mid, LANES), F32),
                        pltpu.VMEM((sub_mid, LANES), F32)],
        compiler_params=_params("arbitrary"),
        name="prompt_attn",
    )(q, k, v, expand, gatt)


def _sample_bias(n_t, past_len, span, new_pad):
    big = 1 << 30
    rel = np.concatenate([np.arange(span) - span, np.arange(n_t),
                          np.full((new_pad - n_t,), -big)])
    rows = []
    for window, dil in DILATED_PATTERNS:
        for t in range(n_t):
            dist = t - rel
            ok = (dist >= 0) & (dist <= window) & (dist % dil == 0) & (past_len + rel >= 0)
            rows += [np.where(ok, 0.0, MASK_VALUE)] * ATT_HEADS
    return jnp.asarray(np.stack(rows), F32)


def _sample_attn(q, kn, vn, cache_k, cache_v, past_len, gatt):
    n_t, batch = q.shape[0], q.shape[1]
    span = cache_k.shape[1]
    assert span == min(MAX_SPAN, past_len)
    new_pad = -(-n_t // LANES) * LANES
    bias = _sample_bias(n_t, past_len, span, new_pad)
    kt = jnp.transpose(cache_k, (0, 2, 3, 1))
    vt = jnp.transpose(cache_v, (0, 2, 3, 1))
    new_blk = pl.BlockSpec((n_t, None, 1, ATT_WIDTH), lambda b: (0, b, 0, 0))
    buf_blk = pl.BlockSpec((None, ATT_HEADS, HEAD_DIM, span), lambda b: (b, 0, 0, 0))
    return pl.pallas_call(
        functools.partial(_sample_attn_kernel, n_t=n_t, new_pad=new_pad),
        grid=(batch,),
        in_specs=[new_blk, new_blk, new_blk, buf_blk, buf_blk, _const_spec(bias.shape),
                  _const_spec(gatt.shape)],
        out_specs=new_blk,
        out_shape=jax.ShapeDtypeStruct((n_t, batch, 1, ATT_WIDTH), F32),
        compiler_params=_params("arbitrary"),
        name="sample_attn",
    )(q, kn, vn, kt, vt, bias, gatt)


def _post(x2d, an, cn, wout, npost, npre, wup, wdown, nmlp):
    n, d = x2d.shape
    tm = min(ROW_BLOCK, n)
    assert n % tm == 0
    row = lambda w: pl.BlockSpec((tm, w), lambda i: (i, 0))
    return pl.pallas_call(
        functools.partial(_post_kernel, ff_chunk=min(FF_CHUNK, wup.shape[1])),
        grid=(n // tm,),
        in_specs=[row(d), row(an.shape[1]), row(cn.shape[1]), _const_spec(wout.shape),
                  _const_spec((1, d)), _const_spec((1, d)), _const_spec(wup.shape),
                  _const_spec(wdown.shape), _const_spec((1, d))],
        out_specs=row(d),
        out_shape=jax.ShapeDtypeStruct((n, d), F32),
        compiler_params=_params("arbitrary"),
        name="post",
    )(x2d, an, cn, wout, npost, npre, wup, wdown, nmlp)


def kernel(x_prompt, x_sample, cache_k, cache_v, state_conv, n_att_pre, n_att_post, w_in, conv_w,
           g_att, g_conv, w_out, n_mlp_pre, n_mlp_post, w_up, w_down):
    depth = w_in.shape[0]
    batch, seq, d = x_prompt.shape
    dec_batch, dec_seq, _ = x_sample.shape
    past_len = PAST_LEN
    keep = min(MAX_SPAN, seq)
    assert keep == seq, "the prompt's window buffer is its whole key/value sequence"
    cw = conv_w.shape[2]

    yp = x_prompt.reshape(batch * seq, d)
    ys = x_sample.swapaxes(0, 1).reshape(dec_seq * dec_batch, d)
    outs = [[] for _ in range(6)]
    for l in range(depth):
        win, wout = w_in[l].astype(BF16), w_out[l].astype(BF16)
        wup, wdown = w_up[l].astype(BF16), w_down[l].astype(BF16)
        gpre, npost = n_att_pre[l][None, :], n_att_post[l][None, :]
        npre, nmlp = n_mlp_pre[l][None, :], n_mlp_post[l][None, :]
        gatt, gconv = g_att[l][None, :], g_conv[l][None, :]

        q, k, v, kt, vt, cn, cs = _proj_prompt(yp, seq, gpre, win, conv_w[l], gconv)
        an = _prompt_attn(q, k, v, gatt)
        yp = _post(yp, an, cn, wout, npost, npre, wup, wdown, nmlp)
        per_head = lambda a: a.reshape(batch, ATT_HEADS, HEAD_DIM, seq).transpose(0, 3, 1, 2)
        outs[0].append(per_head(kt))
        outs[1].append(per_head(vt))
        outs[2].append(cs)

        past = state_conv[l].swapaxes(0, 1).reshape((CONV_K - 1) * dec_batch, cw)
        qs, ks, vs, cns, css = _proj_sample(ys, past_len + jnp.arange(dec_seq), dec_batch, past,
                                            gpre, win, conv_w[l], gconv)
        tm4 = lambda a: a.reshape(dec_seq, dec_batch, 1, ATT_WIDTH)
        ans = _sample_attn(tm4(qs), tm4(ks), tm4(vs), cache_k[l], cache_v[l], past_len, gatt)
        ys = _post(ys, ans.reshape(dec_seq * dec_batch, ATT_WIDTH), cns, wout, npost, npre,
                   wup, wdown, nmlp)
        bm = lambda a: a.reshape(dec_seq, dec_batch, ATT_HEADS, HEAD_DIM).swapaxes(0, 1)
        outs[3].append(bm(ks))
        outs[4].append(bm(vs))
        outs[5].append(css.reshape(CONV_K - 1, dec_batch, cw).swapaxes(0, 1))

    y_prompt = yp.reshape(batch, seq, d)
    y_sample = ys.reshape(dec_seq, dec_batch, d).swapaxes(0, 1)
    return (y_prompt, y_sample) + tuple(jnp.stack(o) for o in outs)
```

```python
import functools

import numpy as np
import jax
import jax.numpy as jnp
from jax import lax
from jax.experimental import pallas as pl
from jax.experimental.pallas import tpu as pltpu

HEAD_DIM = 64
ATT_HEADS = 8
ATT_WIDTH = ATT_HEADS * HEAD_DIM
CONV_K = 3
DILATED_PATTERNS = ((128, 1), (512, 4), (2048, 16))
MAX_SPAN = max(w for w, _ in DILATED_PATTERNS)
PAST_LEN = 8192
ROPE_THETA = 10000.0
NORM_EPS = 1e-6

LANES = 128
HEADS_PER_SLAB = LANES // HEAD_DIM
N_SLABS = ATT_WIDTH // LANES
LSE_LANES_PER_HEAD = LANES // ATT_HEADS
MASK_VALUE = -1e30
LOG2_E = 1.4426950408889634
VMEM_LIMIT_BYTES = 56 * 1024 * 1024

ROW_BLOCK = 512
Q_BLOCK = 128
UNITS_PER_STEP = 4
FF_CHUNK = 1024

BF16 = jnp.bfloat16
F32 = jnp.float32
NT_DIMS = (((1,), (1,)), ((), ()))


def _rmsnorm(x, g):
    return x * lax.rsqrt(jnp.mean(x * x, axis=-1, keepdims=True) + NORM_EPS) * g


def _rope(x, cos, sin_lo, sin_hi):
    half = HEAD_DIM // 2
    outs = []
    for c in range(N_SLABS):
        xs = x[:, c * LANES:(c + 1) * LANES]
        outs.append(xs * cos + pltpu.roll(xs, LANES - half, 1) * sin_lo
                    + pltpu.roll(xs, half, 1) * sin_hi)
    return jnp.concatenate(outs, axis=1)


def _project(x, gpre, win_ref, cos, sin_lo, sin_hi, q_scale):
    h = _rmsnorm(x, gpre).astype(BF16)

    def cols(c):
        return jnp.dot(h, win_ref[:, c * ATT_WIDTH:(c + 1) * ATT_WIDTH],
                       preferred_element_type=F32)

    q = _rope(cols(0), cos, sin_lo, sin_hi) * q_scale
    k = _rope(cols(1), cos, sin_lo, sin_hi)
    v = cols(2)
    gate_b = cols(3)
    gated = cols(4) * cols(5)
    return q, k, v, gate_b, gated


def _proj_prompt_kernel(x_ref, gpre_ref, win_ref, cos_ref, slo_ref, shi_ref, convw_ref,
                        gconv_ref, q_ref, k_ref, v_ref, kt_ref, vt_ref, cn_ref, cs_ref,
                        hist_ref, *, tm, blocks_per_seq):
    j = pl.program_id(0) % blocks_per_seq
    row0 = pl.multiple_of(j * tm, tm)
    q, k, v, gate_b, gated = _project(
        x_ref[...], gpre_ref[...], win_ref, cos_ref[pl.ds(row0, tm), :],
        slo_ref[pl.ds(row0, tm), :], shi_ref[pl.ds(row0, tm), :], HEAD_DIM ** -0.5 * LOG2_E)
    for c in range(N_SLABS):
        q_ref[c] = q[:, c * LANES:(c + 1) * LANES]
        k_ref[c] = k[:, c * LANES:(c + 1) * LANES]
        v_ref[c] = v[:, c * LANES:(c + 1) * LANES]
    kt_ref[...] = k.T
    vt_ref[...] = v.T

    @pl.when(j == 0)
    def _():
        hist_ref[0:8, :] = jnp.zeros((8, gated.shape[1]), F32)

    @pl.when(j != 0)
    def _():
        hist_ref[0:8, :] = hist_ref[tm:tm + 8, :]

    hist_ref[8:8 + tm, :] = gated
    conv = (convw_ref[0:1, :] * hist_ref[6:6 + tm, :] + convw_ref[1:2, :] * hist_ref[7:7 + tm, :]
            + convw_ref[2:3, :] * gated)
    cn_ref[...] = _rmsnorm(gate_b * conv, gconv_ref[...]).astype(BF16)
    cs_ref[...] = gated[tm - (CONV_K - 1):tm, :]


def _proj_sample_kernel(x_ref, gpre_ref, win_ref, cos_ref, slo_ref, shi_ref, convw_ref,
                        gconv_ref, past_ref, q_ref, k_ref, v_ref, cn_ref, cs_ref, hist_ref,
                        *, rows, stride):
    q, k, v, gate_b, gated = _project(x_ref[...], gpre_ref[...], win_ref, cos_ref[...],
                                      slo_ref[...], shi_ref[...], HEAD_DIM ** -0.5)
    q_ref[...] = q
    k_ref[...] = k
    v_ref[...] = v
    npast = (CONV_K - 1) * stride
    hist_ref[0:npast, :] = past_ref[...]
    hist_ref[npast:npast + rows, :] = gated
    conv = (convw_ref[0:1, :] * hist_ref[0:rows, :]
            + convw_ref[1:2, :] * hist_ref[stride:stride + rows, :]
            + convw_ref[2:3, :] * gated)
    cn_ref[...] = _rmsnorm(gate_b * conv, gconv_ref[...]).astype(BF16)
    cs_ref[...] = hist_ref[rows:rows + npast, :]


def _stat_lane_group(c, hh):
    return (HEADS_PER_SLAB - 1 - hh) * (HEAD_DIM // LSE_LANES_PER_HEAD) + c


def _attend(load, store_o, store_stats, i, start, kv, reach, tq):
    lane = lax.broadcasted_iota(jnp.int32, (tq, LANES), 1)
    first_head = lane < HEAD_DIM
    group = lane // LSE_LANES_PER_HEAD
    qpos = i * tq + lax.broadcasted_iota(jnp.int32, (tq, kv), 0)
    kpos = start + lax.broadcasted_iota(jnp.int32, (tq, kv), 1)
    dist = qpos - kpos
    bias = jnp.where((dist >= 0) & (dist <= reach), 0.0, MASK_VALUE).astype(F32)
    max_tile = jnp.zeros((tq, LANES), F32)
    den_tile = jnp.zeros((tq, LANES), F32)
    key_lane = lax.broadcasted_iota(jnp.int32, (kv, LANES), 1)
    key_first = key_lane < HEAD_DIM
    for c in range(N_SLABS):
        q2 = load(0, c, i * tq, tq).astype(BF16)
        k2 = load(1, c, start, kv).astype(BF16)
        v2 = load(2, c, start, kv).astype(BF16)
        if kv == tq:
            zero = jnp.zeros_like(k2)
            k_both = jnp.concatenate([jnp.where(key_first, k2, zero), jnp.where(key_first, zero, k2)], 0)
            v_both = jnp.concatenate([jnp.where(key_first, v2, zero), jnp.where(key_first, zero, v2)], 0)
            s_both = lax.dot_general(q2, k_both, NT_DIMS, preferred_element_type=F32)
            probs = []
            for hh in range(HEADS_PER_SLAB):
                s = s_both[:, hh * kv:(hh + 1) * kv] + bias
                m = jnp.max(s, axis=1, keepdims=True)
                p = jnp.exp2(s - m).astype(BF16)
                den = jnp.sum(p.astype(F32), axis=1, keepdims=True)
                probs.append(p)
                mine = group == _stat_lane_group(c, hh)
                max_tile = jnp.where(mine, m, max_tile)
                den_tile = jnp.where(mine, den, den_tile)
            store_o(c, jnp.dot(jnp.concatenate(probs, axis=1), v_both, preferred_element_type=F32))
        else:
            one = jnp.ones_like(v2)
            outs = []
            for hh in range(HEADS_PER_SLAB):
                own = first_head if hh == 0 else ~first_head
                qm = jnp.where(own, q2, jnp.zeros_like(q2))
                s = lax.dot_general(qm, k2, NT_DIMS, preferred_element_type=F32) + bias
                m = jnp.max(s, axis=1, keepdims=True)
                p = jnp.exp2(s - m).astype(BF16)
                v_aug = jnp.where(key_first if hh == 0 else ~key_first, v2, one)
                r = jnp.dot(p, v_aug, preferred_element_type=F32)
                outs.append(r)
                mine = group == _stat_lane_group(c, hh)
                max_tile = jnp.where(mine, m, max_tile)
                den_tile = jnp.where(mine, r, den_tile)
            store_o(c, jnp.where(first_head, outs[0], outs[1]))
    store_stats(max_tile, den_tile)


def _prompt_attn_kernel(q_ref, k_ref, v_ref, expand_ref, gatt_ref, an_ref, o_scr, l_scr,
                        cls_scr, oc_scr, lc_scr, *, seq, tq):
    srcs = (q_ref, k_ref, v_ref)
    (near_w, near_d), (mid_w, mid_d), (far_w, far_d) = DILATED_PATTERNS
    inner = far_d // mid_d
    sub_mid, sub_far = seq // mid_d, seq // far_d
    nq_far = sub_far // tq

    def aligned(first, count):
        if isinstance(first, int):
            return pl.ds(first, count)
        return pl.ds(pl.multiple_of(first, tq), count)

    def near_unit(i, carry):
        start = jnp.maximum(i - 1, 0) * tq
        rows = aligned(i * tq, tq)

        def store_o(c, val):
            o_scr[0, c, rows, :] = val

        def store_stats(top, den):
            l_scr[0, 0, rows, :] = top
            l_scr[0, 1, rows, :] = den

        _attend(lambda a, c, first, count: srcs[a][c, aligned(first, count), :],
                store_o, store_stats, i, start, min(2 * tq, seq), near_w // near_d, tq)
        return carry

    lax.fori_loop(0, seq // tq, near_unit, 0, unroll=UNITS_PER_STEP)

    def mid_class(r, carry):
        for a in range(3):
            for c in range(N_SLABS):
                cls_scr[a, c] = srcs[a][c, pl.ds(r, sub_mid, stride=mid_d), :]

        def mid_unit(i, carry):
            start = jnp.maximum(i - 1, 0) * tq if sub_mid > tq else 0
            rows = pl.ds(r + mid_d * (i * tq), tq, stride=mid_d)

            def store_o(c, val):
                o_scr[1, c, rows, :] = val

            def store_stats(top, den):
                l_scr[1, 0, rows, :] = top
                l_scr[1, 1, rows, :] = den

            _attend(lambda a, c, first, count: cls_scr[a, c, aligned(first, count), :],
                    store_o, store_stats, i, start, min(2 * tq, sub_mid), mid_w // mid_d, tq)
            return carry

        lax.fori_loop(0, sub_mid // tq, mid_unit, 0, unroll=UNITS_PER_STEP)

        def far_unit(n, carry):
            m = n >> (nq_far.bit_length() - 1)
            i = n & (nq_far - 1)
            start = jnp.maximum(i - 1, 0) * tq if sub_far > tq else 0
            rows = pl.ds(m + inner * (i * tq), tq, stride=inner)

            def store_o(c, val):
                oc_scr[c, rows, :] = val

            def store_stats(top, den):
                lc_scr[0, rows, :] = top
                lc_scr[1, rows, :] = den

            _attend(lambda a, c, first, count:
                    cls_scr[a, c, pl.ds(m + inner * first, count, stride=inner), :],
                    store_o, store_stats, i, start, min(2 * tq, sub_far), far_w // far_d, tq)
            return carry

        lax.fori_loop(0, inner * nq_far, far_unit, 0, unroll=UNITS_PER_STEP)
        back = pl.ds(r, sub_mid, stride=mid_d)
        for c in range(N_SLABS):
            o_scr[2, c, back, :] = oc_scr[c]
        for j in range(2):
            l_scr[2, j, back, :] = lc_scr[j]
        return carry

    lax.fori_loop(0, mid_d, mid_class, 0)

    n_pat = len(DILATED_PATTERNS)

    def merge(b, carry):
        r = pl.ds(pl.multiple_of(b * tq, tq), tq)
        tops = [l_scr[ip, 0, r, :] for ip in range(n_pat)]
        top = functools.reduce(jnp.maximum, tops)
        es = [jnp.exp2(t - top) for t in tops]
        inv = 1.0 / functools.reduce(lambda x, y: x + y,
                                     [e * l_scr[ip, 1, r, :] for ip, e in enumerate(es)])
        slabs = [None] * N_SLABS
        for ip, e in enumerate(es):
            wide = jnp.dot((e * inv).astype(BF16), expand_ref[...], preferred_element_type=F32)
            for c in range(N_SLABS):
                term = wide[:, c * LANES:(c + 1) * LANES] * o_scr[ip, c, r, :]
                slabs[c] = term if slabs[c] is None else slabs[c] + term
        attn = jnp.concatenate(slabs, axis=1)
        an_ref[r, :] = _rmsnorm(attn, gatt_ref[...]).astype(BF16)
        return carry

    lax.fori_loop(0, seq // tq, merge, 0, unroll=UNITS_PER_STEP)


def _sample_attn_kernel(q_ref, kn_ref, vn_ref, kt_ref, vt_ref, bias_ref, gatt_ref, o_ref,
                        *, n_t, new_pad):
    width = ATT_WIDTH
    n_pat = len(DILATED_PATTERNS)
    per = n_t * ATT_HEADS
    sub = lax.broadcasted_iota(jnp.int32, (ATT_HEADS, width), 0)
    lane = lax.broadcasted_iota(jnp.int32, (ATT_HEADS, width), 1)
    own_head = sub == lane // HEAD_DIM
    qf = q_ref[:, 0, :]
    q_tiles = [jnp.where(own_head, jnp.broadcast_to(qf[t:t + 1, :], (ATT_HEADS, width)), 0.0)
               for t in range(n_t)]
    qbd = jnp.concatenate(q_tiles * n_pat, axis=0).astype(BF16)
    span = kt_ref.shape[-1]
    kt = kt_ref[...].reshape(width, span).astype(BF16)
    vt = vt_ref[...].reshape(width, span).astype(BF16)
    pad = jnp.zeros((new_pad - n_t, width), F32)
    kn = jnp.concatenate([kn_ref[:, 0, :], pad], axis=0).astype(BF16)
    vn = jnp.concatenate([vn_ref[:, 0, :], pad], axis=0).astype(BF16)
    s = jnp.concatenate(
        [jnp.dot(qbd, kt, preferred_element_type=F32),
         lax.dot_general(qbd, kn, NT_DIMS, preferred_element_type=F32)], axis=1) + bias_ref[...]
    m = jnp.max(s, axis=1, keepdims=True)
    p = jnp.exp(s - m)
    den = jnp.sum(p, axis=1, keepdims=True)
    lse = m + jnp.log(den)
    lses = [lse[i * per:(i + 1) * per, :] for i in range(n_pat)]
    top = functools.reduce(jnp.maximum, lses)
    es = [jnp.exp(l - top) for l in lses]
    z = functools.reduce(lambda a, b: a + b, es)
    scale = jnp.concatenate([e / z for e in es], axis=0) / den
    pw = (p * scale).astype(BF16)
    o = (lax.dot_general(pw[:, :span], vt, NT_DIMS, preferred_element_type=F32)
         + jnp.dot(pw[:, span:], vn, preferred_element_type=F32))
    for t in range(n_t):
        acc = o[t * ATT_HEADS:(t + 1) * ATT_HEADS, :]
        for i in range(1, n_pat):
            acc = acc + o[i * per + t * ATT_HEADS:i * per + (t + 1) * ATT_HEADS, :]
        attn = jnp.sum(jnp.where(own_head, acc, 0.0), axis=0, keepdims=True)
        o_ref[t, :, :] = _rmsnorm(attn, gatt_ref[...])


def _post_kernel(x_ref, an_ref, cn_ref, wout_ref, npost_ref, npre_ref, wup_ref, wdown_ref,
                 nmlp_ref, y_ref, *, ff_chunk):
    an = an_ref[...].astype(BF16)
    aw = an.shape[1]
    mixed = (jnp.dot(an, wout_ref[0:aw, :], preferred_element_type=F32)
             + jnp.dot(cn_ref[...], wout_ref[aw:, :], preferred_element_type=F32))
    x1 = x_ref[...] + _rmsnorm(mixed, npost_ref[...])
    h = _rmsnorm(x1, npre_ref[...]).astype(BF16)
    d_ff = wup_ref.shape[1]
    f = None
    for c in range(d_ff // ff_chunk):
        u = jnp.dot(h, wup_ref[:, c * ff_chunk:(c + 1) * ff_chunk], preferred_element_type=F32)
        a = jnp.square(jnp.maximum(u, 0.0)).astype(BF16)
        part = jnp.dot(a, wdown_ref[c * ff_chunk:(c + 1) * ff_chunk, :], preferred_element_type=F32)
        f = part if f is None else f + part
    y_ref[...] = x1 + _rmsnorm(f, nmlp_ref[...])


def _const_spec(shape):
    nd = len(shape)
    return pl.BlockSpec(shape, lambda *_: (0,) * nd, pipeline_mode=pl.Buffered(1))


def _rope_tables(positions):
    half = HEAD_DIM // 2
    inv = ROPE_THETA ** (-jnp.arange(half, dtype=F32) * 2.0 / HEAD_DIM)
    ang = positions.astype(F32)[:, None] * inv[None, :]
    cos, sin, zero = jnp.cos(ang), jnp.sin(ang), jnp.zeros_like(ang)
    reps = LANES // HEAD_DIM
    cos_t = jnp.tile(jnp.concatenate([cos, cos], axis=1), (1, reps))
    sin_lo = jnp.tile(jnp.concatenate([-sin, zero], axis=1), (1, reps))
    sin_hi = jnp.tile(jnp.concatenate([zero, sin], axis=1), (1, reps))
    return cos_t, sin_lo, sin_hi


def _params(*semantics):
    return pltpu.CompilerParams(dimension_semantics=semantics, vmem_limit_bytes=VMEM_LIMIT_BYTES)


def _proj_prompt(x2d, seq, gpre, win, convw, gconv):
    n, d = x2d.shape
    tm = min(ROW_BLOCK, seq)
    assert seq % tm == 0 and tm % 8 == 0
    bps = seq // tm
    batch = n // seq
    cw = convw.shape[1]
    tables = _rope_tables(jnp.arange(seq))
    row = lambda w: pl.BlockSpec((tm, w), lambda i: (i, 0))
    slabbed = pl.BlockSpec((None, N_SLABS, tm, LANES), lambda i: (i // bps, 0, i % bps, 0))
    transposed = pl.BlockSpec((None, ATT_WIDTH, tm), lambda i: (i // bps, 0, i % bps))
    slab_shape = jax.ShapeDtypeStruct((batch, N_SLABS, seq, LANES), F32)
    t_shape = jax.ShapeDtypeStruct((batch, ATT_WIDTH, seq), F32)
    out_shape = (slab_shape, slab_shape, slab_shape, t_shape, t_shape,
                 jax.ShapeDtypeStruct((n, cw), BF16),
                 jax.ShapeDtypeStruct((batch, CONV_K - 1, cw), F32))
    return pl.pallas_call(
        functools.partial(_proj_prompt_kernel, tm=tm, blocks_per_seq=bps),
        grid=(n // tm,),
        in_specs=[row(d), _const_spec((1, d)), _const_spec(win.shape)]
        + [_const_spec((seq, LANES))] * 3 + [_const_spec(convw.shape), _const_spec((1, cw))],
        out_specs=[slabbed, slabbed, slabbed, transposed, transposed, row(cw),
                   pl.BlockSpec((None, CONV_K - 1, cw), lambda i: (i // bps, 0, 0))],
        out_shape=out_shape,
        scratch_shapes=[pltpu.VMEM((tm + 8, cw), F32)],
        compiler_params=_params("arbitrary"),
        name="proj_prompt",
    )(x2d, gpre, win, *tables, convw, gconv)


def _proj_sample(x2d, positions, stride, past2d, gpre, win, convw, gconv):
    rows, d = x2d.shape
    cw = convw.shape[1]
    tables = [jnp.repeat(t, stride, axis=0) for t in _rope_tables(positions)]
    npast = (CONV_K - 1) * stride
    full = lambda shape: pl.BlockSpec(shape, lambda i: (0,) * len(shape))
    out_shape = (
        jax.ShapeDtypeStruct((rows, ATT_WIDTH), F32), jax.ShapeDtypeStruct((rows, ATT_WIDTH), F32),
        jax.ShapeDtypeStruct((rows, ATT_WIDTH), F32), jax.ShapeDtypeStruct((rows, cw), BF16),
        jax.ShapeDtypeStruct((npast, cw), F32))
    return pl.pallas_call(
        functools.partial(_proj_sample_kernel, rows=rows, stride=stride),
        grid=(1,),
        in_specs=[full((rows, d)), full((1, d)), full(win.shape)] + [full((rows, LANES))] * 3
        + [full(convw.shape), full((1, cw)), full((npast, cw))],
        out_specs=[full((rows, ATT_WIDTH))] * 3 + [full((rows, cw)), full((npast, cw))],
        out_shape=out_shape,
        scratch_shapes=[pltpu.VMEM((rows + npast, cw), F32)],
        compiler_params=_params("arbitrary"),
        name="proj_sample",
    )(x2d, gpre, win, *tables, convw, gconv, past2d)


def _prompt_attn(q, k, v, gatt):
    batch, _, seq, _ = q.shape
    tq = Q_BLOCK
    (_, near_d), (_, mid_d), (_, far_d) = DILATED_PATTERNS
    assert near_d == 1 and far_d % mid_d == 0
    for window, dil in DILATED_PATTERNS:
        sub = seq // dil
        assert seq % dil == 0 and sub % tq == 0 and window % dil == 0 and window // dil <= tq
        assert (sub // tq) & (sub // tq - 1) == 0
    sub_mid = seq // mid_d
    col_head = np.arange(ATT_WIDTH) // HEAD_DIM
    col_group = np.array([_stat_lane_group(h // HEADS_PER_SLAB, h % HEADS_PER_SLAB) for h in col_head])
    expand = jnp.asarray(np.arange(LANES)[:, None] == col_group[None, :] * LSE_LANES_PER_HEAD, BF16)
    n_pat = len(DILATED_PATTERNS)
    blk = pl.BlockSpec((None, N_SLABS, seq, LANES), lambda b: (b, 0, 0, 0))
    return pl.pallas_call(
        functools.partial(_prompt_attn_kernel, seq=seq, tq=tq),
        grid=(batch,),
        in_specs=[blk, blk, blk, _const_spec(expand.shape), _const_spec(gatt.shape)],
        out_specs=pl.BlockSpec((seq, ATT_WIDTH), lambda b: (b, 0)),
        out_shape=jax.ShapeDtypeStruct((batch * seq, ATT_WIDTH), BF16),
        scratch_shapes=[pltpu.VMEM((n_pat, N_SLABS, seq, LANES), F32),
                        pltpu.VMEM((n_pat, 2, seq, LANES), F32),
                        pltpu.VMEM((3, N_SLABS, sub_mid, LANES), F32),
                        pltpu.VMEM((N_SLABS, sub_mid, LANES), F32),
                        pltpu.VMEM((2, sub_mid, LANES), F32)],
        compiler_params=_params("arbitrary"),
        name="prompt_attn",
    )(q, k, v, expand, gatt)


def _sample_bias(n_t, past_len, span, new_pad):
    big = 1 << 30
    rel = np.concatenate([np.arange(span) - span, np.arange(n_t),
                          np.full((new_pad - n_t,), -big)])
    rows = []
    for window, dil in DILATED_PATTERNS:
        for t in range(n_t):
            dist = t - rel
            ok = (dist >= 0) & (dist <= window) & (dist % dil == 0) & (past_len + rel >= 0)
            rows += [np.where(ok, 0.0, MASK_VALUE)] * ATT_HEADS
    return jnp.asarray(np.stack(rows), F32)


def _sample_attn(q, kn, vn, cache_k, cache_v, past_len, gatt):
    n_t, batch = q.shape[0], q.shape[1]
    span = cache_k.shape[1]
    assert span == min(MAX_SPAN, past_len)
    new_pad = -(-n_t // LANES) * LANES
    bias = _sample_bias(n_t, past_len, span, new_pad)
    kt = jnp.transpose(cache_k, (0, 2, 3, 1))
    vt = jnp.transpose(cache_v, (0, 2, 3, 1))
    new_blk = pl.BlockSpec((n_t, None, 1, ATT_WIDTH), lambda b: (0, b, 0, 0))
    buf_blk = pl.BlockSpec((None, ATT_HEADS, HEAD_DIM, span), lambda b: (b, 0, 0, 0))
    return pl.pallas_call(
        functools.partial(_sample_attn_kernel, n_t=n_t, new_pad=new_pad),
        grid=(batch,),
        in_specs=[new_blk, new_blk, new_blk, buf_blk, buf_blk, _const_spec(bias.shape),
                  _const_spec(gatt.shape)],
        out_specs=new_blk,
        out_shape=jax.ShapeDtypeStruct((n_t, batch, 1, ATT_WIDTH), F32),
        compiler_params=_params("arbitrary"),
        name="sample_attn",
    )(q, kn, vn, kt, vt, bias, gatt)


def _post(x2d, an, cn, wout, npost, npre, wup, wdown, nmlp):
    n, d = x2d.shape
    tm = min(ROW_BLOCK, n)
    assert n % tm == 0
    row = lambda w: pl.BlockSpec((tm, w), lambda i: (i, 0))
    return pl.pallas_call(
        functools.partial(_post_kernel, ff_chunk=min(FF_CHUNK, wup.shape[1])),
        grid=(n // tm,),
        in_specs=[row(d), row(an.shape[1]), row(cn.shape[1]), _const_spec(wout.shape),
                  _const_spec((1, d)), _const_spec((1, d)), _const_spec(wup.shape),
                  _const_spec(wdown.shape), _const_spec((1, d))],
        out_specs=row(d),
        out_shape=jax.ShapeDtypeStruct((n, d), F32),
        compiler_params=_params("arbitrary"),
        name="post",
    )(x2d, an, cn, wout, npost, npre, wup, wdown, nmlp)


def kernel(x_prompt, x_sample, cache_k, cache_v, state_conv, n_att_pre, n_att_post, w_in, conv_w,
           g_att, g_conv, w_out, n_mlp_pre, n_mlp_post, w_up, w_down):
    depth = w_in.shape[0]
    batch, seq, d = x_prompt.shape
    dec_batch, dec_seq, _ = x_sample.shape
    past_len = PAST_LEN
    keep = min(MAX_SPAN, seq)
    assert keep == seq, "the prompt's window buffer is its whole key/value sequence"
    cw = conv_w.shape[2]

    yp = x_prompt.reshape(batch * seq, d)
    ys = x_sample.swapaxes(0, 1).reshape(dec_seq * dec_batch, d)
    outs = [[] for _ in range(6)]
    for l in range(depth):
        win, wout = w_in[l].astype(BF16), w_out[l].astype(BF16)
        wup, wdown = w_up[l].astype(BF16), w_down[l].astype(BF16)
        gpre, npost = n_att_pre[l][None, :], n_att_post[l][None, :]
        npre, nmlp = n_mlp_pre[l][None, :], n_mlp_post[l][None, :]
        gatt, gconv = g_att[l][None, :], g_conv[l][None, :]

        q, k, v, kt, vt, cn, cs = _proj_prompt(yp, seq, gpre, win, conv_w[l], gconv)
        an = _prompt_attn(q, k, v, gatt)
        yp = _post(yp, an, cn, wout, npost, npre, wup, wdown, nmlp)
        per_head = lambda a: a.reshape(batch, ATT_HEADS, HEAD_DIM, seq).transpose(0, 3, 1, 2)
        outs[0].append(per_head(kt))
        outs[1].append(per_head(vt))
        outs[2].append(cs)

        past = state_conv[l].swapaxes(0, 1).reshape((CONV_K - 1) * dec_batch, cw)
        qs, ks, vs, cns, css = _proj_sample(ys, past_len + jnp.arange(dec_seq), dec_batch, past,
                                            gpre, win, conv_w[l], gconv)
        tm4 = lambda a: a.reshape(dec_seq, dec_batch, 1, ATT_WIDTH)
        ans = _sample_attn(tm4(qs), tm4(ks), tm4(vs), cache_k[l], cache_v[l], past_len, gatt)
        ys = _post(ys, ans.reshape(dec_seq * dec_batch, ATT_WIDTH), cns, wout, npost, npre,
                   wup, wdown, nmlp)
        bm = lambda a: a.reshape(dec_seq, dec_batch, ATT_HEADS, HEAD_DIM).swapaxes(0, 1)
        outs[3].append(bm(ks))
        outs[4].append(bm(vs))
        outs[5].append(css.reshape(CONV_K - 1, dec_batch, cw).swapaxes(0, 1))

    y_prompt = yp.reshape(batch, seq, d)
    y_sample = ys.reshape(dec_seq, dec_batch, d).swapaxes(0, 1)
    return (y_prompt, y_sample) + tuple(jnp.stack(o) for o in outs)
```

```python
import functools

import numpy as np
import jax
import jax.numpy as jnp
from jax import lax
from jax.experimental import pallas as pl
from jax.experimental.pallas import tpu as pltpu

HEAD_DIM = 64
ATT_HEADS = 8
ATT_WIDTH = ATT_HEADS * HEAD_DIM
CONV_K = 3
DILATED_PATTERNS = ((128, 1), (512, 4), (2048, 16))
MAX_SPAN = max(w for w, _ in DILATED_PATTERNS)
PAST_LEN = 8192
ROPE_THETA = 10000.0
NORM_EPS = 1e-6

LANES = 128
HEADS_PER_SLAB = LANES // HEAD_DIM
N_SLABS = ATT_WIDTH // LANES
LSE_LANES_PER_HEAD = LANES // ATT_HEADS
MASK_VALUE = -1e30
LOG2_E = 1.4426950408889634
VMEM_LIMIT_BYTES = 56 * 1024 * 1024

ROW_BLOCK = 512
Q_BLOCK = 128
UNITS_PER_STEP = 8
FF_CHUNK = 1024
SAMPLE_SEQS_PER_STEP = 2

BF16 = jnp.bfloat16
F32 = jnp.float32
NT_DIMS = (((1,), (1,)), ((), ()))


def _rmsnorm(x, g):
    return x * lax.rsqrt(jnp.mean(x * x, axis=-1, keepdims=True) + NORM_EPS) * g


def _rope(x, cos, sin_lo, sin_hi):
    half = HEAD_DIM // 2
    outs = []
    for c in range(N_SLABS):
        xs = x[:, c * LANES:(c + 1) * LANES]
        outs.append(xs * cos + pltpu.roll(xs, LANES - half, 1) * sin_lo
                    + pltpu.roll(xs, half, 1) * sin_hi)
    return jnp.concatenate(outs, axis=1)


def _project(x, gpre, win_ref, cos, sin_lo, sin_hi, q_scale):
    h = _rmsnorm(x, gpre).astype(BF16)

    def cols(c):
        return jnp.dot(h, win_ref[:, c * ATT_WIDTH:(c + 1) * ATT_WIDTH],
                       preferred_element_type=F32)

    q = _rope(cols(0), cos, sin_lo, sin_hi) * q_scale
    k = _rope(cols(1), cos, sin_lo, sin_hi)
    v = cols(2)
    gate_b = cols(3)
    gated = cols(4) * cols(5)
    return q, k, v, gate_b, gated


def _proj_prompt_kernel(x_ref, gpre_ref, win_ref, cos_ref, slo_ref, shi_ref, convw_ref,
                        gconv_ref, q_ref, k_ref, v_ref, kt_ref, vt_ref, cn_ref, cs_ref,
                        hist_ref, *, tm, blocks_per_seq):
    j = pl.program_id(0) % blocks_per_seq
    row0 = pl.multiple_of(j * tm, tm)
    q, k, v, gate_b, gated = _project(
        x_ref[...], gpre_ref[...], win_ref, cos_ref[pl.ds(row0, tm), :],
        slo_ref[pl.ds(row0, tm), :], shi_ref[pl.ds(row0, tm), :], HEAD_DIM ** -0.5 * LOG2_E)
    for c in range(N_SLABS):
        q_ref[c] = q[:, c * LANES:(c + 1) * LANES]
        k_ref[c] = k[:, c * LANES:(c + 1) * LANES]
        v_ref[c] = v[:, c * LANES:(c + 1) * LANES]
    kt_ref[...] = k.T
    vt_ref[...] = v.T

    @pl.when(j == 0)
    def _():
        hist_ref[0:8, :] = jnp.zeros((8, gated.shape[1]), F32)

    @pl.when(j != 0)
    def _():
        hist_ref[0:8, :] = hist_ref[tm:tm + 8, :]

    hist_ref[8:8 + tm, :] = gated
    conv = (convw_ref[0:1, :] * hist_ref[6:6 + tm, :] + convw_ref[1:2, :] * hist_ref[7:7 + tm, :]
            + convw_ref[2:3, :] * gated)
    cn_ref[...] = _rmsnorm(gate_b * conv, gconv_ref[...]).astype(BF16)
    cs_ref[...] = gated[tm - (CONV_K - 1):tm, :]


def _proj_sample_kernel(x_ref, gpre_ref, win_ref, cos_ref, slo_ref, shi_ref, convw_ref,
                        gconv_ref, past_ref, q_ref, k_ref, v_ref, cn_ref, cs_ref, hist_ref,
                        *, rows, stride):
    q, k, v, gate_b, gated = _project(x_ref[...], gpre_ref[...], win_ref, cos_ref[...],
                                      slo_ref[...], shi_ref[...], HEAD_DIM ** -0.5)
    q_ref[...] = q
    k_ref[...] = k
    v_ref[...] = v
    npast = (CONV_K - 1) * stride
    hist_ref[0:npast, :] = past_ref[...]
    hist_ref[npast:npast + rows, :] = gated
    conv = (convw_ref[0:1, :] * hist_ref[0:rows, :]
            + convw_ref[1:2, :] * hist_ref[stride:stride + rows, :]
            + convw_ref[2:3, :] * gated)
    cn_ref[...] = _rmsnorm(gate_b * conv, gconv_ref[...]).astype(BF16)
    cs_ref[...] = hist_ref[rows:rows + npast, :]


def _stat_lane_group(c, hh):
    return (HEADS_PER_SLAB - 1 - hh) * (HEAD_DIM // LSE_LANES_PER_HEAD) + c


def _attend(load, store_o, store_stats, i, start, kv, reach, tq):
    lane = lax.broadcasted_iota(jnp.int32, (tq, LANES), 1)
    first_head = lane < HEAD_DIM
    group = lane // LSE_LANES_PER_HEAD
    qpos = i * tq + lax.broadcasted_iota(jnp.int32, (tq, kv), 0)
    kpos = start + lax.broadcasted_iota(jnp.int32, (tq, kv), 1)
    dist = qpos - kpos
    bias = jnp.where((dist >= 0) & (dist <= reach), 0.0, MASK_VALUE).astype(F32)
    max_tile = jnp.zeros((tq, LANES), F32)
    den_tile = jnp.zeros((tq, LANES), F32)
    key_lane = lax.broadcasted_iota(jnp.int32, (kv, LANES), 1)
    key_first = key_lane < HEAD_DIM
    for c in range(N_SLABS):
        q2 = load(0, c, i * tq, tq).astype(BF16)
        k2 = load(1, c, start, kv).astype(BF16)
        v2 = load(2, c, start, kv).astype(BF16)
        if kv == tq:
            zero = jnp.zeros_like(k2)
            k_both = jnp.concatenate([jnp.where(key_first, k2, zero), jnp.where(key_first, zero, k2)], 0)
            v_both = jnp.concatenate([jnp.where(key_first, v2, zero), jnp.where(key_first, zero, v2)], 0)
            s_both = lax.dot_general(q2, k_both, NT_DIMS, preferred_element_type=F32)
            probs = []
            for hh in range(HEADS_PER_SLAB):
                s = s_both[:, hh * kv:(hh + 1) * kv] + bias
                m = jnp.max(s, axis=1, keepdims=True)
                p = jnp.exp2(s - m).astype(BF16)
                den = jnp.sum(p.astype(F32), axis=1, keepdims=True)
                probs.append(p)
                mine = group == _stat_lane_group(c, hh)
                max_tile = jnp.where(mine, m, max_tile)
                den_tile = jnp.where(mine, den, den_tile)
            store_o(c, jnp.dot(jnp.concatenate(probs, axis=1), v_both, preferred_element_type=F32))
        else:
            one = jnp.ones_like(v2)
            outs = []
            for hh in range(HEADS_PER_SLAB):
                own = first_head if hh == 0 else ~first_head
                qm = jnp.where(own, q2, jnp.zeros_like(q2))
                s = lax.dot_general(qm, k2, NT_DIMS, preferred_element_type=F32) + bias
                m = jnp.max(s, axis=1, keepdims=True)
                p = jnp.exp2(s - m).astype(BF16)
                v_aug = jnp.where(key_first if hh == 0 else ~key_first, v2, one)
                r = jnp.dot(p, v_aug, preferred_element_type=F32)
                outs.append(r)
                mine = group == _stat_lane_group(c, hh)
                max_tile = jnp.where(mine, m, max_tile)
                den_tile = jnp.where(mine, r, den_tile)
            store_o(c, jnp.where(first_head, outs[0], outs[1]))
    store_stats(max_tile, den_tile)


def _prompt_attn_kernel(q_ref, k_ref, v_ref, expand_ref, gatt_ref, an_ref, o_scr, l_scr,
                        cls_scr, oc_scr, lc_scr, *, seq, tq):
    srcs = (q_ref, k_ref, v_ref)
    (near_w, near_d), (mid_w, mid_d), (far_w, far_d) = DILATED_PATTERNS
    inner = far_d // mid_d
    sub_mid, sub_far = seq // mid_d, seq // far_d
    nq_far = sub_far // tq

    def aligned(first, count):
        if isinstance(first, int):
            return pl.ds(first, count)
        return pl.ds(pl.multiple_of(first, tq), count)

    def near_unit(i, carry):
        start = jnp.maximum(i - 1, 0) * tq
        rows = aligned(i * tq, tq)

        def store_o(c, val):
            o_scr[0, c, rows, :] = val

        def store_stats(top, den):
            l_scr[0, 0, rows, :] = top
            l_scr[0, 1, rows, :] = den

        _attend(lambda a, c, first, count: srcs[a][c, aligned(first, count), :],
                store_o, store_stats, i, start, min(2 * tq, seq), near_w // near_d, tq)
        return carry

    lax.fori_loop(0, seq // tq, near_unit, 0, unroll=UNITS_PER_STEP)

    def mid_class(r, carry):
        for a in range(3):
            for c in range(N_SLABS):
                cls_scr[a, c] = srcs[a][c, pl.ds(r, sub_mid, stride=mid_d), :]

        def mid_unit(i, carry):
            start = jnp.maximum(i - 1, 0) * tq if sub_mid > tq else 0
            rows = pl.ds(r + mid_d * (i * tq), tq, stride=mid_d)

            def store_o(c, val):
                o_scr[1, c, rows, :] = val

            def store_stats(top, den):
                l_scr[1, 0, rows, :] = top
                l_scr[1, 1, rows, :] = den

            _attend(lambda a, c, first, count: cls_scr[a, c, aligned(first, count), :],
                    store_o, store_stats, i, start, min(2 * tq, sub_mid), mid_w // mid_d, tq)
            return carry

        def far_unit(n, carry):
            m = n >> (nq_far.bit_length() - 1)
            i = n & (nq_far - 1)
            start = jnp.maximum(i - 1, 0) * tq if sub_far > tq else 0
            rows = pl.ds(m + inner * (i * tq), tq, stride=inner)

            def store_o(c, val):
                oc_scr[c, rows, :] = val

            def store_stats(top, den):
                lc_scr[0, rows, :] = top
                lc_scr[1, rows, :] = den

            _attend(lambda a, c, first, count:
                    cls_scr[a, c, pl.ds(m + inner * first, count, stride=inner), :],
                    store_o, store_stats, i, start, min(2 * tq, sub_far), far_w // far_d, tq)
            return carry

        lax.fori_loop(0, sub_mid // tq, lambda n, carry: far_unit(n, mid_unit(n, carry)), 0,
                      unroll=UNITS_PER_STEP)
        back = pl.ds(r, sub_mid, stride=mid_d)
        for c in range(N_SLABS):
            o_scr[2, c, back, :] = oc_scr[c]
        for j in range(2):
            l_scr[2, j, back, :] = lc_scr[j]
        return carry

    lax.fori_loop(0, mid_d, mid_class, 0)

    n_pat = len(DILATED_PATTERNS)

    def merge(b, carry):
        r = pl.ds(pl.multiple_of(b * tq, tq), tq)
        tops = [l_scr[ip, 0, r, :] for ip in range(n_pat)]
        top = functools.reduce(jnp.maximum, tops)
        es = [jnp.exp2(t - top) for t in tops]
        inv = 1.0 / functools.reduce(lambda x, y: x + y,
                                     [e * l_scr[ip, 1, r, :] for ip, e in enumerate(es)])
        slabs = [None] * N_SLABS
        for ip, e in enumerate(es):
            wide = jnp.dot((e * inv).astype(BF16), expand_ref[...], preferred_element_type=F32)
            for c in range(N_SLABS):
                term = wide[:, c * LANES:(c + 1) * LANES] * o_scr[ip, c, r, :]
                slabs[c] = term if slabs[c] is None else slabs[c] + term
        attn = jnp.concatenate(slabs, axis=1)
        an_ref[r, :] = _rmsnorm(attn, gatt_ref[...]).astype(BF16)
        return carry

    lax.fori_loop(0, seq // tq, merge, 0, unroll=UNITS_PER_STEP)


def _sample_attn_kernel(q_ref, kn_ref, vn_ref, kt_ref, vt_ref, bias_ref, gatt_ref, o_ref,
                        *, n_t, new_pad):
    width = ATT_WIDTH
    n_pat = len(DILATED_PATTERNS)
    per = n_t * ATT_HEADS
    sub = lax.broadcasted_iota(jnp.int32, (ATT_HEADS, width), 0)
    lane = lax.broadcasted_iota(jnp.int32, (ATT_HEADS, width), 1)
    own_head = sub == lane // HEAD_DIM
    span = kt_ref.shape[-1]
    pad = jnp.zeros((new_pad - n_t, width), F32)
    for j in range(kt_ref.shape[0]):
        qf = q_ref[:, j, 0, :]
        q_tiles = [jnp.where(own_head, jnp.broadcast_to(qf[t:t + 1, :], (ATT_HEADS, width)), 0.0)
                   for t in range(n_t)]
        qbd = jnp.concatenate(q_tiles * n_pat, axis=0).astype(BF16)
        kt = kt_ref[j].reshape(width, span).astype(BF16)
        vt = vt_ref[j].reshape(width, span).astype(BF16)
        kn = jnp.concatenate([kn_ref[:, j, 0, :], pad], axis=0).astype(BF16)
        vn = jnp.concatenate([vn_ref[:, j, 0, :], pad], axis=0).astype(BF16)
        s = jnp.concatenate(
            [jnp.dot(qbd, kt, preferred_element_type=F32),
             lax.dot_general(qbd, kn, NT_DIMS, preferred_element_type=F32)], axis=1) + bias_ref[...]
        m = jnp.max(s, axis=1, keepdims=True)
        p = jnp.exp(s - m)
        den = jnp.sum(p, axis=1, keepdims=True)
        lse = m + jnp.log(den)
        lses = [lse[i * per:(i + 1) * per, :] for i in range(n_pat)]
        top = functools.reduce(jnp.maximum, lses)
        es = [jnp.exp(l - top) for l in lses]
        z = functools.reduce(lambda a, b: a + b, es)
        scale = jnp.concatenate([e / z for e in es], axis=0) / den
        pw = (p * scale).astype(BF16)
        o = (lax.dot_general(pw[:, :span], vt, NT_DIMS, preferred_element_type=F32)
             + jnp.dot(pw[:, span:], vn, preferred_element_type=F32))
        for t in range(n_t):
            acc = o[t * ATT_HEADS:(t + 1) * ATT_HEADS, :]
            for i in range(1, n_pat):
                acc = acc + o[i * per + t * ATT_HEADS:i * per + (t + 1) * ATT_HEADS, :]
            attn = jnp.sum(jnp.where(own_head, acc, 0.0), axis=0, keepdims=True)
            o_ref[t, j, :, :] = _rmsnorm(attn, gatt_ref[...])


def _post_kernel(x_ref, an_ref, cn_ref, wout_ref, npost_ref, npre_ref, wup_ref, wdown_ref,
                 nmlp_ref, y_ref, *, ff_chunk):
    an = an_ref[...].astype(BF16)
    aw = an.shape[1]
    mixed = (jnp.dot(an, wout_ref[0:aw, :], preferred_element_type=F32)
             + jnp.dot(cn_ref[...], wout_ref[aw:, :], preferred_element_type=F32))
    x1 = x_ref[...] + _rmsnorm(mixed, npost_ref[...])
    h = _rmsnorm(x1, npre_ref[...]).astype(BF16)
    d_ff = wup_ref.shape[1]
    f = None
    for c in range(d_ff // ff_chunk):
        u = jnp.dot(h, wup_ref[:, c * ff_chunk:(c + 1) * ff_chunk], preferred_element_type=F32)
        a = jnp.square(jnp.maximum(u, 0.0)).astype(BF16)
        part = jnp.dot(a, wdown_ref[c * ff_chunk:(c + 1) * ff_chunk, :], preferred_element_type=F32)
        f = part if f is None else f + part
    y_ref[...] = x1 + _rmsnorm(f, nmlp_ref[...])


def _const_spec(shape):
    nd = len(shape)
    return pl.BlockSpec(shape, lambda *_: (0,) * nd, pipeline_mode=pl.Buffered(1))


def _rope_tables(positions):
    half = HEAD_DIM // 2
    inv = ROPE_THETA ** (-jnp.arange(half, dtype=F32) * 2.0 / HEAD_DIM)
    ang = positions.astype(F32)[:, None] * inv[None, :]
    cos, sin, zero = jnp.cos(ang), jnp.sin(ang), jnp.zeros_like(ang)
    reps = LANES // HEAD_DIM
    cos_t = jnp.tile(jnp.concatenate([cos, cos], axis=1), (1, reps))
    sin_lo = jnp.tile(jnp.concatenate([-sin, zero], axis=1), (1, reps))
    sin_hi = jnp.tile(jnp.concatenate([zero, sin], axis=1), (1, reps))
    return cos_t, sin_lo, sin_hi


def _params(*semantics):
    return pltpu.CompilerParams(dimension_semantics=semantics, vmem_limit_bytes=VMEM_LIMIT_BYTES)


def _proj_prompt(x2d, seq, gpre, win, convw, gconv):
    n, d = x2d.shape
    tm = min(ROW_BLOCK, seq)
    assert seq % tm == 0 and tm % 8 == 0
    bps = seq // tm
    batch = n // seq
    cw = convw.shape[1]
    tables = _rope_tables(jnp.arange(seq))
    row = lambda w: pl.BlockSpec((tm, w), lambda i: (i, 0))
    slabbed = pl.BlockSpec((None, N_SLABS, tm, LANES), lambda i: (i // bps, 0, i % bps, 0))
    transposed = pl.BlockSpec((None, ATT_WIDTH, tm), lambda i: (i // bps, 0, i % bps))
    slab_shape = jax.ShapeDtypeStruct((batch, N_SLABS, seq, LANES), F32)
    t_shape = jax.ShapeDtypeStruct((batch, ATT_WIDTH, seq), F32)
    out_shape = (slab_shape, slab_shape, slab_shape, t_shape, t_shape,
                 jax.ShapeDtypeStruct((n, cw), BF16),
                 jax.ShapeDtypeStruct((batch, CONV_K - 1, cw), F32))
    return pl.pallas_call(
        functools.partial(_proj_prompt_kernel, tm=tm, blocks_per_seq=bps),
        grid=(n // tm,),
        in_specs=[row(d), _const_spec((1, d)), _const_spec(win.shape)]
        + [_const_spec((seq, LANES))] * 3 + [_const_spec(convw.shape), _const_spec((1, cw))],
        out_specs=[slabbed, slabbed, slabbed, transposed, transposed, row(cw),
                   pl.BlockSpec((None, CONV_K - 1, cw), lambda i: (i // bps, 0, 0))],
        out_shape=out_shape,
        scratch_shapes=[pltpu.VMEM((tm + 8, cw), F32)],
        compiler_params=_params("arbitrary"),
        name="proj_prompt",
    )(x2d, gpre, win, *tables, convw, gconv)


def _proj_sample(x2d, positions, stride, past2d, gpre, win, convw, gconv):
    rows, d = x2d.shape
    cw = convw.shape[1]
    tables = [jnp.repeat(t, stride, axis=0) for t in _rope_tables(positions)]
    npast = (CONV_K - 1) * stride
    full = lambda shape: pl.BlockSpec(shape, lambda i: (0,) * len(shape))
    out_shape = (
        jax.ShapeDtypeStruct((rows, ATT_WIDTH), F32), jax.ShapeDtypeStruct((rows, ATT_WIDTH), F32),
        jax.ShapeDtypeStruct((rows, ATT_WIDTH), F32), jax.ShapeDtypeStruct((rows, cw), BF16),
        jax.ShapeDtypeStruct((npast, cw), F32))
    return pl.pallas_call(
        functools.partial(_proj_sample_kernel, rows=rows, stride=stride),
        grid=(1,),
        in_specs=[full((rows, d)), full((1, d)), full(win.shape)] + [full((rows, LANES))] * 3
        + [full(convw.shape), full((1, cw)), full((npast, cw))],
        out_specs=[full((rows, ATT_WIDTH))] * 3 + [full((rows, cw)), full((npast, cw))],
        out_shape=out_shape,
        scratch_shapes=[pltpu.VMEM((rows + npast, cw), F32)],
        compiler_params=_params("arbitrary"),
        name="proj_sample",
    )(x2d, gpre, win, *tables, convw, gconv, past2d)


def _prompt_attn(q, k, v, gatt):
    batch, _, seq, _ = q.shape
    tq = Q_BLOCK
    (_, near_d), (_, mid_d), (_, far_d) = DILATED_PATTERNS
    assert near_d == 1 and far_d % mid_d == 0
    for window, dil in DILATED_PATTERNS:
        sub = seq // dil
        assert seq % dil == 0 and sub % tq == 0 and window % dil == 0 and window // dil <= tq
        assert (sub // tq) & (sub // tq - 1) == 0
    sub_mid = seq // mid_d
    col_head = np.arange(ATT_WIDTH) // HEAD_DIM
    col_group = np.array([_stat_lane_group(h // HEADS_PER_SLAB, h % HEADS_PER_SLAB) for h in col_head])
    expand = jnp.asarray(np.arange(LANES)[:, None] == col_group[None, :] * LSE_LANES_PER_HEAD, BF16)
    n_pat = len(DILATED_PATTERNS)
    blk = pl.BlockSpec((None, N_SLABS, seq, LANES), lambda b: (b, 0, 0, 0))
    return pl.pallas_call(
        functools.partial(_prompt_attn_kernel, seq=seq, tq=tq),
        grid=(batch,),
        in_specs=[blk, blk, blk, _const_spec(expand.shape), _const_spec(gatt.shape)],
        out_specs=pl.BlockSpec((seq, ATT_WIDTH), lambda b: (b, 0)),
        out_shape=jax.ShapeDtypeStruct((batch * seq, ATT_WIDTH), BF16),
        scratch_shapes=[pltpu.VMEM((n_pat, N_SLABS, seq, LANES), F32),
                        pltpu.VMEM((n_pat, 2, seq, LANES), F32),
                        pltpu.VMEM((3, N_SLABS, sub_mid, LANES), F32),
                        pltpu.VMEM((N_SLABS, sub_mid, LANES), F32),
                        pltpu.VMEM((2, sub_mid, LANES), F32)],
        compiler_params=_params("arbitrary"),
        name="prompt_attn",
    )(q, k, v, expand, gatt)


def _sample_bias(n_t, past_len, span, new_pad):
    big = 1 << 30
    rel = np.concatenate([np.arange(span) - span, np.arange(n_t),
                          np.full((new_pad - n_t,), -big)])
    rows = []
    for window, dil in DILATED_PATTERNS:
        for t in range(n_t):
            dist = t - rel
            ok = (dist >= 0) & (dist <= window) & (dist % dil == 0) & (past_len + rel >= 0)
            rows += [np.where(ok, 0.0, MASK_VALUE)] * ATT_HEADS
    return jnp.asarray(np.stack(rows), F32)


def _sample_attn(q, kn, vn, cache_k, cache_v, past_len, gatt):
    n_t, batch = q.shape[0], q.shape[1]
    span = cache_k.shape[1]
    assert span == min(MAX_SPAN, past_len)
    new_pad = -(-n_t // LANES) * LANES
    bias = _sample_bias(n_t, past_len, span, new_pad)
    kt = jnp.transpose(cache_k, (0, 2, 3, 1))
    vt = jnp.transpose(cache_v, (0, 2, 3, 1))
    bb = SAMPLE_SEQS_PER_STEP
    assert batch % bb == 0
    new_blk = pl.BlockSpec((n_t, bb, 1, ATT_WIDTH), lambda b: (0, b, 0, 0))
    buf_blk = pl.BlockSpec((bb, ATT_HEADS, HEAD_DIM, span), lambda b: (b, 0, 0, 0))
    return pl.pallas_call(
        functools.partial(_sample_attn_kernel, n_t=n_t, new_pad=new_pad),
        grid=(batch // bb,),
        in_specs=[new_blk, new_blk, new_blk, buf_blk, buf_blk, _const_spec(bias.shape),
                  _const_spec(gatt.shape)],
        out_specs=new_blk,
        out_shape=jax.ShapeDtypeStruct((n_t, batch, 1, ATT_WIDTH), F32),
        compiler_params=_params("arbitrary"),
        name="sample_attn",
    )(q, kn, vn, kt, vt, bias, gatt)


def _post(x2d, an, cn, wout, npost, npre, wup, wdown, nmlp):
    n, d = x2d.shape
    tm = min(ROW_BLOCK, n)
    assert n % tm == 0
    row = lambda w: pl.BlockSpec((tm, w), lambda i: (i, 0))
    return pl.pallas_call(
        functools.partial(_post_kernel, ff_chunk=min(FF_CHUNK, wup.shape[1])),
        grid=(n // tm,),
        in_specs=[row(d), row(an.shape[1]), row(cn.shape[1]), _const_spec(wout.shape),
                  _const_spec((1, d)), _const_spec((1, d)), _const_spec(wup.shape),
                  _const_spec(wdown.shape), _const_spec((1, d))],
        out_specs=row(d),
        out_shape=jax.ShapeDtypeStruct((n, d), F32),
        compiler_params=_params("arbitrary"),
        name="post",
    )(x2d, an, cn, wout, npost, npre, wup, wdown, nmlp)


def kernel(x_prompt, x_sample, cache_k, cache_v, state_conv, n_att_pre, n_att_post, w_in, conv_w,
           g_att, g_conv, w_out, n_mlp_pre, n_mlp_post, w_up, w_down):
    depth = w_in.shape[0]
    batch, seq, d = x_prompt.shape
    dec_batch, dec_seq, _ = x_sample.shape
    past_len = PAST_LEN
    keep = min(MAX_SPAN, seq)
    assert keep == seq, "the prompt's window buffer is its whole key/value sequence"
    cw = conv_w.shape[2]

    yp = x_prompt.reshape(batch * seq, d)
    ys = x_sample.swapaxes(0, 1).reshape(dec_seq * dec_batch, d)
    outs = [[] for _ in range(6)]
    for l in range(depth):
        win, wout = w_in[l].astype(BF16), w_out[l].astype(BF16)
        wup, wdown = w_up[l].astype(BF16), w_down[l].astype(BF16)
        gpre, npost = n_att_pre[l][None, :], n_att_post[l][None, :]
        npre, nmlp = n_mlp_pre[l][None, :], n_mlp_post[l][None, :]
        gatt, gconv = g_att[l][None, :], g_conv[l][None, :]

        q, k, v, kt, vt, cn, cs = _proj_prompt(yp, seq, gpre, win, conv_w[l], gconv)
        an = _prompt_attn(q, k, v, gatt)
        yp = _post(yp, an, cn, wout, npost, npre, wup, wdown, nmlp)
        per_head = lambda a: a.reshape(batch, ATT_HEADS, HEAD_DIM, seq).transpose(0, 3, 1, 2)
        outs[0].append(per_head(kt))
        outs[1].append(per_head(vt))
        outs[2].append(cs)

        past = state_conv[l].swapaxes(0, 1).reshape((CONV_K - 1) * dec_batch, cw)
        qs, ks, vs, cns, css = _proj_sample(ys, past_len + jnp.arange(dec_seq), dec_batch, past,
                                            gpre, win, conv_w[l], gconv)
        tm4 = lambda a: a.reshape(dec_seq, dec_batch, 1, ATT_WIDTH)
        ans = _sample_attn(tm4(qs), tm4(ks), tm4(vs), cache_k[l], cache_v[l], past_len, gatt)
        ys = _post(ys, ans.reshape(dec_seq * dec_batch, ATT_WIDTH), cns, wout, npost, npre,
                   wup, wdown, nmlp)
        bm = lambda a: a.reshape(dec_seq, dec_batch, ATT_HEADS, HEAD_DIM).swapaxes(0, 1)
        outs[3].append(bm(ks))
        outs[4].append(bm(vs))
        outs[5].append(css.reshape(CONV_K - 1, dec_batch, cw).swapaxes(0, 1))

    y_prompt = yp.reshape(batch, seq, d)
    y_sample = ys.reshape(dec_seq, dec_batch, d).swapaxes(0, 1)
    return (y_prompt, y_sample) + tuple(jnp.stack(o) for o in outs)
```

```python
import functools

import numpy as np
import jax
import jax.numpy as jnp
from jax import lax
from jax.experimental import pallas as pl
from jax.experimental.pallas import tpu as pltpu

HEAD_DIM = 64
ATT_HEADS = 8
ATT_WIDTH = ATT_HEADS * HEAD_DIM
CONV_K = 3
DILATED_PATTERNS = ((128, 1), (512, 4), (2048, 16))
MAX_SPAN = max(w for w, _ in DILATED_PATTERNS)
PAST_LEN = 8192
ROPE_THETA = 10000.0
NORM_EPS = 1e-6

LANES = 128
HEADS_PER_SLAB = LANES // HEAD_DIM
N_SLABS = ATT_WIDTH // LANES
LSE_LANES_PER_HEAD = LANES // ATT_HEADS
MASK_VALUE = -1e30
LOG2_E = 1.4426950408889634
VMEM_LIMIT_BYTES = 56 * 1024 * 1024

ROW_BLOCK = 512
PROJ_ROW_BLOCK = 256
Q_BLOCK = 128
UNITS_PER_STEP = 8
FF_CHUNK = 1024

BF16 = jnp.bfloat16
F32 = jnp.float32
NT_DIMS = (((1,), (1,)), ((), ()))


def _rmsnorm(x, g):
    return x * lax.rsqrt(jnp.mean(x * x, axis=-1, keepdims=True) + NORM_EPS) * g


def _rope(x, cos, sin_lo, sin_hi):
    half = HEAD_DIM // 2
    outs = []
    for c in range(N_SLABS):
        xs = x[:, c * LANES:(c + 1) * LANES]
        outs.append(xs * cos + pltpu.roll(xs, LANES - half, 1) * sin_lo
                    + pltpu.roll(xs, half, 1) * sin_hi)
    return jnp.concatenate(outs, axis=1)


def _project(x, gpre, win_ref, cos, sin_lo, sin_hi, q_scale):
    h = _rmsnorm(x, gpre).astype(BF16)

    def cols(c):
        return jnp.dot(h, win_ref[:, c * ATT_WIDTH:(c + 1) * ATT_WIDTH],
                       preferred_element_type=F32)

    q = _rope(cols(0), cos, sin_lo, sin_hi) * q_scale
    k = _rope(cols(1), cos, sin_lo, sin_hi)
    v = cols(2)
    gate_b = cols(3)
    gated = cols(4) * cols(5)
    return q, k, v, gate_b, gated


def _proj_prompt_kernel(x_ref, gpre_ref, win_ref, cos_ref, slo_ref, shi_ref, convw_ref,
                        gconv_ref, sq_ref, skn_ref, svn_ref, skt_ref, svt_ref, sbias_ref, gatt_ref,
                        q_ref, k_ref, v_ref, kt_ref, vt_ref, cn_ref, cs_ref, san_ref,
                        hist_ref, *, tm, blocks_per_seq, n_t, new_pad):
    _sample_attn_kernel(sq_ref, skn_ref, svn_ref, skt_ref, svt_ref, sbias_ref, gatt_ref, san_ref,
                        n_t=n_t, new_pad=new_pad)
    j = pl.program_id(0) % blocks_per_seq
    row0 = pl.multiple_of(j * tm, tm)
    q, k, v, gate_b, gated = _project(
        x_ref[...], gpre_ref[...], win_ref, cos_ref[pl.ds(row0, tm), :],
        slo_ref[pl.ds(row0, tm), :], shi_ref[pl.ds(row0, tm), :], HEAD_DIM ** -0.5 * LOG2_E)
    for c in range(N_SLABS):
        q_ref[c] = q[:, c * LANES:(c + 1) * LANES]
        k_ref[c] = k[:, c * LANES:(c + 1) * LANES]
        v_ref[c] = v[:, c * LANES:(c + 1) * LANES]
    kt_ref[...] = k.T
    vt_ref[...] = v.T

    @pl.when(j == 0)
    def _():
        hist_ref[0:8, :] = jnp.zeros((8, gated.shape[1]), F32)

    @pl.when(j != 0)
    def _():
        hist_ref[0:8, :] = hist_ref[tm:tm + 8, :]

    hist_ref[8:8 + tm, :] = gated
    conv = (convw_ref[0:1, :] * hist_ref[6:6 + tm, :] + convw_ref[1:2, :] * hist_ref[7:7 + tm, :]
            + convw_ref[2:3, :] * gated)
    cn_ref[...] = _rmsnorm(gate_b * conv, gconv_ref[...]).astype(BF16)
    cs_ref[...] = gated[tm - (CONV_K - 1):tm, :]


def _proj_sample_kernel(x_ref, gpre_ref, win_ref, cos_ref, slo_ref, shi_ref, convw_ref,
                        gconv_ref, past_ref, q_ref, k_ref, v_ref, cn_ref, cs_ref, hist_ref,
                        *, rows, stride):
    q, k, v, gate_b, gated = _project(x_ref[...], gpre_ref[...], win_ref, cos_ref[...],
                                      slo_ref[...], shi_ref[...], HEAD_DIM ** -0.5)
    q_ref[...] = q
    k_ref[...] = k
    v_ref[...] = v
    npast = (CONV_K - 1) * stride
    hist_ref[0:npast, :] = past_ref[...]
    hist_ref[npast:npast + rows, :] = gated
    conv = (convw_ref[0:1, :] * hist_ref[0:rows, :]
            + convw_ref[1:2, :] * hist_ref[stride:stride + rows, :]
            + convw_ref[2:3, :] * gated)
    cn_ref[...] = _rmsnorm(gate_b * conv, gconv_ref[...]).astype(BF16)
    cs_ref[...] = hist_ref[rows:rows + npast, :]


def _stat_lane_group(c, hh):
    return (HEADS_PER_SLAB - 1 - hh) * (HEAD_DIM // LSE_LANES_PER_HEAD) + c


def _attend(load, store_o, store_stats, i, start, kv, reach, tq):
    lane = lax.broadcasted_iota(jnp.int32, (tq, LANES), 1)
    first_head = lane < HEAD_DIM
    group = lane // LSE_LANES_PER_HEAD
    qpos = i * tq + lax.broadcasted_iota(jnp.int32, (tq, kv), 0)
    kpos = start + lax.broadcasted_iota(jnp.int32, (tq, kv), 1)
    dist = qpos - kpos
    bias = jnp.where((dist >= 0) & (dist <= reach), 0.0, MASK_VALUE).astype(F32)
    max_tile = jnp.zeros((tq, LANES), F32)
    den_tile = jnp.zeros((tq, LANES), F32)
    key_lane = lax.broadcasted_iota(jnp.int32, (kv, LANES), 1)
    key_first = key_lane < HEAD_DIM
    for c in range(N_SLABS):
        q2 = load(0, c, i * tq, tq).astype(BF16)
        k2 = load(1, c, start, kv).astype(BF16)
        v2 = load(2, c, start, kv).astype(BF16)
        if kv == tq:
            zero = jnp.zeros_like(k2)
            k_both = jnp.concatenate([jnp.where(key_first, k2, zero), jnp.where(key_first, zero, k2)], 0)
            v_both = jnp.concatenate([jnp.where(key_first, v2, zero), jnp.where(key_first, zero, v2)], 0)
            s_both = lax.dot_general(q2, k_both, NT_DIMS, preferred_element_type=F32)
            probs = []
            for hh in range(HEADS_PER_SLAB):
                s = s_both[:, hh * kv:(hh + 1) * kv] + bias
                m = jnp.max(s, axis=1, keepdims=True)
                p = jnp.exp2(s - m).astype(BF16)
                den = jnp.sum(p.astype(F32), axis=1, keepdims=True)
                probs.append(p)
                mine = group == _stat_lane_group(c, hh)
                max_tile = jnp.where(mine, m, max_tile)
                den_tile = jnp.where(mine, den, den_tile)
            store_o(c, jnp.dot(jnp.concatenate(probs, axis=1), v_both, preferred_element_type=F32))
        else:
            one = jnp.ones_like(v2)
            outs = []
            for hh in range(HEADS_PER_SLAB):
                own = first_head if hh == 0 else ~first_head
                qm = jnp.where(own, q2, jnp.zeros_like(q2))
                s = lax.dot_general(qm, k2, NT_DIMS, preferred_element_type=F32) + bias
                m = jnp.max(s, axis=1, keepdims=True)
                p = jnp.exp2(s - m).astype(BF16)
                v_aug = jnp.where(key_first if hh == 0 else ~key_first, v2, one)
                r = jnp.dot(p, v_aug, preferred_element_type=F32)
                outs.append(r)
                mine = group == _stat_lane_group(c, hh)
                max_tile = jnp.where(mine, m, max_tile)
                den_tile = jnp.where(mine, r, den_tile)
            store_o(c, jnp.where(first_head, outs[0], outs[1]))
    store_stats(max_tile, den_tile)


def _prompt_attn_kernel(q_ref, k_ref, v_ref, expand_ref, gatt_ref, an_ref, o_scr, l_scr,
                        cls_scr, oc_scr, lc_scr, *, seq, tq):
    srcs = (q_ref, k_ref, v_ref)
    (near_w, near_d), (mid_w, mid_d), (far_w, far_d) = DILATED_PATTERNS
    inner = far_d // mid_d
    sub_mid, sub_far = seq // mid_d, seq // far_d
    nq_far = sub_far // tq

    def aligned(first, count):
        if isinstance(first, int):
            return pl.ds(first, count)
        return pl.ds(pl.multiple_of(first, tq), count)

    def near_unit(i, carry):
        start = jnp.maximum(i - 1, 0) * tq
        rows = aligned(i * tq, tq)

        def store_o(c, val):
            o_scr[0, c, rows, :] = val

        def store_stats(top, den):
            l_scr[0, 0, rows, :] = top
            l_scr[0, 1, rows, :] = den

        _attend(lambda a, c, first, count: srcs[a][c, aligned(first, count), :],
                store_o, store_stats, i, start, min(2 * tq, seq), near_w // near_d, tq)
        return carry

    lax.fori_loop(0, seq // tq, near_unit, 0, unroll=UNITS_PER_STEP)

    def mid_class(r, carry):
        for a in range(3):
            for c in range(N_SLABS):
                cls_scr[a, c] = srcs[a][c, pl.ds(r, sub_mid, stride=mid_d), :]

        def mid_unit(i, carry):
            start = jnp.maximum(i - 1, 0) * tq if sub_mid > tq else 0
            rows = pl.ds(r + mid_d * (i * tq), tq, stride=mid_d)

            def store_o(c, val):
                o_scr[1, c, rows, :] = val

            def store_stats(top, den):
                l_scr[1, 0, rows, :] = top
                l_scr[1, 1, rows, :] = den

            _attend(lambda a, c, first, count: cls_scr[a, c, aligned(first, count), :],
                    store_o, store_stats, i, start, min(2 * tq, sub_mid), mid_w // mid_d, tq)
            return carry

        def far_unit(n, carry):
            m = n >> (nq_far.bit_length() - 1)
            i = n & (nq_far - 1)
            start = jnp.maximum(i - 1, 0) * tq if sub_far > tq else 0
            rows = pl.ds(m + inner * (i * tq), tq, stride=inner)

            def store_o(c, val):
                oc_scr[c, rows, :] = val

            def store_stats(top, den):
                lc_scr[0, rows, :] = top
                lc_scr[1, rows, :] = den

            _attend(lambda a, c, first, count:
                    cls_scr[a, c, pl.ds(m + inner * first, count, stride=inner), :],
                    store_o, store_stats, i, start, min(2 * tq, sub_far), far_w // far_d, tq)
            return carry

        lax.fori_loop(0, sub_mid // tq, lambda n, carry: far_unit(n, mid_unit(n, carry)), 0,
                      unroll=UNITS_PER_STEP)
        back = pl.ds(r, sub_mid, stride=mid_d)
        for c in range(N_SLABS):
            o_scr[2, c, back, :] = oc_scr[c]
        for j in range(2):
            l_scr[2, j, back, :] = lc_scr[j]
        return carry

    lax.fori_loop(0, mid_d, mid_class, 0)

    n_pat = len(DILATED_PATTERNS)

    def merge(b, carry):
        r = pl.ds(pl.multiple_of(b * tq, tq), tq)
        tops = [l_scr[ip, 0, r, :] for ip in range(n_pat)]
        top = functools.reduce(jnp.maximum, tops)
        es = [jnp.exp2(t - top) for t in tops]
        inv = 1.0 / functools.reduce(lambda x, y: x + y,
                                     [e * l_scr[ip, 1, r, :] for ip, e in enumerate(es)])
        slabs = [None] * N_SLABS
        for ip, e in enumerate(es):
            wide = jnp.dot((e * inv).astype(BF16), expand_ref[...], preferred_element_type=F32)
            for c in range(N_SLABS):
                term = wide[:, c * LANES:(c + 1) * LANES] * o_scr[ip, c, r, :]
                slabs[c] = term if slabs[c] is None else slabs[c] + term
        attn = jnp.concatenate(slabs, axis=1)
        an_ref[r, :] = _rmsnorm(attn, gatt_ref[...]).astype(BF16)
        return carry

    lax.fori_loop(0, seq // tq, merge, 0, unroll=UNITS_PER_STEP)


def _sample_attn_kernel(q_ref, kn_ref, vn_ref, kt_ref, vt_ref, bias_ref, gatt_ref, o_ref,
                        *, n_t, new_pad):
    width = ATT_WIDTH
    n_pat = len(DILATED_PATTERNS)
    per = n_t * ATT_HEADS
    sub = lax.broadcasted_iota(jnp.int32, (ATT_HEADS, width), 0)
    lane = lax.broadcasted_iota(jnp.int32, (ATT_HEADS, width), 1)
    own_head = sub == lane // HEAD_DIM
    span = kt_ref.shape[-1]
    pad = jnp.zeros((new_pad - n_t, width), F32)
    for j in range(kt_ref.shape[0]):
        qf = q_ref[:, j, 0, :]
        q_tiles = [jnp.where(own_head, jnp.broadcast_to(qf[t:t + 1, :], (ATT_HEADS, width)), 0.0)
                   for t in range(n_t)]
        qbd = jnp.concatenate(q_tiles * n_pat, axis=0).astype(BF16)
        kt = kt_ref[j].reshape(width, span).astype(BF16)
        vt = vt_ref[j].reshape(width, span).astype(BF16)
        kn = jnp.concatenate([kn_ref[:, j, 0, :], pad], axis=0).astype(BF16)
        vn = jnp.concatenate([vn_ref[:, j, 0, :], pad], axis=0).astype(BF16)
        s = jnp.concatenate(
            [jnp.dot(qbd, kt, preferred_element_type=F32),
             lax.dot_general(qbd, kn, NT_DIMS, preferred_element_type=F32)], axis=1) + bias_ref[...]
        m = jnp.max(s, axis=1, keepdims=True)
        p = jnp.exp(s - m)
        den = jnp.sum(p, axis=1, keepdims=True)
        lse = m + jnp.log(den)
        lses = [lse[i * per:(i + 1) * per, :] for i in range(n_pat)]
        top = functools.reduce(jnp.maximum, lses)
        es = [jnp.exp(l - top) for l in lses]
        z = functools.reduce(lambda a, b: a + b, es)
        scale = jnp.concatenate([e / z for e in es], axis=0) / den
        pw = (p * scale).astype(BF16)
        o = (lax.dot_general(pw[:, :span], vt, NT_DIMS, preferred_element_type=F32)
             + jnp.dot(pw[:, span:], vn, preferred_element_type=F32))
        for t in range(n_t):
            acc = o[t * ATT_HEADS:(t + 1) * ATT_HEADS, :]
            for i in range(1, n_pat):
                acc = acc + o[i * per + t * ATT_HEADS:i * per + (t + 1) * ATT_HEADS, :]
            attn = jnp.sum(jnp.where(own_head, acc, 0.0), axis=0, keepdims=True)
            o_ref[t, j, :, :] = _rmsnorm(attn, gatt_ref[...])


def _post_kernel(x_ref, an_ref, cn_ref, wout_ref, npost_ref, npre_ref, wup_ref, wdown_ref,
                 nmlp_ref, y_ref, *, ff_chunk):
    an = an_ref[...].astype(BF16)
    aw = an.shape[1]
    mixed = (jnp.dot(an, wout_ref[0:aw, :], preferred_element_type=F32)
             + jnp.dot(cn_ref[...], wout_ref[aw:, :], preferred_element_type=F32))
    x1 = x_ref[...] + _rmsnorm(mixed, npost_ref[...])
    h = _rmsnorm(x1, npre_ref[...]).astype(BF16)
    d_ff = wup_ref.shape[1]
    f = None
    for c in range(d_ff // ff_chunk):
        u = jnp.dot(h, wup_ref[:, c * ff_chunk:(c + 1) * ff_chunk], preferred_element_type=F32)
        a = jnp.square(jnp.maximum(u, 0.0)).astype(BF16)
        part = jnp.dot(a, wdown_ref[c * ff_chunk:(c + 1) * ff_chunk, :], preferred_element_type=F32)
        f = part if f is None else f + part
    y_ref[...] = x1 + _rmsnorm(f, nmlp_ref[...])


def _const_spec(shape):
    nd = len(shape)
    return pl.BlockSpec(shape, lambda *_: (0,) * nd, pipeline_mode=pl.Buffered(1))


def _rope_tables(positions):
    half = HEAD_DIM // 2
    inv = ROPE_THETA ** (-jnp.arange(half, dtype=F32) * 2.0 / HEAD_DIM)
    ang = positions.astype(F32)[:, None] * inv[None, :]
    cos, sin, zero = jnp.cos(ang), jnp.sin(ang), jnp.zeros_like(ang)
    reps = LANES // HEAD_DIM
    cos_t = jnp.tile(jnp.concatenate([cos, cos], axis=1), (1, reps))
    sin_lo = jnp.tile(jnp.concatenate([-sin, zero], axis=1), (1, reps))
    sin_hi = jnp.tile(jnp.concatenate([zero, sin], axis=1), (1, reps))
    return cos_t, sin_lo, sin_hi


def _params(*semantics):
    return pltpu.CompilerParams(dimension_semantics=semantics, vmem_limit_bytes=VMEM_LIMIT_BYTES)


def _proj_prompt(x2d, seq, gpre, win, convw, gconv, sq, skn, svn, cache_k, cache_v, past_len, gatt):
    n, d = x2d.shape
    tm = min(PROJ_ROW_BLOCK, seq)
    assert seq % tm == 0 and tm % 8 == 0
    bps = seq // tm
    batch = n // seq
    steps = n // tm
    cw = convw.shape[1]
    tables = _rope_tables(jnp.arange(seq))
    row = lambda w: pl.BlockSpec((tm, w), lambda i: (i, 0))
    slabbed = pl.BlockSpec((None, N_SLABS, tm, LANES), lambda i: (i // bps, 0, i % bps, 0))
    transposed = pl.BlockSpec((None, ATT_WIDTH, tm), lambda i: (i // bps, 0, i % bps))
    slab_shape = jax.ShapeDtypeStruct((batch, N_SLABS, seq, LANES), F32)
    t_shape = jax.ShapeDtypeStruct((batch, ATT_WIDTH, seq), F32)

    n_t, dec_batch = sq.shape[0], sq.shape[1]
    span = cache_k.shape[1]
    assert span == min(MAX_SPAN, past_len)
    assert dec_batch % steps == 0, "every grid step takes the same number of sample sequences"
    bb = dec_batch // steps
    new_pad = -(-n_t // LANES) * LANES
    bias = _sample_bias(n_t, past_len, span, new_pad)
    skt = jnp.transpose(cache_k, (0, 2, 3, 1))
    svt = jnp.transpose(cache_v, (0, 2, 3, 1))
    new_blk = pl.BlockSpec((n_t, bb, 1, ATT_WIDTH), lambda i: (0, i, 0, 0))
    buf_blk = pl.BlockSpec((bb, ATT_HEADS, HEAD_DIM, span), lambda i: (i, 0, 0, 0))

    out_shape = (slab_shape, slab_shape, slab_shape, t_shape, t_shape,
                 jax.ShapeDtypeStruct((n, cw), BF16),
                 jax.ShapeDtypeStruct((batch, CONV_K - 1, cw), F32),
                 jax.ShapeDtypeStruct((n_t, dec_batch, 1, ATT_WIDTH), F32))
    return pl.pallas_call(
        functools.partial(_proj_prompt_kernel, tm=tm, blocks_per_seq=bps, n_t=n_t, new_pad=new_pad),
        grid=(steps,),
        in_specs=[row(d), _const_spec((1, d)), _const_spec(win.shape)]
        + [_const_spec((seq, LANES))] * 3 + [_const_spec(convw.shape), _const_spec((1, cw))]
        + [new_blk, new_blk, new_blk, buf_blk, buf_blk, _const_spec(bias.shape),
           _const_spec(gatt.shape)],
        out_specs=[slabbed, slabbed, slabbed, transposed, transposed, row(cw),
                   pl.BlockSpec((None, CONV_K - 1, cw), lambda i: (i // bps, 0, 0)), new_blk],
        out_shape=out_shape,
        scratch_shapes=[pltpu.VMEM((tm + 8, cw), F32)],
        compiler_params=_params("arbitrary"),
        name="proj_prompt",
    )(x2d, gpre, win, *tables, convw, gconv, sq, skn, svn, skt, svt, bias, gatt)


def _proj_sample(x2d, positions, stride, past2d, gpre, win, convw, gconv):
    rows, d = x2d.shape
    cw = convw.shape[1]
    tables = [jnp.repeat(t, stride, axis=0) for t in _rope_tables(positions)]
    npast = (CONV_K - 1) * stride
    full = lambda shape: pl.BlockSpec(shape, lambda i: (0,) * len(shape))
    out_shape = (
        jax.ShapeDtypeStruct((rows, ATT_WIDTH), F32), jax.ShapeDtypeStruct((rows, ATT_WIDTH), F32),
        jax.ShapeDtypeStruct((rows, ATT_WIDTH), F32), jax.ShapeDtypeStruct((rows, cw), BF16),
        jax.ShapeDtypeStruct((npast, cw), F32))
    return pl.pallas_call(
        functools.partial(_proj_sample_kernel, rows=rows, stride=stride),
        grid=(1,),
        in_specs=[full((rows, d)), full((1, d)), full(win.shape)] + [full((rows, LANES))] * 3
        + [full(convw.shape), full((1, cw)), full((npast, cw))],
        out_specs=[full((rows, ATT_WIDTH))] * 3 + [full((rows, cw)), full((npast, cw))],
        out_shape=out_shape,
        scratch_shapes=[pltpu.VMEM((rows + npast, cw), F32)],
        compiler_params=_params("arbitrary"),
        name="proj_sample",
    )(x2d, gpre, win, *tables, convw, gconv, past2d)


def _prompt_attn(q, k, v, gatt):
    batch, _, seq, _ = q.shape
    tq = Q_BLOCK
    (_, near_d), (_, mid_d), (_, far_d) = DILATED_PATTERNS
    assert near_d == 1 and far_d % mid_d == 0
    for window, dil in DILATED_PATTERNS:
        sub = seq // dil
        assert seq % dil == 0 and sub % tq == 0 and window % dil == 0 and window // dil <= tq
        assert (sub // tq) & (sub // tq - 1) == 0
    sub_mid = seq // mid_d
    col_head = np.arange(ATT_WIDTH) // HEAD_DIM
    col_group = np.array([_stat_lane_group(h // HEADS_PER_SLAB, h % HEADS_PER_SLAB) for h in col_head])
    expand = jnp.asarray(np.arange(LANES)[:, None] == col_group[None, :] * LSE_LANES_PER_HEAD, BF16)
    n_pat = len(DILATED_PATTERNS)
    blk = pl.BlockSpec((None, N_SLABS, seq, LANES), lambda b: (b, 0, 0, 0))
    return pl.pallas_call(
        functools.partial(_prompt_attn_kernel, seq=seq, tq=tq),
        grid=(batch,),
        in_specs=[blk, blk, blk, _const_spec(expand.shape), _const_spec(gatt.shape)],
        out_specs=pl.BlockSpec((seq, ATT_WIDTH), lambda b: (b, 0)),
        out_shape=jax.ShapeDtypeStruct((batch * seq, ATT_WIDTH), BF16),
        scratch_shapes=[pltpu.VMEM((n_pat, N_SLABS, seq, LANES), F32),
                        pltpu.VMEM((n_pat, 2, seq, LANES), F32),
                        pltpu.VMEM((3, N_SLABS, sub_mid, LANES), F32),
                        pltpu.VMEM((N_SLABS, sub_mid, LANES), F32),
                        pltpu.VMEM((2, sub_mid, LANES), F32)],
        compiler_params=_params("arbitrary"),
        name="prompt_attn",
    )(q, k, v, expand, gatt)


def _sample_bias(n_t, past_len, span, new_pad):
    big = 1 << 30
    rel = np.concatenate([np.arange(span) - span, np.arange(n_t),
                          np.full((new_pad - n_t,), -big)])
    rows = []
    for window, dil in DILATED_PATTERNS:
        for t in range(n_t):
            dist = t - rel
            ok = (dist >= 0) & (dist <= window) & (dist % dil == 0) & (past_len + rel >= 0)
            rows += [np.where(ok, 0.0, MASK_VALUE)] * ATT_HEADS
    return jnp.asarray(np.stack(rows), F32)


def _post(x2d, an, cn, wout, npost, npre, wup, wdown, nmlp):
    n, d = x2d.shape
    tm = min(ROW_BLOCK, n)
    assert n % tm == 0
    row = lambda w: pl.BlockSpec((tm, w), lambda i: (i, 0))
    return pl.pallas_call(
        functools.partial(_post_kernel, ff_chunk=min(FF_CHUNK, wup.shape[1])),
        grid=(n // tm,),
        in_specs=[row(d), row(an.shape[1]), row(cn.shape[1]), _const_spec(wout.shape),
                  _const_spec((1, d)), _const_spec((1, d)), _const_spec(wup.shape),
                  _const_spec(wdown.shape), _const_spec((1, d))],
        out_specs=row(d),
        out_shape=jax.ShapeDtypeStruct((n, d), F32),
        compiler_params=_params("arbitrary"),
        name="post",
    )(x2d, an, cn, wout, npost, npre, wup, wdown, nmlp)


def kernel(x_prompt, x_sample, cache_k, cache_v, state_conv, n_att_pre, n_att_post, w_in, conv_w,
           g_att, g_conv, w_out, n_mlp_pre, n_mlp_post, w_up, w_down):
    depth = w_in.shape[0]
    batch, seq, d = x_prompt.shape
    dec_batch, dec_seq, _ = x_sample.shape
    past_len = PAST_LEN
    keep = min(MAX_SPAN, seq)
    assert keep == seq, "the prompt's window buffer is its whole key/value sequence"
    cw = conv_w.shape[2]

    yp = x_prompt.reshape(batch * seq, d)
    ys = x_sample.swapaxes(0, 1).reshape(dec_seq * dec_batch, d)
    outs = [[] for _ in range(6)]
    for l in range(depth):
        win, wout = w_in[l].astype(BF16), w_out[l].astype(BF16)
        wup, wdown = w_up[l].astype(BF16), w_down[l].astype(BF16)
        gpre, npost = n_att_pre[l][None, :], n_att_post[l][None, :]
        npre, nmlp = n_mlp_pre[l][None, :], n_mlp_post[l][None, :]
        gatt, gconv = g_att[l][None, :], g_conv[l][None, :]

        past = state_conv[l].swapaxes(0, 1).reshape((CONV_K - 1) * dec_batch, cw)
        qs, ks, vs, cns, css = _proj_sample(ys, past_len + jnp.arange(dec_seq), dec_batch, past,
                                            gpre, win, conv_w[l], gconv)
        tm4 = lambda a: a.reshape(dec_seq, dec_batch, 1, ATT_WIDTH)
        q, k, v, kt, vt, cn, cs, ans = _proj_prompt(
            yp, seq, gpre, win, conv_w[l], gconv, tm4(qs), tm4(ks), tm4(vs),
            cache_k[l], cache_v[l], past_len, gatt)
        an = _prompt_attn(q, k, v, gatt)
        yp = _post(yp, an, cn, wout, npost, npre, wup, wdown, nmlp)
        per_head = lambda a: a.reshape(batch, ATT_HEADS, HEAD_DIM, seq).transpose(0, 3, 1, 2)
        outs[0].append(per_head(kt))
        outs[1].append(per_head(vt))
        outs[2].append(cs)
        ys = _post(ys, ans.reshape(dec_seq * dec_batch, ATT_WIDTH), cns, wout, npost, npre,
                   wup, wdown, nmlp)
        bm = lambda a: a.reshape(dec_seq, dec_batch, ATT_HEADS, HEAD_DIM).swapaxes(0, 1)
        outs[3].append(bm(ks))
        outs[4].append(bm(vs))
        outs[5].append(css.reshape(CONV_K - 1, dec_batch, cw).swapaxes(0, 1))

    y_prompt = yp.reshape(batch, seq, d)
    y_sample = ys.reshape(dec_seq, dec_batch, d).swapaxes(0, 1)
    return (y_prompt, y_sample) + tuple(jnp.stack(o) for o in outs)
```

```python
import functools

import numpy as np
import jax
import jax.numpy as jnp
from jax import lax
from jax.experimental import pallas as pl
from jax.experimental.pallas import tpu as pltpu

HEAD_DIM = 64
ATT_HEADS = 8
ATT_WIDTH = ATT_HEADS * HEAD_DIM
CONV_K = 3
DILATED_PATTERNS = ((128, 1), (512, 4), (2048, 16))
MAX_SPAN = max(w for w, _ in DILATED_PATTERNS)
PAST_LEN = 8192
ROPE_THETA = 10000.0
NORM_EPS = 1e-6

LANES = 128
HEADS_PER_SLAB = LANES // HEAD_DIM
N_SLABS = ATT_WIDTH // LANES
LSE_LANES_PER_HEAD = LANES // ATT_HEADS
MASK_VALUE = -1e30
LOG2_E = 1.4426950408889634
VMEM_LIMIT_BYTES = 56 * 1024 * 1024

ROW_BLOCK = 512
POST_ROW_BLOCK = 256
Q_BLOCK = 128
UNITS_PER_STEP = 8
FF_CHUNK = 1024

BF16 = jnp.bfloat16
F32 = jnp.float32
NT_DIMS = (((1,), (1,)), ((), ()))


def _rmsnorm(x, g):
    return x * lax.rsqrt(jnp.mean(x * x, axis=-1, keepdims=True) + NORM_EPS) * g


def _rope(x, cos, sin_lo, sin_hi):
    half = HEAD_DIM // 2
    outs = []
    for c in range(N_SLABS):
        xs = x[:, c * LANES:(c + 1) * LANES]
        outs.append(xs * cos + pltpu.roll(xs, LANES - half, 1) * sin_lo
                    + pltpu.roll(xs, half, 1) * sin_hi)
    return jnp.concatenate(outs, axis=1)


def _project(x, gpre, win_ref, cos, sin_lo, sin_hi, q_scale):
    h = _rmsnorm(x, gpre).astype(BF16)

    def cols(c):
        return jnp.dot(h, win_ref[:, c * ATT_WIDTH:(c + 1) * ATT_WIDTH],
                       preferred_element_type=F32)

    q = _rope(cols(0), cos, sin_lo, sin_hi) * q_scale
    k = _rope(cols(1), cos, sin_lo, sin_hi)
    v = cols(2)
    gate_b = cols(3)
    gated = cols(4) * cols(5)
    return q, k, v, gate_b, gated


def _proj_prompt_kernel(x_ref, gpre_ref, win_ref, cos_ref, slo_ref, shi_ref, convw_ref,
                        gconv_ref, q_ref, k_ref, v_ref, kt_ref, vt_ref, cn_ref, cs_ref,
                        hist_ref, *, tm, blocks_per_seq):
    j = pl.program_id(0) % blocks_per_seq
    row0 = pl.multiple_of(j * tm, tm)
    q, k, v, gate_b, gated = _project(
        x_ref[...], gpre_ref[...], win_ref, cos_ref[pl.ds(row0, tm), :],
        slo_ref[pl.ds(row0, tm), :], shi_ref[pl.ds(row0, tm), :], HEAD_DIM ** -0.5 * LOG2_E)
    for c in range(N_SLABS):
        q_ref[c] = q[:, c * LANES:(c + 1) * LANES]
        k_ref[c] = k[:, c * LANES:(c + 1) * LANES]
        v_ref[c] = v[:, c * LANES:(c + 1) * LANES]
    kt_ref[...] = k.T
    vt_ref[...] = v.T

    @pl.when(j == 0)
    def _():
        hist_ref[0:8, :] = jnp.zeros((8, gated.shape[1]), F32)

    @pl.when(j != 0)
    def _():
        hist_ref[0:8, :] = hist_ref[tm:tm + 8, :]

    hist_ref[8:8 + tm, :] = gated
    conv = (convw_ref[0:1, :] * hist_ref[6:6 + tm, :] + convw_ref[1:2, :] * hist_ref[7:7 + tm, :]
            + convw_ref[2:3, :] * gated)
    cn_ref[...] = _rmsnorm(gate_b * conv, gconv_ref[...]).astype(BF16)
    cs_ref[...] = gated[tm - (CONV_K - 1):tm, :]


def _proj_sample_kernel(x_ref, gpre_ref, win_ref, cos_ref, slo_ref, shi_ref, convw_ref,
                        gconv_ref, past_ref, q_ref, k_ref, v_ref, cn_ref, cs_ref, hist_ref,
                        *, rows, stride):
    q, k, v, gate_b, gated = _project(x_ref[...], gpre_ref[...], win_ref, cos_ref[...],
                                      slo_ref[...], shi_ref[...], HEAD_DIM ** -0.5)
    q_ref[...] = q
    k_ref[...] = k
    v_ref[...] = v
    npast = (CONV_K - 1) * stride
    hist_ref[0:npast, :] = past_ref[...]
    hist_ref[npast:npast + rows, :] = gated
    conv = (convw_ref[0:1, :] * hist_ref[0:rows, :]
            + convw_ref[1:2, :] * hist_ref[stride:stride + rows, :]
            + convw_ref[2:3, :] * gated)
    cn_ref[...] = _rmsnorm(gate_b * conv, gconv_ref[...]).astype(BF16)
    cs_ref[...] = hist_ref[rows:rows + npast, :]


def _stat_lane_group(c, hh):
    return (HEADS_PER_SLAB - 1 - hh) * (HEAD_DIM // LSE_LANES_PER_HEAD) + c


def _attend(load, store_o, store_stats, i, start, kv, reach, tq):
    lane = lax.broadcasted_iota(jnp.int32, (tq, LANES), 1)
    first_head = lane < HEAD_DIM
    group = lane // LSE_LANES_PER_HEAD
    qpos = i * tq + lax.broadcasted_iota(jnp.int32, (tq, kv), 0)
    kpos = start + lax.broadcasted_iota(jnp.int32, (tq, kv), 1)
    dist = qpos - kpos
    bias = jnp.where((dist >= 0) & (dist <= reach), 0.0, MASK_VALUE).astype(F32)
    max_tile = jnp.zeros((tq, LANES), F32)
    den_tile = jnp.zeros((tq, LANES), F32)
    key_lane = lax.broadcasted_iota(jnp.int32, (kv, LANES), 1)
    key_first = key_lane < HEAD_DIM
    for c in range(N_SLABS):
        q2 = load(0, c, i * tq, tq).astype(BF16)
        k2 = load(1, c, start, kv).astype(BF16)
        v2 = load(2, c, start, kv).astype(BF16)
        if kv == tq:
            zero = jnp.zeros_like(k2)
            k_both = jnp.concatenate([jnp.where(key_first, k2, zero), jnp.where(key_first, zero, k2)], 0)
            v_both = jnp.concatenate([jnp.where(key_first, v2, zero), jnp.where(key_first, zero, v2)], 0)
            s_both = lax.dot_general(q2, k_both, NT_DIMS, preferred_element_type=F32)
            probs = []
            for hh in range(HEADS_PER_SLAB):
                s = s_both[:, hh * kv:(hh + 1) * kv] + bias
                m = jnp.max(s, axis=1, keepdims=True)
                p = jnp.exp2(s - m).astype(BF16)
                den = jnp.sum(p.astype(F32), axis=1, keepdims=True)
                probs.append(p)
                mine = group == _stat_lane_group(c, hh)
                max_tile = jnp.where(mine, m, max_tile)
                den_tile = jnp.where(mine, den, den_tile)
            store_o(c, jnp.dot(jnp.concatenate(probs, axis=1), v_both, preferred_element_type=F32))
        else:
            one = jnp.ones_like(v2)
            outs = []
            for hh in range(HEADS_PER_SLAB):
                own = first_head if hh == 0 else ~first_head
                qm = jnp.where(own, q2, jnp.zeros_like(q2))
                s = lax.dot_general(qm, k2, NT_DIMS, preferred_element_type=F32) + bias
                m = jnp.max(s, axis=1, keepdims=True)
                p = jnp.exp2(s - m).astype(BF16)
                v_aug = jnp.where(key_first if hh == 0 else ~key_first, v2, one)
                r = jnp.dot(p, v_aug, preferred_element_type=F32)
                outs.append(r)
                mine = group == _stat_lane_group(c, hh)
                max_tile = jnp.where(mine, m, max_tile)
                den_tile = jnp.where(mine, r, den_tile)
            store_o(c, jnp.where(first_head, outs[0], outs[1]))
    store_stats(max_tile, den_tile)


def _prompt_attn_kernel(q_ref, k_ref, v_ref, expand_ref, gatt_ref, an_ref, o_scr, l_scr,
                        cls_scr, oc_scr, lc_scr, *, seq, tq):
    srcs = (q_ref, k_ref, v_ref)
    (near_w, near_d), (mid_w, mid_d), (far_w, far_d) = DILATED_PATTERNS
    inner = far_d // mid_d
    sub_mid, sub_far = seq // mid_d, seq // far_d
    nq_far = sub_far // tq

    def aligned(first, count):
        if isinstance(first, int):
            return pl.ds(first, count)
        return pl.ds(pl.multiple_of(first, tq), count)

    def near_unit(i, carry):
        start = jnp.maximum(i - 1, 0) * tq
        rows = aligned(i * tq, tq)

        def store_o(c, val):
            o_scr[0, c, rows, :] = val

        def store_stats(top, den):
            l_scr[0, 0, rows, :] = top
            l_scr[0, 1, rows, :] = den

        _attend(lambda a, c, first, count: srcs[a][c, aligned(first, count), :],
                store_o, store_stats, i, start, min(2 * tq, seq), near_w // near_d, tq)
        return carry

    lax.fori_loop(0, seq // tq, near_unit, 0, unroll=UNITS_PER_STEP)

    def mid_class(r, carry):
        for a in range(3):
            for c in range(N_SLABS):
                cls_scr[a, c] = srcs[a][c, pl.ds(r, sub_mid, stride=mid_d), :]

        def mid_unit(i, carry):
            start = jnp.maximum(i - 1, 0) * tq if sub_mid > tq else 0
            rows = pl.ds(r + mid_d * (i * tq), tq, stride=mid_d)

            def store_o(c, val):
                o_scr[1, c, rows, :] = val

            def store_stats(top, den):
                l_scr[1, 0, rows, :] = top
                l_scr[1, 1, rows, :] = den

            _attend(lambda a, c, first, count: cls_scr[a, c, aligned(first, count), :],
                    store_o, store_stats, i, start, min(2 * tq, sub_mid), mid_w // mid_d, tq)
            return carry

        def far_unit(n, carry):
            m = n >> (nq_far.bit_length() - 1)
            i = n & (nq_far - 1)
            start = jnp.maximum(i - 1, 0) * tq if sub_far > tq else 0
            rows = pl.ds(m + inner * (i * tq), tq, stride=inner)

            def store_o(c, val):
                oc_scr[c, rows, :] = val

            def store_stats(top, den):
                lc_scr[0, rows, :] = top
                lc_scr[1, rows, :] = den

            _attend(lambda a, c, first, count:
                    cls_scr[a, c, pl.ds(m + inner * first, count, stride=inner), :],
                    store_o, store_stats, i, start, min(2 * tq, sub_far), far_w // far_d, tq)
            return carry

        lax.fori_loop(0, sub_mid // tq, lambda n, carry: far_unit(n, mid_unit(n, carry)), 0,
                      unroll=UNITS_PER_STEP)
        back = pl.ds(r, sub_mid, stride=mid_d)
        for c in range(N_SLABS):
            o_scr[2, c, back, :] = oc_scr[c]
        for j in range(2):
            l_scr[2, j, back, :] = lc_scr[j]
        return carry

    lax.fori_loop(0, mid_d, mid_class, 0)

    n_pat = len(DILATED_PATTERNS)

    def merge(b, carry):
        r = pl.ds(pl.multiple_of(b * tq, tq), tq)
        tops = [l_scr[ip, 0, r, :] for ip in range(n_pat)]
        top = functools.reduce(jnp.maximum, tops)
        es = [jnp.exp2(t - top) for t in tops]
        inv = 1.0 / functools.reduce(lambda x, y: x + y,
                                     [e * l_scr[ip, 1, r, :] for ip, e in enumerate(es)])
        slabs = [None] * N_SLABS
        for ip, e in enumerate(es):
            wide = jnp.dot((e * inv).astype(BF16), expand_ref[...], preferred_element_type=F32)
            for c in range(N_SLABS):
                term = wide[:, c * LANES:(c + 1) * LANES] * o_scr[ip, c, r, :]
                slabs[c] = term if slabs[c] is None else slabs[c] + term
        attn = jnp.concatenate(slabs, axis=1)
        an_ref[r, :] = _rmsnorm(attn, gatt_ref[...]).astype(BF16)
        return carry

    lax.fori_loop(0, seq // tq, merge, 0, unroll=UNITS_PER_STEP)


def _sample_attn_kernel(q_ref, kn_ref, vn_ref, kt_ref, vt_ref, bias_ref, gatt_ref, o_ref,
                        *, n_t, new_pad):
    width = ATT_WIDTH
    n_pat = len(DILATED_PATTERNS)
    per = n_t * ATT_HEADS
    sub = lax.broadcasted_iota(jnp.int32, (ATT_HEADS, width), 0)
    lane = lax.broadcasted_iota(jnp.int32, (ATT_HEADS, width), 1)
    own_head = sub == lane // HEAD_DIM
    span = kt_ref.shape[-1]
    pad = jnp.zeros((new_pad - n_t, width), F32)
    for j in range(kt_ref.shape[0]):
        qf = q_ref[:, j, 0, :]
        q_tiles = [jnp.where(own_head, jnp.broadcast_to(qf[t:t + 1, :], (ATT_HEADS, width)), 0.0)
                   for t in range(n_t)]
        qbd = jnp.concatenate(q_tiles * n_pat, axis=0).astype(BF16)
        kt = kt_ref[j].reshape(width, span).astype(BF16)
        vt = vt_ref[j].reshape(width, span).astype(BF16)
        kn = jnp.concatenate([kn_ref[:, j, 0, :], pad], axis=0).astype(BF16)
        vn = jnp.concatenate([vn_ref[:, j, 0, :], pad], axis=0).astype(BF16)
        s = jnp.concatenate(
            [jnp.dot(qbd, kt, preferred_element_type=F32),
             lax.dot_general(qbd, kn, NT_DIMS, preferred_element_type=F32)], axis=1) + bias_ref[...]
        m = jnp.max(s, axis=1, keepdims=True)
        p = jnp.exp(s - m)
        den = jnp.sum(p, axis=1, keepdims=True)
        lse = m + jnp.log(den)
        lses = [lse[i * per:(i + 1) * per, :] for i in range(n_pat)]
        top = functools.reduce(jnp.maximum, lses)
        es = [jnp.exp(l - top) for l in lses]
        z = functools.reduce(lambda a, b: a + b, es)
        scale = jnp.concatenate([e / z for e in es], axis=0) / den
        pw = (p * scale).astype(BF16)
        o = (lax.dot_general(pw[:, :span], vt, NT_DIMS, preferred_element_type=F32)
             + jnp.dot(pw[:, span:], vn, preferred_element_type=F32))
        for t in range(n_t):
            acc = o[t * ATT_HEADS:(t + 1) * ATT_HEADS, :]
            for i in range(1, n_pat):
                acc = acc + o[i * per + t * ATT_HEADS:i * per + (t + 1) * ATT_HEADS, :]
            attn = jnp.sum(jnp.where(own_head, acc, 0.0), axis=0, keepdims=True)
            o_ref[t, j, :, :] = _rmsnorm(attn, gatt_ref[...])


def _post_kernel(*refs, ff_chunk, sample_n_t, sample_new_pad):
    (x_ref, an_ref, cn_ref, wout_ref, npost_ref, npre_ref, wup_ref, wdown_ref, nmlp_ref) = refs[:9]
    if sample_n_t:
        _sample_attn_kernel(*refs[9:16], refs[17], n_t=sample_n_t, new_pad=sample_new_pad)
    y_ref = refs[16] if sample_n_t else refs[9]
    an = an_ref[...].astype(BF16)
    aw = an.shape[1]
    mixed = (jnp.dot(an, wout_ref[0:aw, :], preferred_element_type=F32)
             + jnp.dot(cn_ref[...], wout_ref[aw:, :], preferred_element_type=F32))
    x1 = x_ref[...] + _rmsnorm(mixed, npost_ref[...])
    h = _rmsnorm(x1, npre_ref[...]).astype(BF16)
    d_ff = wup_ref.shape[1]
    f = None
    for c in range(d_ff // ff_chunk):
        u = jnp.dot(h, wup_ref[:, c * ff_chunk:(c + 1) * ff_chunk], preferred_element_type=F32)
        a = jnp.square(jnp.maximum(u, 0.0)).astype(BF16)
        part = jnp.dot(a, wdown_ref[c * ff_chunk:(c + 1) * ff_chunk, :], preferred_element_type=F32)
        f = part if f is None else f + part
    y_ref[...] = x1 + _rmsnorm(f, nmlp_ref[...])


def _const_spec(shape):
    nd = len(shape)
    return pl.BlockSpec(shape, lambda *_: (0,) * nd, pipeline_mode=pl.Buffered(1))


def _rope_tables(positions):
    half = HEAD_DIM // 2
    inv = ROPE_THETA ** (-jnp.arange(half, dtype=F32) * 2.0 / HEAD_DIM)
    ang = positions.astype(F32)[:, None] * inv[None, :]
    cos, sin, zero = jnp.cos(ang), jnp.sin(ang), jnp.zeros_like(ang)
    reps = LANES // HEAD_DIM
    cos_t = jnp.tile(jnp.concatenate([cos, cos], axis=1), (1, reps))
    sin_lo = jnp.tile(jnp.concatenate([-sin, zero], axis=1), (1, reps))
    sin_hi = jnp.tile(jnp.concatenate([zero, sin], axis=1), (1, reps))
    return cos_t, sin_lo, sin_hi


def _params(*semantics):
    return pltpu.CompilerParams(dimension_semantics=semantics, vmem_limit_bytes=VMEM_LIMIT_BYTES)


def _proj_prompt(x2d, seq, gpre, win, convw, gconv):
    n, d = x2d.shape
    tm = min(ROW_BLOCK, seq)
    assert seq % tm == 0 and tm % 8 == 0
    bps = seq // tm
    batch = n // seq
    cw = convw.shape[1]
    tables = _rope_tables(jnp.arange(seq))
    row = lambda w: pl.BlockSpec((tm, w), lambda i: (i, 0))
    slabbed = pl.BlockSpec((None, N_SLABS, tm, LANES), lambda i: (i // bps, 0, i % bps, 0))
    transposed = pl.BlockSpec((None, ATT_WIDTH, tm), lambda i: (i // bps, 0, i % bps))
    slab_shape = jax.ShapeDtypeStruct((batch, N_SLABS, seq, LANES), F32)
    t_shape = jax.ShapeDtypeStruct((batch, ATT_WIDTH, seq), F32)
    out_shape = (slab_shape, slab_shape, slab_shape, t_shape, t_shape,
                 jax.ShapeDtypeStruct((n, cw), BF16),
                 jax.ShapeDtypeStruct((batch, CONV_K - 1, cw), F32))
    return pl.pallas_call(
        functools.partial(_proj_prompt_kernel, tm=tm, blocks_per_seq=bps),
        grid=(n // tm,),
        in_specs=[row(d), _const_spec((1, d)), _const_spec(win.shape)]
        + [_const_spec((seq, LANES))] * 3 + [_const_spec(convw.shape), _const_spec((1, cw))],
        out_specs=[slabbed, slabbed, slabbed, transposed, transposed, row(cw),
                   pl.BlockSpec((None, CONV_K - 1, cw), lambda i: (i // bps, 0, 0))],
        out_shape=out_shape,
        scratch_shapes=[pltpu.VMEM((tm + 8, cw), F32)],
        compiler_params=_params("arbitrary"),
        name="proj_prompt",
    )(x2d, gpre, win, *tables, convw, gconv)


def _proj_sample(x2d, positions, stride, past2d, gpre, win, convw, gconv):
    rows, d = x2d.shape
    cw = convw.shape[1]
    tables = [jnp.repeat(t, stride, axis=0) for t in _rope_tables(positions)]
    npast = (CONV_K - 1) * stride
    full = lambda shape: pl.BlockSpec(shape, lambda i: (0,) * len(shape))
    out_shape = (
        jax.ShapeDtypeStruct((rows, ATT_WIDTH), F32), jax.ShapeDtypeStruct((rows, ATT_WIDTH), F32),
        jax.ShapeDtypeStruct((rows, ATT_WIDTH), F32), jax.ShapeDtypeStruct((rows, cw), BF16),
        jax.ShapeDtypeStruct((npast, cw), F32))
    return pl.pallas_call(
        functools.partial(_proj_sample_kernel, rows=rows, stride=stride),
        grid=(1,),
        in_specs=[full((rows, d)), full((1, d)), full(win.shape)] + [full((rows, LANES))] * 3
        + [full(convw.shape), full((1, cw)), full((npast, cw))],
        out_specs=[full((rows, ATT_WIDTH))] * 3 + [full((rows, cw)), full((npast, cw))],
        out_shape=out_shape,
        scratch_shapes=[pltpu.VMEM((rows + npast, cw), F32)],
        compiler_params=_params("arbitrary"),
        name="proj_sample",
    )(x2d, gpre, win, *tables, convw, gconv, past2d)


def _prompt_attn(q, k, v, gatt):
    batch, _, seq, _ = q.shape
    tq = Q_BLOCK
    (_, near_d), (_, mid_d), (_, far_d) = DILATED_PATTERNS
    assert near_d == 1 and far_d % mid_d == 0
    for window, dil in DILATED_PATTERNS:
        sub = seq // dil
        assert seq % dil == 0 and sub % tq == 0 and window % dil == 0 and window // dil <= tq
        assert (sub // tq) & (sub // tq - 1) == 0
    sub_mid = seq // mid_d
    col_head = np.arange(ATT_WIDTH) // HEAD_DIM
    col_group = np.array([_stat_lane_group(h // HEADS_PER_SLAB, h % HEADS_PER_SLAB) for h in col_head])
    expand = jnp.asarray(np.arange(LANES)[:, None] == col_group[None, :] * LSE_LANES_PER_HEAD, BF16)
    n_pat = len(DILATED_PATTERNS)
    blk = pl.BlockSpec((None, N_SLABS, seq, LANES), lambda b: (b, 0, 0, 0))
    return pl.pallas_call(
        functools.partial(_prompt_attn_kernel, seq=seq, tq=tq),
        grid=(batch,),
        in_specs=[blk, blk, blk, _const_spec(expand.shape), _const_spec(gatt.shape)],
        out_specs=pl.BlockSpec((seq, ATT_WIDTH), lambda b: (b, 0)),
        out_shape=jax.ShapeDtypeStruct((batch * seq, ATT_WIDTH), BF16),
        scratch_shapes=[pltpu.VMEM((n_pat, N_SLABS, seq, LANES), F32),
                        pltpu.VMEM((n_pat, 2, seq, LANES), F32),
                        pltpu.VMEM((3, N_SLABS, sub_mid, LANES), F32),
                        pltpu.VMEM((N_SLABS, sub_mid, LANES), F32),
                        pltpu.VMEM((2, sub_mid, LANES), F32)],
        compiler_params=_params("arbitrary"),
        name="prompt_attn",
    )(q, k, v, expand, gatt)


def _sample_bias(n_t, past_len, span, new_pad):
    big = 1 << 30
    rel = np.concatenate([np.arange(span) - span, np.arange(n_t),
                          np.full((new_pad - n_t,), -big)])
    rows = []
    for window, dil in DILATED_PATTERNS:
        for t in range(n_t):
            dist = t - rel
            ok = (dist >= 0) & (dist <= window) & (dist % dil == 0) & (past_len + rel >= 0)
            rows += [np.where(ok, 0.0, MASK_VALUE)] * ATT_HEADS
    return jnp.asarray(np.stack(rows), F32)


def _post(x2d, an, cn, wout, npost, npre, wup, wdown, nmlp, sample=None):
    n, d = x2d.shape
    tm = min(POST_ROW_BLOCK if sample else ROW_BLOCK, n)
    assert n % tm == 0
    steps = n // tm
    row = lambda w: pl.BlockSpec((tm, w), lambda i: (i, 0))
    args = [x2d, an, cn, wout, npost, npre, wup, wdown, nmlp]
    in_specs = [row(d), row(an.shape[1]), row(cn.shape[1]), _const_spec(wout.shape),
                _const_spec((1, d)), _const_spec((1, d)), _const_spec(wup.shape),
                _const_spec(wdown.shape), _const_spec((1, d))]
    out_specs = [row(d)]
    out_shape = [jax.ShapeDtypeStruct((n, d), F32)]
    n_t = new_pad = 0
    if sample:
        sq, skn, svn, cache_k, cache_v, past_len, gatt = sample
        n_t, dec_batch = sq.shape[0], sq.shape[1]
        span = cache_k.shape[1]
        assert span == min(MAX_SPAN, past_len)
        assert dec_batch % steps == 0, "every grid step takes the same number of sample sequences"
        bb = dec_batch // steps
        new_pad = -(-n_t // LANES) * LANES
        bias = _sample_bias(n_t, past_len, span, new_pad)
        skt = jnp.transpose(cache_k, (0, 2, 3, 1))
        svt = jnp.transpose(cache_v, (0, 2, 3, 1))
        new_blk = pl.BlockSpec((n_t, bb, 1, ATT_WIDTH), lambda i: (0, i, 0, 0))
        buf_blk = pl.BlockSpec((bb, ATT_HEADS, HEAD_DIM, span), lambda i: (i, 0, 0, 0))
        args += [sq, skn, svn, skt, svt, bias, gatt]
        in_specs += [new_blk, new_blk, new_blk, buf_blk, buf_blk, _const_spec(bias.shape),
                     _const_spec(gatt.shape)]
        out_specs.append(new_blk)
        out_shape.append(jax.ShapeDtypeStruct((n_t, dec_batch, 1, ATT_WIDTH), F32))
    outs = pl.pallas_call(
        functools.partial(_post_kernel, ff_chunk=min(FF_CHUNK, wup.shape[1]), sample_n_t=n_t,
                          sample_new_pad=new_pad),
        grid=(steps,),
        in_specs=in_specs,
        out_specs=out_specs,
        out_shape=out_shape,
        compiler_params=_params("arbitrary"),
        name="post",
    )(*args)
    return outs if sample else outs[0]


def kernel(x_prompt, x_sample, cache_k, cache_v, state_conv, n_att_pre, n_att_post, w_in, conv_w,
           g_att, g_conv, w_out, n_mlp_pre, n_mlp_post, w_up, w_down):
    depth = w_in.shape[0]
    batch, seq, d = x_prompt.shape
    dec_batch, dec_seq, _ = x_sample.shape
    past_len = PAST_LEN
    keep = min(MAX_SPAN, seq)
    assert keep == seq, "the prompt's window buffer is its whole key/value sequence"
    cw = conv_w.shape[2]

    yp = x_prompt.reshape(batch * seq, d)
    ys = x_sample.swapaxes(0, 1).reshape(dec_seq * dec_batch, d)
    outs = [[] for _ in range(6)]
    for l in range(depth):
        win, wout = w_in[l].astype(BF16), w_out[l].astype(BF16)
        wup, wdown = w_up[l].astype(BF16), w_down[l].astype(BF16)
        gpre, npost = n_att_pre[l][None, :], n_att_post[l][None, :]
        npre, nmlp = n_mlp_pre[l][None, :], n_mlp_post[l][None, :]
        gatt, gconv = g_att[l][None, :], g_conv[l][None, :]

        past = state_conv[l].swapaxes(0, 1).reshape((CONV_K - 1) * dec_batch, cw)
        qs, ks, vs, cns, css = _proj_sample(ys, past_len + jnp.arange(dec_seq), dec_batch, past,
                                            gpre, win, conv_w[l], gconv)
        tm4 = lambda a: a.reshape(dec_seq, dec_batch, 1, ATT_WIDTH)
        q, k, v, kt, vt, cn, cs = _proj_prompt(yp, seq, gpre, win, conv_w[l], gconv)
        an = _prompt_attn(q, k, v, gatt)
        yp, ans = _post(yp, an, cn, wout, npost, npre, wup, wdown, nmlp,
                        sample=(tm4(qs), tm4(ks), tm4(vs), cache_k[l], cache_v[l], past_len, gatt))
        per_head = lambda a: a.reshape(batch, ATT_HEADS, HEAD_DIM, seq).transpose(0, 3, 1, 2)
        outs[0].append(per_head(kt))
        outs[1].append(per_head(vt))
        outs[2].append(cs)
        ys = _post(ys, ans.reshape(dec_seq * dec_batch, ATT_WIDTH), cns, wout, npost, npre,
                   wup, wdown, nmlp)
        bm = lambda a: a.reshape(dec_seq, dec_batch, ATT_HEADS, HEAD_DIM).swapaxes(0, 1)
        outs[3].append(bm(ks))
        outs[4].append(bm(vs))
        outs[5].append(css.reshape(CONV_K - 1, dec_batch, cw).swapaxes(0, 1))

    y_prompt = yp.reshape(batch, seq, d)
    y_sample = ys.reshape(dec_seq, dec_batch, d).swapaxes(0, 1)
    return (y_prompt, y_sample) + tuple(jnp.stack(o) for o in outs)
```

```python
import functools

import numpy as np
import jax
import jax.numpy as jnp
from jax import lax
from jax.experimental import pallas as pl
from jax.experimental.pallas import tpu as pltpu

HEAD_DIM = 64
ATT_HEADS = 8
ATT_WIDTH = ATT_HEADS * HEAD_DIM
CONV_K = 3
DILATED_PATTERNS = ((128, 1), (512, 4), (2048, 16))
MAX_SPAN = max(w for w, _ in DILATED_PATTERNS)
PAST_LEN = 8192
ROPE_THETA = 10000.0
NORM_EPS = 1e-6

LANES = 128
HEADS_PER_SLAB = LANES // HEAD_DIM
N_SLABS = ATT_WIDTH // LANES
LSE_LANES_PER_HEAD = LANES // ATT_HEADS
MASK_VALUE = -1e30
LOG2_E = 1.4426950408889634
VMEM_LIMIT_BYTES = 56 * 1024 * 1024

ROW_BLOCK = 512
Q_BLOCK = 128
UNITS_PER_STEP = 8
FF_CHUNK = 1024

BF16 = jnp.bfloat16
F32 = jnp.float32
NT_DIMS = (((1,), (1,)), ((), ()))


def _rmsnorm(x, g):
    return x * lax.rsqrt(jnp.mean(x * x, axis=-1, keepdims=True) + NORM_EPS) * g


def _rope(x, cos, sin_lo, sin_hi):
    half = HEAD_DIM // 2
    outs = []
    for c in range(N_SLABS):
        xs = x[:, c * LANES:(c + 1) * LANES]
        outs.append(xs * cos + pltpu.roll(xs, LANES - half, 1) * sin_lo
                    + pltpu.roll(xs, half, 1) * sin_hi)
    return jnp.concatenate(outs, axis=1)


def _project(x, gpre, win_ref, cos, sin_lo, sin_hi, q_scale):
    h = _rmsnorm(x, gpre).astype(BF16)

    def cols(c):
        return jnp.dot(h, win_ref[:, c * ATT_WIDTH:(c + 1) * ATT_WIDTH],
                       preferred_element_type=F32)

    q = _rope(cols(0), cos, sin_lo, sin_hi) * q_scale
    k = _rope(cols(1), cos, sin_lo, sin_hi)
    v = cols(2)
    gate_b = cols(3)
    gated = cols(4) * cols(5)
    return q, k, v, gate_b, gated


def _proj_prompt_kernel(x_ref, gpre_ref, win_ref, cos_ref, slo_ref, shi_ref, convw_ref,
                        gconv_ref, sq_ref, skn_ref, skt_ref, sbias_ref,
                        q_ref, k_ref, v_ref, kt_ref, vt_ref, cn_ref, cs_ref, spw_ref,
                        hist_ref, *, tm, blocks_per_seq, n_t, new_pad):
    _sample_scores_kernel(sq_ref, skn_ref, skt_ref, sbias_ref, spw_ref, n_t=n_t, new_pad=new_pad)
    j = pl.program_id(0) % blocks_per_seq
    row0 = pl.multiple_of(j * tm, tm)
    q, k, v, gate_b, gated = _project(
        x_ref[...], gpre_ref[...], win_ref, cos_ref[pl.ds(row0, tm), :],
        slo_ref[pl.ds(row0, tm), :], shi_ref[pl.ds(row0, tm), :], HEAD_DIM ** -0.5 * LOG2_E)
    for c in range(N_SLABS):
        q_ref[c] = q[:, c * LANES:(c + 1) * LANES]
        k_ref[c] = k[:, c * LANES:(c + 1) * LANES]
        v_ref[c] = v[:, c * LANES:(c + 1) * LANES]
    kt_ref[...] = k.T
    vt_ref[...] = v.T

    @pl.when(j == 0)
    def _():
        hist_ref[0:8, :] = jnp.zeros((8, gated.shape[1]), F32)

    @pl.when(j != 0)
    def _():
        hist_ref[0:8, :] = hist_ref[tm:tm + 8, :]

    hist_ref[8:8 + tm, :] = gated
    conv = (convw_ref[0:1, :] * hist_ref[6:6 + tm, :] + convw_ref[1:2, :] * hist_ref[7:7 + tm, :]
            + convw_ref[2:3, :] * gated)
    cn_ref[...] = _rmsnorm(gate_b * conv, gconv_ref[...]).astype(BF16)
    cs_ref[...] = gated[tm - (CONV_K - 1):tm, :]


def _proj_sample_kernel(x_ref, gpre_ref, win_ref, cos_ref, slo_ref, shi_ref, convw_ref,
                        gconv_ref, past_ref, q_ref, k_ref, v_ref, cn_ref, cs_ref, hist_ref,
                        *, rows, stride):
    q, k, v, gate_b, gated = _project(x_ref[...], gpre_ref[...], win_ref, cos_ref[...],
                                      slo_ref[...], shi_ref[...], HEAD_DIM ** -0.5)
    q_ref[...] = q
    k_ref[...] = k
    v_ref[...] = v
    npast = (CONV_K - 1) * stride
    hist_ref[0:npast, :] = past_ref[...]
    hist_ref[npast:npast + rows, :] = gated
    conv = (convw_ref[0:1, :] * hist_ref[0:rows, :]
            + convw_ref[1:2, :] * hist_ref[stride:stride + rows, :]
            + convw_ref[2:3, :] * gated)
    cn_ref[...] = _rmsnorm(gate_b * conv, gconv_ref[...]).astype(BF16)
    cs_ref[...] = hist_ref[rows:rows + npast, :]


def _stat_lane_group(c, hh):
    return (HEADS_PER_SLAB - 1 - hh) * (HEAD_DIM // LSE_LANES_PER_HEAD) + c


def _attend(load, store_o, store_stats, i, start, kv, reach, tq):
    lane = lax.broadcasted_iota(jnp.int32, (tq, LANES), 1)
    first_head = lane < HEAD_DIM
    group = lane // LSE_LANES_PER_HEAD
    qpos = i * tq + lax.broadcasted_iota(jnp.int32, (tq, kv), 0)
    kpos = start + lax.broadcasted_iota(jnp.int32, (tq, kv), 1)
    dist = qpos - kpos
    bias = jnp.where((dist >= 0) & (dist <= reach), 0.0, MASK_VALUE).astype(F32)
    max_tile = jnp.zeros((tq, LANES), F32)
    den_tile = jnp.zeros((tq, LANES), F32)
    key_lane = lax.broadcasted_iota(jnp.int32, (kv, LANES), 1)
    key_first = key_lane < HEAD_DIM
    for c in range(N_SLABS):
        q2 = load(0, c, i * tq, tq).astype(BF16)
        k2 = load(1, c, start, kv).astype(BF16)
        v2 = load(2, c, start, kv).astype(BF16)
        if kv == tq:
            zero = jnp.zeros_like(k2)
            k_both = jnp.concatenate([jnp.where(key_first, k2, zero), jnp.where(key_first, zero, k2)], 0)
            v_both = jnp.concatenate([jnp.where(key_first, v2, zero), jnp.where(key_first, zero, v2)], 0)
            s_both = lax.dot_general(q2, k_both, NT_DIMS, preferred_element_type=F32)
            probs = []
            for hh in range(HEADS_PER_SLAB):
                s = s_both[:, hh * kv:(hh + 1) * kv] + bias
                m = jnp.max(s, axis=1, keepdims=True)
                p = jnp.exp2(s - m).astype(BF16)
                den = jnp.sum(p.astype(F32), axis=1, keepdims=True)
                probs.append(p)
                mine = group == _stat_lane_group(c, hh)
                max_tile = jnp.where(mine, m, max_tile)
                den_tile = jnp.where(mine, den, den_tile)
            store_o(c, jnp.dot(jnp.concatenate(probs, axis=1), v_both, preferred_element_type=F32))
        else:
            one = jnp.ones_like(v2)
            outs = []
            for hh in range(HEADS_PER_SLAB):
                own = first_head if hh == 0 else ~first_head
                qm = jnp.where(own, q2, jnp.zeros_like(q2))
                s = lax.dot_general(qm, k2, NT_DIMS, preferred_element_type=F32) + bias
                m = jnp.max(s, axis=1, keepdims=True)
                p = jnp.exp2(s - m).astype(BF16)
                v_aug = jnp.where(key_first if hh == 0 else ~key_first, v2, one)
                r = jnp.dot(p, v_aug, preferred_element_type=F32)
                outs.append(r)
                mine = group == _stat_lane_group(c, hh)
                max_tile = jnp.where(mine, m, max_tile)
                den_tile = jnp.where(mine, r, den_tile)
            store_o(c, jnp.where(first_head, outs[0], outs[1]))
    store_stats(max_tile, den_tile)


def _prompt_attn_kernel(q_ref, k_ref, v_ref, expand_ref, gatt_ref, an_ref, o_scr, l_scr,
                        cls_scr, oc_scr, lc_scr, *, seq, tq):
    srcs = (q_ref, k_ref, v_ref)
    (near_w, near_d), (mid_w, mid_d), (far_w, far_d) = DILATED_PATTERNS
    inner = far_d // mid_d
    sub_mid, sub_far = seq // mid_d, seq // far_d
    nq_far = sub_far // tq

    def aligned(first, count):
        if isinstance(first, int):
            return pl.ds(first, count)
        return pl.ds(pl.multiple_of(first, tq), count)

    def near_unit(i, carry):
        start = jnp.maximum(i - 1, 0) * tq
        rows = aligned(i * tq, tq)

        def store_o(c, val):
            o_scr[0, c, rows, :] = val

        def store_stats(top, den):
            l_scr[0, 0, rows, :] = top
            l_scr[0, 1, rows, :] = den

        _attend(lambda a, c, first, count: srcs[a][c, aligned(first, count), :],
                store_o, store_stats, i, start, min(2 * tq, seq), near_w // near_d, tq)
        return carry

    lax.fori_loop(0, seq // tq, near_unit, 0, unroll=UNITS_PER_STEP)

    def mid_class(r, carry):
        for a in range(3):
            for c in range(N_SLABS):
                cls_scr[a, c] = srcs[a][c, pl.ds(r, sub_mid, stride=mid_d), :]

        def mid_unit(i, carry):
            start = jnp.maximum(i - 1, 0) * tq if sub_mid > tq else 0
            rows = pl.ds(r + mid_d * (i * tq), tq, stride=mid_d)

            def store_o(c, val):
                o_scr[1, c, rows, :] = val

            def store_stats(top, den):
                l_scr[1, 0, rows, :] = top
                l_scr[1, 1, rows, :] = den

            _attend(lambda a, c, first, count: cls_scr[a, c, aligned(first, count), :],
                    store_o, store_stats, i, start, min(2 * tq, sub_mid), mid_w // mid_d, tq)
            return carry

        def far_unit(n, carry):
            m = n >> (nq_far.bit_length() - 1)
            i = n & (nq_far - 1)
            start = jnp.maximum(i - 1, 0) * tq if sub_far > tq else 0
            rows = pl.ds(m + inner * (i * tq), tq, stride=inner)

            def store_o(c, val):
                oc_scr[c, rows, :] = val

            def store_stats(top, den):
                lc_scr[0, rows, :] = top
                lc_scr[1, rows, :] = den

            _attend(lambda a, c, first, count:
                    cls_scr[a, c, pl.ds(m + inner * first, count, stride=inner), :],
                    store_o, store_stats, i, start, min(2 * tq, sub_far), far_w // far_d, tq)
            return carry

        lax.fori_loop(0, sub_mid // tq, lambda n, carry: far_unit(n, mid_unit(n, carry)), 0,
                      unroll=UNITS_PER_STEP)
        back = pl.ds(r, sub_mid, stride=mid_d)
        for c in range(N_SLABS):
            o_scr[2, c, back, :] = oc_scr[c]
        for j in range(2):
            l_scr[2, j, back, :] = lc_scr[j]
        return carry

    lax.fori_loop(0, mid_d, mid_class, 0)

    n_pat = len(DILATED_PATTERNS)

    def merge(b, carry):
        r = pl.ds(pl.multiple_of(b * tq, tq), tq)
        tops = [l_scr[ip, 0, r, :] for ip in range(n_pat)]
        top = functools.reduce(jnp.maximum, tops)
        es = [jnp.exp2(t - top) for t in tops]
        inv = 1.0 / functools.reduce(lambda x, y: x + y,
                                     [e * l_scr[ip, 1, r, :] for ip, e in enumerate(es)])
        slabs = [None] * N_SLABS
        for ip, e in enumerate(es):
            wide = jnp.dot((e * inv).astype(BF16), expand_ref[...], preferred_element_type=F32)
            for c in range(N_SLABS):
                term = wide[:, c * LANES:(c + 1) * LANES] * o_scr[ip, c, r, :]
                slabs[c] = term if slabs[c] is None else slabs[c] + term
        attn = jnp.concatenate(slabs, axis=1)
        an_ref[r, :] = _rmsnorm(attn, gatt_ref[...]).astype(BF16)
        return carry

    lax.fori_loop(0, seq // tq, merge, 0, unroll=UNITS_PER_STEP)


def _sample_scores_kernel(q_ref, kn_ref, kt_ref, bias_ref, pw_ref, *, n_t, new_pad):
    width = ATT_WIDTH
    n_pat = len(DILATED_PATTERNS)
    per = n_t * ATT_HEADS
    sub = lax.broadcasted_iota(jnp.int32, (ATT_HEADS, width), 0)
    lane = lax.broadcasted_iota(jnp.int32, (ATT_HEADS, width), 1)
    own_head = sub == lane // HEAD_DIM
    span = kt_ref.shape[-1]
    pad = jnp.zeros((new_pad - n_t, width), F32)
    for j in range(kt_ref.shape[0]):
        qf = q_ref[:, j, 0, :]
        q_tiles = [jnp.where(own_head, jnp.broadcast_to(qf[t:t + 1, :], (ATT_HEADS, width)), 0.0)
                   for t in range(n_t)]
        qbd = jnp.concatenate(q_tiles * n_pat, axis=0).astype(BF16)
        kt = kt_ref[j].reshape(width, span).astype(BF16)
        kn = jnp.concatenate([kn_ref[:, j, 0, :], pad], axis=0).astype(BF16)
        s = jnp.concatenate(
            [jnp.dot(qbd, kt, preferred_element_type=F32),
             lax.dot_general(qbd, kn, NT_DIMS, preferred_element_type=F32)], axis=1) + bias_ref[...]
        m = jnp.max(s, axis=1, keepdims=True)
        p = jnp.exp(s - m)
        den = jnp.sum(p, axis=1, keepdims=True)
        lse = m + jnp.log(den)
        lses = [lse[i * per:(i + 1) * per, :] for i in range(n_pat)]
        top = functools.reduce(jnp.maximum, lses)
        es = [jnp.exp(l - top) for l in lses]
        z = functools.reduce(lambda a, b: a + b, es)
        scale = jnp.concatenate([e / z for e in es], axis=0) / den
        pw_ref[j] = (p * scale).astype(BF16)


def _sample_values_kernel(pw_ref, vn_ref, vt_ref, gatt_ref, o_ref, *, n_t):
    width = ATT_WIDTH
    n_pat = len(DILATED_PATTERNS)
    per = n_t * ATT_HEADS
    sub = lax.broadcasted_iota(jnp.int32, (ATT_HEADS, width), 0)
    lane = lax.broadcasted_iota(jnp.int32, (ATT_HEADS, width), 1)
    own_head = sub == lane // HEAD_DIM
    span = vt_ref.shape[-1]
    pad = jnp.zeros((pw_ref.shape[-1] - span - n_t, width), F32)
    for j in range(vt_ref.shape[0]):
        vt = vt_ref[j].reshape(width, span).astype(BF16)
        vn = jnp.concatenate([vn_ref[:, j, 0, :], pad], axis=0).astype(BF16)
        o = (lax.dot_general(pw_ref[j, :, :span], vt, NT_DIMS, preferred_element_type=F32)
             + jnp.dot(pw_ref[j, :, span:], vn, preferred_element_type=F32))
        for t in range(n_t):
            acc = o[t * ATT_HEADS:(t + 1) * ATT_HEADS, :]
            for i in range(1, n_pat):
                acc = acc + o[i * per + t * ATT_HEADS:i * per + (t + 1) * ATT_HEADS, :]
            attn = jnp.sum(jnp.where(own_head, acc, 0.0), axis=0, keepdims=True)
            o_ref[t, j, :, :] = _rmsnorm(attn, gatt_ref[...])


def _post_kernel(*refs, ff_chunk, sample_n_t):
    (x_ref, an_ref, cn_ref, wout_ref, npost_ref, npre_ref, wup_ref, wdown_ref, nmlp_ref) = refs[:9]
    if sample_n_t:
        _sample_values_kernel(*refs[9:13], refs[14], n_t=sample_n_t)
    y_ref = refs[13] if sample_n_t else refs[9]
    an = an_ref[...].astype(BF16)
    aw = an.shape[1]
    mixed = (jnp.dot(an, wout_ref[0:aw, :], preferred_element_type=F32)
             + jnp.dot(cn_ref[...], wout_ref[aw:, :], preferred_element_type=F32))
    x1 = x_ref[...] + _rmsnorm(mixed, npost_ref[...])
    h = _rmsnorm(x1, npre_ref[...]).astype(BF16)
    d_ff = wup_ref.shape[1]
    f = None
    for c in range(d_ff // ff_chunk):
        u = jnp.dot(h, wup_ref[:, c * ff_chunk:(c + 1) * ff_chunk], preferred_element_type=F32)
        a = jnp.square(jnp.maximum(u, 0.0)).astype(BF16)
        part = jnp.dot(a, wdown_ref[c * ff_chunk:(c + 1) * ff_chunk, :], preferred_element_type=F32)
        f = part if f is None else f + part
    y_ref[...] = x1 + _rmsnorm(f, nmlp_ref[...])


def _const_spec(shape):
    nd = len(shape)
    return pl.BlockSpec(shape, lambda *_: (0,) * nd, pipeline_mode=pl.Buffered(1))


def _rope_tables(positions):
    half = HEAD_DIM // 2
    inv = ROPE_THETA ** (-jnp.arange(half, dtype=F32) * 2.0 / HEAD_DIM)
    ang = positions.astype(F32)[:, None] * inv[None, :]
    cos, sin, zero = jnp.cos(ang), jnp.sin(ang), jnp.zeros_like(ang)
    reps = LANES // HEAD_DIM
    cos_t = jnp.tile(jnp.concatenate([cos, cos], axis=1), (1, reps))
    sin_lo = jnp.tile(jnp.concatenate([-sin, zero], axis=1), (1, reps))
    sin_hi = jnp.tile(jnp.concatenate([zero, sin], axis=1), (1, reps))
    return cos_t, sin_lo, sin_hi


def _params(*semantics):
    return pltpu.CompilerParams(dimension_semantics=semantics, vmem_limit_bytes=VMEM_LIMIT_BYTES)


def _sample_blocks(n_t, dec_batch, span, steps):
    assert dec_batch % steps == 0, "every grid step takes the same number of sample sequences"
    bb = dec_batch // steps
    new_blk = pl.BlockSpec((n_t, bb, 1, ATT_WIDTH), lambda i: (0, i, 0, 0))
    buf_blk = pl.BlockSpec((bb, ATT_HEADS, HEAD_DIM, span), lambda i: (i, 0, 0, 0))
    new_pad = -(-n_t // LANES) * LANES
    n_score = len(DILATED_PATTERNS) * n_t * ATT_HEADS
    pw_blk = pl.BlockSpec((bb, n_score, span + new_pad), lambda i: (i, 0, 0))
    return new_blk, buf_blk, pw_blk, new_pad, n_score


def _proj_prompt(x2d, seq, gpre, win, convw, gconv, sq, skn, cache_k, past_len):
    n, d = x2d.shape
    tm = min(ROW_BLOCK, seq)
    assert seq % tm == 0 and tm % 8 == 0
    bps = seq // tm
    batch = n // seq
    steps = n // tm
    cw = convw.shape[1]
    tables = _rope_tables(jnp.arange(seq))
    row = lambda w: pl.BlockSpec((tm, w), lambda i: (i, 0))
    slabbed = pl.BlockSpec((None, N_SLABS, tm, LANES), lambda i: (i // bps, 0, i % bps, 0))
    transposed = pl.BlockSpec((None, ATT_WIDTH, tm), lambda i: (i // bps, 0, i % bps))
    slab_shape = jax.ShapeDtypeStruct((batch, N_SLABS, seq, LANES), F32)
    t_shape = jax.ShapeDtypeStruct((batch, ATT_WIDTH, seq), F32)

    n_t, dec_batch = sq.shape[0], sq.shape[1]
    span = cache_k.shape[1]
    assert span == min(MAX_SPAN, past_len)
    new_blk, buf_blk, pw_blk, new_pad, n_score = _sample_blocks(n_t, dec_batch, span, steps)
    bias = _sample_bias(n_t, past_len, span, new_pad)
    skt = jnp.transpose(cache_k, (0, 2, 3, 1))

    out_shape = (slab_shape, slab_shape, slab_shape, t_shape, t_shape,
                 jax.ShapeDtypeStruct((n, cw), BF16),
                 jax.ShapeDtypeStruct((batch, CONV_K - 1, cw), F32),
                 jax.ShapeDtypeStruct((dec_batch, n_score, span + new_pad), BF16))
    return pl.pallas_call(
        functools.partial(_proj_prompt_kernel, tm=tm, blocks_per_seq=bps, n_t=n_t, new_pad=new_pad),
        grid=(steps,),
        in_specs=[row(d), _const_spec((1, d)), _const_spec(win.shape)]
        + [_const_spec((seq, LANES))] * 3 + [_const_spec(convw.shape), _const_spec((1, cw))]
        + [new_blk, new_blk, buf_blk, _const_spec(bias.shape)],
        out_specs=[slabbed, slabbed, slabbed, transposed, transposed, row(cw),
                   pl.BlockSpec((None, CONV_K - 1, cw), lambda i: (i // bps, 0, 0)), pw_blk],
        out_shape=out_shape,
        scratch_shapes=[pltpu.VMEM((tm + 8, cw), F32)],
        compiler_params=_params("arbitrary"),
        name="proj_prompt",
    )(x2d, gpre, win, *tables, convw, gconv, sq, skn, skt, bias)


def _proj_sample(x2d, positions, stride, past2d, gpre, win, convw, gconv):
    rows, d = x2d.shape
    cw = convw.shape[1]
    tables = [jnp.repeat(t, stride, axis=0) for t in _rope_tables(positions)]
    npast = (CONV_K - 1) * stride
    full = lambda shape: pl.BlockSpec(shape, lambda i: (0,) * len(shape))
    out_shape = (
        jax.ShapeDtypeStruct((rows, ATT_WIDTH), F32), jax.ShapeDtypeStruct((rows, ATT_WIDTH), F32),
        jax.ShapeDtypeStruct((rows, ATT_WIDTH), F32), jax.ShapeDtypeStruct((rows, cw), BF16),
        jax.ShapeDtypeStruct((npast, cw), F32))
    return pl.pallas_call(
        functools.partial(_proj_sample_kernel, rows=rows, stride=stride),
        grid=(1,),
        in_specs=[full((rows, d)), full((1, d)), full(win.shape)] + [full((rows, LANES))] * 3
        + [full(convw.shape), full((1, cw)), full((npast, cw))],
        out_specs=[full((rows, ATT_WIDTH))] * 3 + [full((rows, cw)), full((npast, cw))],
        out_shape=out_shape,
        scratch_shapes=[pltpu.VMEM((rows + npast, cw), F32)],
        compiler_params=_params("arbitrary"),
        name="proj_sample",
    )(x2d, gpre, win, *tables, convw, gconv, past2d)


def _prompt_attn(q, k, v, gatt):
    batch, _, seq, _ = q.shape
    tq = Q_BLOCK
    (_, near_d), (_, mid_d), (_, far_d) = DILATED_PATTERNS
    assert near_d == 1 and far_d % mid_d == 0
    for window, dil in DILATED_PATTERNS:
        sub = seq // dil
        assert seq % dil == 0 and sub % tq == 0 and window % dil == 0 and window // dil <= tq
        assert (sub // tq) & (sub // tq - 1) == 0
    sub_mid = seq // mid_d
    col_head = np.arange(ATT_WIDTH) // HEAD_DIM
    col_group = np.array([_stat_lane_group(h // HEADS_PER_SLAB, h % HEADS_PER_SLAB) for h in col_head])
    expand = jnp.asarray(np.arange(LANES)[:, None] == col_group[None, :] * LSE_LANES_PER_HEAD, BF16)
    n_pat = len(DILATED_PATTERNS)
    blk = pl.BlockSpec((None, N_SLABS, seq, LANES), lambda b: (b, 0, 0, 0))
    return pl.pallas_call(
        functools.partial(_prompt_attn_kernel, seq=seq, tq=tq),
        grid=(batch,),
        in_specs=[blk, blk, blk, _const_spec(expand.shape), _const_spec(gatt.shape)],
        out_specs=pl.BlockSpec((seq, ATT_WIDTH), lambda b: (b, 0)),
        out_shape=jax.ShapeDtypeStruct((batch * seq, ATT_WIDTH), BF16),
        scratch_shapes=[pltpu.VMEM((n_pat, N_SLABS, seq, LANES), F32),
                        pltpu.VMEM((n_pat, 2, seq, LANES), F32),
                        pltpu.VMEM((3, N_SLABS, sub_mid, LANES), F32),
                        pltpu.VMEM((N_SLABS, sub_mid, LANES), F32),
                        pltpu.VMEM((2, sub_mid, LANES), F32)],
        compiler_params=_params("arbitrary"),
        name="prompt_attn",
    )(q, k, v, expand, gatt)


def _sample_bias(n_t, past_len, span, new_pad):
    big = 1 << 30
    rel = np.concatenate([np.arange(span) - span, np.arange(n_t),
                          np.full((new_pad - n_t,), -big)])
    rows = []
    for window, dil in DILATED_PATTERNS:
        for t in range(n_t):
            dist = t - rel
            ok = (dist >= 0) & (dist <= window) & (dist % dil == 0) & (past_len + rel >= 0)
            rows += [np.where(ok, 0.0, MASK_VALUE)] * ATT_HEADS
    return jnp.asarray(np.stack(rows), F32)


def _post(x2d, an, cn, wout, npost, npre, wup, wdown, nmlp, sample=None):
    n, d = x2d.shape
    tm = min(ROW_BLOCK, n)
    assert n % tm == 0
    steps = n // tm
    row = lambda w: pl.BlockSpec((tm, w), lambda i: (i, 0))
    args = [x2d, an, cn, wout, npost, npre, wup, wdown, nmlp]
    in_specs = [row(d), row(an.shape[1]), row(cn.shape[1]), _const_spec(wout.shape),
                _const_spec((1, d)), _const_spec((1, d)), _const_spec(wup.shape),
                _const_spec(wdown.shape), _const_spec((1, d))]
    out_specs = [row(d)]
    out_shape = [jax.ShapeDtypeStruct((n, d), F32)]
    n_t = 0
    if sample:
        pw, svn, cache_v, gatt = sample
        n_t, dec_batch = svn.shape[0], svn.shape[1]
        span = cache_v.shape[1]
        new_blk, buf_blk, pw_blk, new_pad, n_score = _sample_blocks(n_t, dec_batch, span, steps)
        assert pw.shape == (dec_batch, n_score, span + new_pad)
        svt = jnp.transpose(cache_v, (0, 2, 3, 1))
        args += [pw, svn, svt, gatt]
        in_specs += [pw_blk, new_blk, buf_blk, _const_spec(gatt.shape)]
        out_specs.append(new_blk)
        out_shape.append(jax.ShapeDtypeStruct((n_t, dec_batch, 1, ATT_WIDTH), F32))
    outs = pl.pallas_call(
        functools.partial(_post_kernel, ff_chunk=min(FF_CHUNK, wup.shape[1]), sample_n_t=n_t),
        grid=(steps,),
        in_specs=in_specs,
        out_specs=out_specs,
        out_shape=out_shape,
        compiler_params=_params("arbitrary"),
        name="post",
    )(*args)
    return outs if sample else outs[0]


def kernel(x_prompt, x_sample, cache_k, cache_v, state_conv, n_att_pre, n_att_post, w_in, conv_w,
           g_att, g_conv, w_out, n_mlp_pre, n_mlp_post, w_up, w_down):
    depth = w_in.shape[0]
    batch, seq, d = x_prompt.shape
    dec_batch, dec_seq, _ = x_sample.shape
    past_len = PAST_LEN
    keep = min(MAX_SPAN, seq)
    assert keep == seq, "the prompt's window buffer is its whole key/value sequence"
    cw = conv_w.shape[2]

    yp = x_prompt.reshape(batch * seq, d)
    ys = x_sample.swapaxes(0, 1).reshape(dec_seq * dec_batch, d)
    outs = [[] for _ in range(6)]
    for l in range(depth):
        win, wout = w_in[l].astype(BF16), w_out[l].astype(BF16)
        wup, wdown = w_up[l].astype(BF16), w_down[l].astype(BF16)
        gpre, npost = n_att_pre[l][None, :], n_att_post[l][None, :]
        npre, nmlp = n_mlp_pre[l][None, :], n_mlp_post[l][None, :]
        gatt, gconv = g_att[l][None, :], g_conv[l][None, :]

        past = state_conv[l].swapaxes(0, 1).reshape((CONV_K - 1) * dec_batch, cw)
        qs, ks, vs, cns, css = _proj_sample(ys, past_len + jnp.arange(dec_seq), dec_batch, past,
                                            gpre, win, conv_w[l], gconv)
        tm4 = lambda a: a.reshape(dec_seq, dec_batch, 1, ATT_WIDTH)
        q, k, v, kt, vt, cn, cs, pw = _proj_prompt(yp, seq, gpre, win, conv_w[l], gconv,
                                                   tm4(qs), tm4(ks), cache_k[l], past_len)
        an = _prompt_attn(q, k, v, gatt)
        yp, ans = _post(yp, an, cn, wout, npost, npre, wup, wdown, nmlp,
                        sample=(pw, tm4(vs), cache_v[l], gatt))
        per_head = lambda a: a.reshape(batch, ATT_HEADS, HEAD_DIM, seq).transpose(0, 3, 1, 2)
        outs[0].append(per_head(kt))
        outs[1].append(per_head(vt))
        outs[2].append(cs)
        ys = _post(ys, ans.reshape(dec_seq * dec_batch, ATT_WIDTH), cns, wout, npost, npre,
                   wup, wdown, nmlp)
        bm = lambda a: a.reshape(dec_seq, dec_batch, ATT_HEADS, HEAD_DIM).swapaxes(0, 1)
        outs[3].append(bm(ks))
        outs[4].append(bm(vs))
        outs[5].append(css.reshape(CONV_K - 1, dec_batch, cw).swapaxes(0, 1))

    y_prompt = yp.reshape(batch, seq, d)
    y_sample = ys.reshape(dec_seq, dec_batch, d).swapaxes(0, 1)
    return (y_prompt, y_sample) + tuple(jnp.stack(o) for o in outs)
```

```python
import functools

import numpy as np
import jax
import jax.numpy as jnp
from jax import lax
from jax.experimental import pallas as pl
from jax.experimental.pallas import tpu as pltpu

HEAD_DIM = 64
ATT_HEADS = 8
ATT_WIDTH = ATT_HEADS * HEAD_DIM
CONV_K = 3
DILATED_PATTERNS = ((128, 1), (512, 4), (2048, 16))
MAX_SPAN = max(w for w, _ in DILATED_PATTERNS)
PAST_LEN = 8192
ROPE_THETA = 10000.0
NORM_EPS = 1e-6

LANES = 128
HEADS_PER_SLAB = LANES // HEAD_DIM
N_SLABS = ATT_WIDTH // LANES
LSE_LANES_PER_HEAD = LANES // ATT_HEADS
MASK_VALUE = -1e30
LOG2_E = 1.4426950408889634
VMEM_LIMIT_BYTES = 56 * 1024 * 1024

ROW_BLOCK = 512
Q_BLOCK = 128
UNITS_PER_STEP = 8
FF_CHUNK = 1024
ROW_GROUPS = 2

BF16 = jnp.bfloat16
F32 = jnp.float32
NT_DIMS = (((1,), (1,)), ((), ()))


def _rmsnorm(x, g):
    return x * lax.rsqrt(jnp.mean(x * x, axis=-1, keepdims=True) + NORM_EPS) * g


def _rope(x, cos, sin_lo, sin_hi):
    half = HEAD_DIM // 2
    outs = []
    for c in range(N_SLABS):
        xs = x[:, c * LANES:(c + 1) * LANES]
        outs.append(xs * cos + pltpu.roll(xs, LANES - half, 1) * sin_lo
                    + pltpu.roll(xs, half, 1) * sin_hi)
    return jnp.concatenate(outs, axis=1)


def _project(x, gpre, win_ref, cos, sin_lo, sin_hi, q_scale):
    h = _rmsnorm(x, gpre).astype(BF16)

    def cols(c):
        return jnp.dot(h, win_ref[:, c * ATT_WIDTH:(c + 1) * ATT_WIDTH],
                       preferred_element_type=F32)

    q = _rope(cols(0), cos, sin_lo, sin_hi) * q_scale
    k = _rope(cols(1), cos, sin_lo, sin_hi)
    v = cols(2)
    gate_b = cols(3)
    gated = cols(4) * cols(5)
    return q, k, v, gate_b, gated


def _proj_prompt_kernel(x_ref, gpre_ref, win_ref, cos_ref, slo_ref, shi_ref, convw_ref,
                        gconv_ref, sq_ref, skn_ref, skt_ref, sbias_ref,
                        q_ref, k_ref, v_ref, kt_ref, vt_ref, cn_ref, cs_ref, spw_ref,
                        hist_ref, *, tm, blocks_per_seq, n_t, new_pad):
    _sample_scores_kernel(sq_ref, skn_ref, skt_ref, sbias_ref, spw_ref, n_t=n_t, new_pad=new_pad)
    j = pl.program_id(0) % blocks_per_seq
    row0 = pl.multiple_of(j * tm, tm)
    q, k, v, gate_b, gated = _project(
        x_ref[...], gpre_ref[...], win_ref, cos_ref[pl.ds(row0, tm), :],
        slo_ref[pl.ds(row0, tm), :], shi_ref[pl.ds(row0, tm), :], HEAD_DIM ** -0.5 * LOG2_E)
    for c in range(N_SLABS):
        q_ref[c] = q[:, c * LANES:(c + 1) * LANES]
        k_ref[c] = k[:, c * LANES:(c + 1) * LANES]
        v_ref[c] = v[:, c * LANES:(c + 1) * LANES]
    kt_ref[...] = k.T
    vt_ref[...] = v.T

    @pl.when(j == 0)
    def _():
        hist_ref[0:8, :] = jnp.zeros((8, gated.shape[1]), F32)

    @pl.when(j != 0)
    def _():
        hist_ref[0:8, :] = hist_ref[tm:tm + 8, :]

    hist_ref[8:8 + tm, :] = gated
    conv = (convw_ref[0:1, :] * hist_ref[6:6 + tm, :] + convw_ref[1:2, :] * hist_ref[7:7 + tm, :]
            + convw_ref[2:3, :] * gated)
    cn_ref[...] = _rmsnorm(gate_b * conv, gconv_ref[...]).astype(BF16)
    cs_ref[...] = gated[tm - (CONV_K - 1):tm, :]


def _proj_sample_kernel(x_ref, gpre_ref, win_ref, cos_ref, slo_ref, shi_ref, convw_ref,
                        gconv_ref, past_ref, q_ref, k_ref, v_ref, cn_ref, cs_ref, hist_ref,
                        *, rows, stride):
    q, k, v, gate_b, gated = _project(x_ref[...], gpre_ref[...], win_ref, cos_ref[...],
                                      slo_ref[...], shi_ref[...], HEAD_DIM ** -0.5)
    q_ref[...] = q
    k_ref[...] = k
    v_ref[...] = v
    npast = (CONV_K - 1) * stride
    hist_ref[0:npast, :] = past_ref[...]
    hist_ref[npast:npast + rows, :] = gated
    conv = (convw_ref[0:1, :] * hist_ref[0:rows, :]
            + convw_ref[1:2, :] * hist_ref[stride:stride + rows, :]
            + convw_ref[2:3, :] * gated)
    cn_ref[...] = _rmsnorm(gate_b * conv, gconv_ref[...]).astype(BF16)
    cs_ref[...] = hist_ref[rows:rows + npast, :]


def _stat_lane_group(c, hh):
    return (HEADS_PER_SLAB - 1 - hh) * (HEAD_DIM // LSE_LANES_PER_HEAD) + c


def _attend(load, store_o, store_stats, i, start, kv, reach, tq):
    lane = lax.broadcasted_iota(jnp.int32, (tq, LANES), 1)
    first_head = lane < HEAD_DIM
    group = lane // LSE_LANES_PER_HEAD
    qpos = i * tq + lax.broadcasted_iota(jnp.int32, (tq, kv), 0)
    kpos = start + lax.broadcasted_iota(jnp.int32, (tq, kv), 1)
    dist = qpos - kpos
    bias = jnp.where((dist >= 0) & (dist <= reach), 0.0, MASK_VALUE).astype(F32)
    max_tile = jnp.zeros((tq, LANES), F32)
    den_tile = jnp.zeros((tq, LANES), F32)
    key_lane = lax.broadcasted_iota(jnp.int32, (kv, LANES), 1)
    key_first = key_lane < HEAD_DIM
    for c in range(N_SLABS):
        q2 = load(0, c, i * tq, tq).astype(BF16)
        k2 = load(1, c, start, kv).astype(BF16)
        v2 = load(2, c, start, kv).astype(BF16)
        if kv == tq:
            zero = jnp.zeros_like(k2)
            k_both = jnp.concatenate([jnp.where(key_first, k2, zero), jnp.where(key_first, zero, k2)], 0)
            v_both = jnp.concatenate([jnp.where(key_first, v2, zero), jnp.where(key_first, zero, v2)], 0)
            s_both = lax.dot_general(q2, k_both, NT_DIMS, preferred_element_type=F32)
            probs = []
            for hh in range(HEADS_PER_SLAB):
                s = s_both[:, hh * kv:(hh + 1) * kv] + bias
                m = jnp.max(s, axis=1, keepdims=True)
                p = jnp.exp2(s - m).astype(BF16)
                den = jnp.sum(p.astype(F32), axis=1, keepdims=True)
                probs.append(p)
                mine = group == _stat_lane_group(c, hh)
                max_tile = jnp.where(mine, m, max_tile)
                den_tile = jnp.where(mine, den, den_tile)
            store_o(c, jnp.dot(jnp.concatenate(probs, axis=1), v_both, preferred_element_type=F32))
        else:
            one = jnp.ones_like(v2)
            outs = []
            for hh in range(HEADS_PER_SLAB):
                own = first_head if hh == 0 else ~first_head
                qm = jnp.where(own, q2, jnp.zeros_like(q2))
                s = lax.dot_general(qm, k2, NT_DIMS, preferred_element_type=F32) + bias
                m = jnp.max(s, axis=1, keepdims=True)
                p = jnp.exp2(s - m).astype(BF16)
                v_aug = jnp.where(key_first if hh == 0 else ~key_first, v2, one)
                r = jnp.dot(p, v_aug, preferred_element_type=F32)
                outs.append(r)
                mine = group == _stat_lane_group(c, hh)
                max_tile = jnp.where(mine, m, max_tile)
                den_tile = jnp.where(mine, r, den_tile)
            store_o(c, jnp.where(first_head, outs[0], outs[1]))
    store_stats(max_tile, den_tile)


def _prompt_attn_kernel(q_ref, k_ref, v_ref, expand_ref, gatt_ref, an_ref, o_scr, l_scr,
                        cls_scr, oc_scr, lc_scr, *, seq, tq):
    srcs = (q_ref, k_ref, v_ref)
    (near_w, near_d), (mid_w, mid_d), (far_w, far_d) = DILATED_PATTERNS
    inner = far_d // mid_d
    sub_mid, sub_far = seq // mid_d, seq // far_d
    nq_far = sub_far // tq

    def aligned(first, count):
        if isinstance(first, int):
            return pl.ds(first, count)
        return pl.ds(pl.multiple_of(first, tq), count)

    def near_unit(i, carry):
        start = jnp.maximum(i - 1, 0) * tq
        rows = aligned(i * tq, tq)

        def store_o(c, val):
            o_scr[0, c, rows, :] = val

        def store_stats(top, den):
            l_scr[0, 0, rows, :] = top
            l_scr[0, 1, rows, :] = den

        _attend(lambda a, c, first, count: srcs[a][c, aligned(first, count), :],
                store_o, store_stats, i, start, min(2 * tq, seq), near_w // near_d, tq)
        return carry

    lax.fori_loop(0, seq // tq, near_unit, 0, unroll=UNITS_PER_STEP)

    def mid_class(r, carry):
        for a in range(3):
            for c in range(N_SLABS):
                cls_scr[a, c] = srcs[a][c, pl.ds(r, sub_mid, stride=mid_d), :]

        def mid_unit(i, carry):
            start = jnp.maximum(i - 1, 0) * tq if sub_mid > tq else 0
            rows = pl.ds(r + mid_d * (i * tq), tq, stride=mid_d)

            def store_o(c, val):
                o_scr[1, c, rows, :] = val

            def store_stats(top, den):
                l_scr[1, 0, rows, :] = top
                l_scr[1, 1, rows, :] = den

            _attend(lambda a, c, first, count: cls_scr[a, c, aligned(first, count), :],
                    store_o, store_stats, i, start, min(2 * tq, sub_mid), mid_w // mid_d, tq)
            return carry

        def far_unit(n, carry):
            m = n >> (nq_far.bit_length() - 1)
            i = n & (nq_far - 1)
            start = jnp.maximum(i - 1, 0) * tq if sub_far > tq else 0
            rows = pl.ds(m + inner * (i * tq), tq, stride=inner)

            def store_o(c, val):
                oc_scr[c, rows, :] = val

            def store_stats(top, den):
                lc_scr[0, rows, :] = top
                lc_scr[1, rows, :] = den

            _attend(lambda a, c, first, count:
                    cls_scr[a, c, pl.ds(m + inner * first, count, stride=inner), :],
                    store_o, store_stats, i, start, min(2 * tq, sub_far), far_w // far_d, tq)
            return carry

        lax.fori_loop(0, sub_mid // tq, lambda n, carry: far_unit(n, mid_unit(n, carry)), 0,
                      unroll=UNITS_PER_STEP)
        back = pl.ds(r, sub_mid, stride=mid_d)
        for c in range(N_SLABS):
            o_scr[2, c, back, :] = oc_scr[c]
        for j in range(2):
            l_scr[2, j, back, :] = lc_scr[j]
        return carry

    lax.fori_loop(0, mid_d, mid_class, 0)

    n_pat = len(DILATED_PATTERNS)

    def merge(b, carry):
        r = pl.ds(pl.multiple_of(b * tq, tq), tq)
        tops = [l_scr[ip, 0, r, :] for ip in range(n_pat)]
        top = functools.reduce(jnp.maximum, tops)
        es = [jnp.exp2(t - top) for t in tops]
        inv = 1.0 / functools.reduce(lambda x, y: x + y,
                                     [e * l_scr[ip, 1, r, :] for ip, e in enumerate(es)])
        slabs = [None] * N_SLABS
        for ip, e in enumerate(es):
            wide = jnp.dot((e * inv).astype(BF16), expand_ref[...], preferred_element_type=F32)
            for c in range(N_SLABS):
                term = wide[:, c * LANES:(c + 1) * LANES] * o_scr[ip, c, r, :]
                slabs[c] = term if slabs[c] is None else slabs[c] + term
        attn = jnp.concatenate(slabs, axis=1)
        an_ref[r, :] = _rmsnorm(attn, gatt_ref[...]).astype(BF16)
        return carry

    lax.fori_loop(0, seq // tq, merge, 0, unroll=UNITS_PER_STEP)


def _sample_scores_kernel(q_ref, kn_ref, kt_ref, bias_ref, pw_ref, *, n_t, new_pad):
    width = ATT_WIDTH
    n_pat = len(DILATED_PATTERNS)
    per = n_t * ATT_HEADS
    sub = lax.broadcasted_iota(jnp.int32, (ATT_HEADS, width), 0)
    lane = lax.broadcasted_iota(jnp.int32, (ATT_HEADS, width), 1)
    own_head = sub == lane // HEAD_DIM
    span = kt_ref.shape[-1]
    pad = jnp.zeros((new_pad - n_t, width), F32)
    for j in range(kt_ref.shape[0]):
        qf = q_ref[:, j, 0, :]
        q_tiles = [jnp.where(own_head, jnp.broadcast_to(qf[t:t + 1, :], (ATT_HEADS, width)), 0.0)
                   for t in range(n_t)]
        qbd = jnp.concatenate(q_tiles * n_pat, axis=0).astype(BF16)
        kt = kt_ref[j].reshape(width, span).astype(BF16)
        kn = jnp.concatenate([kn_ref[:, j, 0, :], pad], axis=0).astype(BF16)
        s = jnp.concatenate(
            [jnp.dot(qbd, kt, preferred_element_type=F32),
             lax.dot_general(qbd, kn, NT_DIMS, preferred_element_type=F32)], axis=1) + bias_ref[...]
        m = jnp.max(s, axis=1, keepdims=True)
        p = jnp.exp(s - m)
        den = jnp.sum(p, axis=1, keepdims=True)
        lse = m + jnp.log(den)
        lses = [lse[i * per:(i + 1) * per, :] for i in range(n_pat)]
        top = functools.reduce(jnp.maximum, lses)
        es = [jnp.exp(l - top) for l in lses]
        z = functools.reduce(lambda a, b: a + b, es)
        scale = jnp.concatenate([e / z for e in es], axis=0) / den
        pw_ref[j] = (p * scale).astype(BF16)


def _sample_values_kernel(pw_ref, vn_ref, vt_ref, gatt_ref, o_ref, *, n_t):
    width = ATT_WIDTH
    n_pat = len(DILATED_PATTERNS)
    per = n_t * ATT_HEADS
    sub = lax.broadcasted_iota(jnp.int32, (ATT_HEADS, width), 0)
    lane = lax.broadcasted_iota(jnp.int32, (ATT_HEADS, width), 1)
    own_head = sub == lane // HEAD_DIM
    span = vt_ref.shape[-1]
    pad = jnp.zeros((pw_ref.shape[-1] - span - n_t, width), F32)
    for j in range(vt_ref.shape[0]):
        vt = vt_ref[j].reshape(width, span).astype(BF16)
        vn = jnp.concatenate([vn_ref[:, j, 0, :], pad], axis=0).astype(BF16)
        o = (lax.dot_general(pw_ref[j, :, :span], vt, NT_DIMS, preferred_element_type=F32)
             + jnp.dot(pw_ref[j, :, span:], vn, preferred_element_type=F32))
        for t in range(n_t):
            acc = o[t * ATT_HEADS:(t + 1) * ATT_HEADS, :]
            for i in range(1, n_pat):
                acc = acc + o[i * per + t * ATT_HEADS:i * per + (t + 1) * ATT_HEADS, :]
            attn = jnp.sum(jnp.where(own_head, acc, 0.0), axis=0, keepdims=True)
            o_ref[t, j, :, :] = _rmsnorm(attn, gatt_ref[...])


def _post_kernel(*refs, ff_chunk, sample_n_t):
    (x_ref, an_ref, cn_ref, wout_ref, npost_ref, npre_ref, wup_ref, wdown_ref, nmlp_ref) = refs[:9]
    if sample_n_t:
        _sample_values_kernel(*refs[9:13], refs[14], n_t=sample_n_t)
    y_ref = refs[13] if sample_n_t else refs[9]
    aw = an_ref.shape[1]
    d_ff = wup_ref.shape[1]
    rows = x_ref.shape[0] // ROW_GROUPS
    groups = [slice(g * rows, (g + 1) * rows) for g in range(ROW_GROUPS)]
    mixed = [jnp.dot(an_ref[r, :].astype(BF16), wout_ref[0:aw, :], preferred_element_type=F32)
             + jnp.dot(cn_ref[r, :], wout_ref[aw:, :], preferred_element_type=F32) for r in groups]
    for r, mix in zip(groups, mixed):
        x1 = x_ref[r, :] + _rmsnorm(mix, npost_ref[...])
        h = _rmsnorm(x1, npre_ref[...]).astype(BF16)
        f = None
        for c in range(d_ff // ff_chunk):
            u = jnp.dot(h, wup_ref[:, c * ff_chunk:(c + 1) * ff_chunk], preferred_element_type=F32)
            a = jnp.square(jnp.maximum(u, 0.0)).astype(BF16)
            part = jnp.dot(a, wdown_ref[c * ff_chunk:(c + 1) * ff_chunk, :],
                           preferred_element_type=F32)
            f = part if f is None else f + part
        y_ref[r, :] = x1 + _rmsnorm(f, nmlp_ref[...])


def _const_spec(shape):
    nd = len(shape)
    return pl.BlockSpec(shape, lambda *_: (0,) * nd, pipeline_mode=pl.Buffered(1))


def _rope_tables(positions):
    half = HEAD_DIM // 2
    inv = ROPE_THETA ** (-jnp.arange(half, dtype=F32) * 2.0 / HEAD_DIM)
    ang = positions.astype(F32)[:, None] * inv[None, :]
    cos, sin, zero = jnp.cos(ang), jnp.sin(ang), jnp.zeros_like(ang)
    reps = LANES // HEAD_DIM
    cos_t = jnp.tile(jnp.concatenate([cos, cos], axis=1), (1, reps))
    sin_lo = jnp.tile(jnp.concatenate([-sin, zero], axis=1), (1, reps))
    sin_hi = jnp.tile(jnp.concatenate([zero, sin], axis=1), (1, reps))
    return cos_t, sin_lo, sin_hi


def _params(*semantics):
    return pltpu.CompilerParams(dimension_semantics=semantics, vmem_limit_bytes=VMEM_LIMIT_BYTES)


def _sample_blocks(n_t, dec_batch, span, steps):
    assert dec_batch % steps == 0, "every grid step takes the same number of sample sequences"
    bb = dec_batch // steps
    new_blk = pl.BlockSpec((n_t, bb, 1, ATT_WIDTH), lambda i: (0, i, 0, 0))
    buf_blk = pl.BlockSpec((bb, ATT_HEADS, HEAD_DIM, span), lambda i: (i, 0, 0, 0))
    new_pad = -(-n_t // LANES) * LANES
    n_score = len(DILATED_PATTERNS) * n_t * ATT_HEADS
    pw_blk = pl.BlockSpec((bb, n_score, span + new_pad), lambda i: (i, 0, 0))
    return new_blk, buf_blk, pw_blk, new_pad, n_score


def _proj_prompt(x2d, seq, gpre, win, convw, gconv, sq, skn, cache_k, past_len):
    n, d = x2d.shape
    tm = min(ROW_BLOCK, seq)
    assert seq % tm == 0 and tm % 8 == 0
    bps = seq // tm
    batch = n // seq
    steps = n // tm
    cw = convw.shape[1]
    tables = _rope_tables(jnp.arange(seq))
    row = lambda w: pl.BlockSpec((tm, w), lambda i: (i, 0))
    slabbed = pl.BlockSpec((None, N_SLABS, tm, LANES), lambda i: (i // bps, 0, i % bps, 0))
    transposed = pl.BlockSpec((None, ATT_WIDTH, tm), lambda i: (i // bps, 0, i % bps))
    slab_shape = jax.ShapeDtypeStruct((batch, N_SLABS, seq, LANES), F32)
    t_shape = jax.ShapeDtypeStruct((batch, ATT_WIDTH, seq), F32)

    n_t, dec_batch = sq.shape[0], sq.shape[1]
    span = cache_k.shape[1]
    assert span == min(MAX_SPAN, past_len)
    new_blk, buf_blk, pw_blk, new_pad, n_score = _sample_blocks(n_t, dec_batch, span, steps)
    bias = _sample_bias(n_t, past_len, span, new_pad)
    skt = jnp.transpose(cache_k, (0, 2, 3, 1))

    out_shape = (slab_shape, slab_shape, slab_shape, t_shape, t_shape,
                 jax.ShapeDtypeStruct((n, cw), BF16),
                 jax.ShapeDtypeStruct((batch, CONV_K - 1, cw), F32),
                 jax.ShapeDtypeStruct((dec_batch, n_score, span + new_pad), BF16))
    return pl.pallas_call(
        functools.partial(_proj_prompt_kernel, tm=tm, blocks_per_seq=bps, n_t=n_t, new_pad=new_pad),
        grid=(steps,),
        in_specs=[row(d), _const_spec((1, d)), _const_spec(win.shape)]
        + [_const_spec((seq, LANES))] * 3 + [_const_spec(convw.shape), _const_spec((1, cw))]
        + [new_blk, new_blk, buf_blk, _const_spec(bias.shape)],
        out_specs=[slabbed, slabbed, slabbed, transposed, transposed, row(cw),
                   pl.BlockSpec((None, CONV_K - 1, cw), lambda i: (i // bps, 0, 0)), pw_blk],
        out_shape=out_shape,
        scratch_shapes=[pltpu.VMEM((tm + 8, cw), F32)],
        compiler_params=_params("arbitrary"),
        name="proj_prompt",
    )(x2d, gpre, win, *tables, convw, gconv, sq, skn, skt, bias)


def _proj_sample(x2d, positions, stride, past2d, gpre, win, convw, gconv):
    rows, d = x2d.shape
    cw = convw.shape[1]
    tables = [jnp.repeat(t, stride, axis=0) for t in _rope_tables(positions)]
    npast = (CONV_K - 1) * stride
    full = lambda shape: pl.BlockSpec(shape, lambda i: (0,) * len(shape))
    out_shape = (
        jax.ShapeDtypeStruct((rows, ATT_WIDTH), F32), jax.ShapeDtypeStruct((rows, ATT_WIDTH), F32),
        jax.ShapeDtypeStruct((rows, ATT_WIDTH), F32), jax.ShapeDtypeStruct((rows, cw), BF16),
        jax.ShapeDtypeStruct((npast, cw), F32))
    return pl.pallas_call(
        functools.partial(_proj_sample_kernel, rows=rows, stride=stride),
        grid=(1,),
        in_specs=[full((rows, d)), full((1, d)), full(win.shape)] + [full((rows, LANES))] * 3
        + [full(convw.shape), full((1, cw)), full((npast, cw))],
        out_specs=[full((rows, ATT_WIDTH))] * 3 + [full((rows, cw)), full((npast, cw))],
        out_shape=out_shape,
        scratch_shapes=[pltpu.VMEM((rows + npast, cw), F32)],
        compiler_params=_params("arbitrary"),
        name="proj_sample",
    )(x2d, gpre, win, *tables, convw, gconv, past2d)


def _prompt_attn(q, k, v, gatt):
    batch, _, seq, _ = q.shape
    tq = Q_BLOCK
    (_, near_d), (_, mid_d), (_, far_d) = DILATED_PATTERNS
    assert near_d == 1 and far_d % mid_d == 0
    for window, dil in DILATED_PATTERNS:
        sub = seq // dil
        assert seq % dil == 0 and sub % tq == 0 and window % dil == 0 and window // dil <= tq
        assert (sub // tq) & (sub // tq - 1) == 0
    sub_mid = seq // mid_d
    col_head = np.arange(ATT_WIDTH) // HEAD_DIM
    col_group = np.array([_stat_lane_group(h // HEADS_PER_SLAB, h % HEADS_PER_SLAB) for h in col_head])
    expand = jnp.asarray(np.arange(LANES)[:, None] == col_group[None, :] * LSE_LANES_PER_HEAD, BF16)
    n_pat = len(DILATED_PATTERNS)
    blk = pl.BlockSpec((None, N_SLABS, seq, LANES), lambda b: (b, 0, 0, 0))
    return pl.pallas_call(
        functools.partial(_prompt_attn_kernel, seq=seq, tq=tq),
        grid=(batch,),
        in_specs=[blk, blk, blk, _const_spec(expand.shape), _const_spec(gatt.shape)],
        out_specs=pl.BlockSpec((seq, ATT_WIDTH), lambda b: (b, 0)),
        out_shape=jax.ShapeDtypeStruct((batch * seq, ATT_WIDTH), BF16),
        scratch_shapes=[pltpu.VMEM((n_pat, N_SLABS, seq, LANES), F32),
                        pltpu.VMEM((n_pat, 2, seq, LANES), F32),
                        pltpu.VMEM((3, N_SLABS, sub_mid, LANES), F32),
                        pltpu.VMEM((N_SLABS, sub_mid, LANES), F32),
                        pltpu.VMEM((2, sub_mid, LANES), F32)],
        compiler_params=_params("arbitrary"),
        name="prompt_attn",
    )(q, k, v, expand, gatt)


def _sample_bias(n_t, past_len, span, new_pad):
    big = 1 << 30
    rel = np.concatenate([np.arange(span) - span, np.arange(n_t),
                          np.full((new_pad - n_t,), -big)])
    rows = []
    for window, dil in DILATED_PATTERNS:
        for t in range(n_t):
            dist = t - rel
            ok = (dist >= 0) & (dist <= window) & (dist % dil == 0) & (past_len + rel >= 0)
            rows += [np.where(ok, 0.0, MASK_VALUE)] * ATT_HEADS
    return jnp.asarray(np.stack(rows), F32)


def _post(x2d, an, cn, wout, npost, npre, wup, wdown, nmlp, sample=None):
    n, d = x2d.shape
    tm = min(ROW_BLOCK, n)
    assert n % tm == 0
    steps = n // tm
    row = lambda w: pl.BlockSpec((tm, w), lambda i: (i, 0))
    args = [x2d, an, cn, wout, npost, npre, wup, wdown, nmlp]
    in_specs = [row(d), row(an.shape[1]), row(cn.shape[1]), _const_spec(wout.shape),
                _const_spec((1, d)), _const_spec((1, d)), _const_spec(wup.shape),
                _const_spec(wdown.shape), _const_spec((1, d))]
    out_specs = [row(d)]
    out_shape = [jax.ShapeDtypeStruct((n, d), F32)]
    n_t = 0
    if sample:
        pw, svn, cache_v, gatt = sample
        n_t, dec_batch = svn.shape[0], svn.shape[1]
        span = cache_v.shape[1]
        new_blk, buf_blk, pw_blk, new_pad, n_score = _sample_blocks(n_t, dec_batch, span, steps)
        assert pw.shape == (dec_batch, n_score, span + new_pad)
        svt = jnp.transpose(cache_v, (0, 2, 3, 1))
        args += [pw, svn, svt, gatt]
        in_specs += [pw_blk, new_blk, buf_blk, _const_spec(gatt.shape)]
        out_specs.append(new_blk)
        out_shape.append(jax.ShapeDtypeStruct((n_t, dec_batch, 1, ATT_WIDTH), F32))
    outs = pl.pallas_call(
        functools.partial(_post_kernel, ff_chunk=min(FF_CHUNK, wup.shape[1]), sample_n_t=n_t),
        grid=(steps,),
        in_specs=in_specs,
        out_specs=out_specs,
        out_shape=out_shape,
        compiler_params=_params("arbitrary"),
        name="post",
    )(*args)
    return outs if sample else outs[0]


def kernel(x_prompt, x_sample, cache_k, cache_v, state_conv, n_att_pre, n_att_post, w_in, conv_w,
           g_att, g_conv, w_out, n_mlp_pre, n_mlp_post, w_up, w_down):
    depth = w_in.shape[0]
    batch, seq, d = x_prompt.shape
    dec_batch, dec_seq, _ = x_sample.shape
    past_len = PAST_LEN
    keep = min(MAX_SPAN, seq)
    assert keep == seq, "the prompt's window buffer is its whole key/value sequence"
    cw = conv_w.shape[2]

    yp = x_prompt.reshape(batch * seq, d)
    ys = x_sample.swapaxes(0, 1).reshape(dec_seq * dec_batch, d)
    outs = [[] for _ in range(6)]
    for l in range(depth):
        win, wout = w_in[l].astype(BF16), w_out[l].astype(BF16)
        wup, wdown = w_up[l].astype(BF16), w_down[l].astype(BF16)
        gpre, npost = n_att_pre[l][None, :], n_att_post[l][None, :]
        npre, nmlp = n_mlp_pre[l][None, :], n_mlp_post[l][None, :]
        gatt, gconv = g_att[l][None, :], g_conv[l][None, :]

        past = state_conv[l].swapaxes(0, 1).reshape((CONV_K - 1) * dec_batch, cw)
        qs, ks, vs, cns, css = _proj_sample(ys, past_len + jnp.arange(dec_seq), dec_batch, past,
                                            gpre, win, conv_w[l], gconv)
        tm4 = lambda a: a.reshape(dec_seq, dec_batch, 1, ATT_WIDTH)
        q, k, v, kt, vt, cn, cs, pw = _proj_prompt(yp, seq, gpre, win, conv_w[l], gconv,
                                                   tm4(qs), tm4(ks), cache_k[l], past_len)
        an = _prompt_attn(q, k, v, gatt)
        yp, ans = _post(yp, an, cn, wout, npost, npre, wup, wdown, nmlp,
                        sample=(pw, tm4(vs), cache_v[l], gatt))
        per_head = lambda a: a.reshape(batch, ATT_HEADS, HEAD_DIM, seq).transpose(0, 3, 1, 2)
        outs[0].append(per_head(kt))
        outs[1].append(per_head(vt))
        outs[2].append(cs)
        ys = _post(ys, ans.reshape(dec_seq * dec_batch, ATT_WIDTH), cns, wout, npost, npre,
                   wup, wdown, nmlp)
        bm = lambda a: a.reshape(dec_seq, dec_batch, ATT_HEADS, HEAD_DIM).swapaxes(0, 1)
        outs[3].append(bm(ks))
        outs[4].append(bm(vs))
        outs[5].append(css.reshape(CONV_K - 1, dec_batch, cw).swapaxes(0, 1))

    y_prompt = yp.reshape(batch, seq, d)
    y_sample = ys.reshape(dec_seq, dec_batch, d).swapaxes(0, 1)
    return (y_prompt, y_sample) + tuple(jnp.stack(o) for o in outs)
```

```python
import functools

import numpy as np
import jax
import jax.numpy as jnp
from jax import lax
from jax.experimental import pallas as pl
from jax.experimental.pallas import tpu as pltpu

HEAD_DIM = 64
ATT_HEADS = 8
ATT_WIDTH = ATT_HEADS * HEAD_DIM
CONV_K = 3
DILATED_PATTERNS = ((128, 1), (512, 4), (2048, 16))
MAX_SPAN = max(w for w, _ in DILATED_PATTERNS)
PAST_LEN = 8192
ROPE_THETA = 10000.0
NORM_EPS = 1e-6

LANES = 128
HEADS_PER_SLAB = LANES // HEAD_DIM
N_SLABS = ATT_WIDTH // LANES
LSE_LANES_PER_HEAD = LANES // ATT_HEADS
MASK_VALUE = -1e30
LOG2_E = 1.4426950408889634
VMEM_LIMIT_BYTES = 60 * 1024 * 1024

ROW_BLOCK = 512
Q_BLOCK = 128
UNITS_PER_STEP = 8
CLASSES_PER_STEP = 2
FF_CHUNK = 1024
ROW_GROUPS = 2

BF16 = jnp.bfloat16
F32 = jnp.float32
NT_DIMS = (((1,), (1,)), ((), ()))


def _rmsnorm(x, g):
    return x * lax.rsqrt(jnp.mean(x * x, axis=-1, keepdims=True) + NORM_EPS) * g


def _rope(x, cos, sin_lo, sin_hi):
    half = HEAD_DIM // 2
    outs = []
    for c in range(N_SLABS):
        xs = x[:, c * LANES:(c + 1) * LANES]
        outs.append(xs * cos + pltpu.roll(xs, LANES - half, 1) * sin_lo
                    + pltpu.roll(xs, half, 1) * sin_hi)
    return jnp.concatenate(outs, axis=1)


def _project(x, gpre, win_ref, cos, sin_lo, sin_hi, q_scale):
    h = _rmsnorm(x, gpre).astype(BF16)

    def cols(c):
        return jnp.dot(h, win_ref[:, c * ATT_WIDTH:(c + 1) * ATT_WIDTH],
                       preferred_element_type=F32)

    q = _rope(cols(0), cos, sin_lo, sin_hi) * q_scale
    k = _rope(cols(1), cos, sin_lo, sin_hi)
    v = cols(2)
    gate_b = cols(3)
    gated = cols(4) * cols(5)
    return q, k, v, gate_b, gated


def _proj_prompt_kernel(x_ref, gpre_ref, win_ref, cos_ref, slo_ref, shi_ref, convw_ref,
                        gconv_ref, sq_ref, skn_ref, skt_ref, sbias_ref,
                        q_ref, k_ref, v_ref, kt_ref, vt_ref, cn_ref, cs_ref, spw_ref,
                        hist_ref, *, tm, blocks_per_seq, n_t, new_pad):
    _sample_scores_kernel(sq_ref, skn_ref, skt_ref, sbias_ref, spw_ref, n_t=n_t, new_pad=new_pad)
    j = pl.program_id(0) % blocks_per_seq
    row0 = pl.multiple_of(j * tm, tm)
    q, k, v, gate_b, gated = _project(
        x_ref[...], gpre_ref[...], win_ref, cos_ref[pl.ds(row0, tm), :],
        slo_ref[pl.ds(row0, tm), :], shi_ref[pl.ds(row0, tm), :], HEAD_DIM ** -0.5 * LOG2_E)
    for c in range(N_SLABS):
        q_ref[c] = q[:, c * LANES:(c + 1) * LANES]
        k_ref[c] = k[:, c * LANES:(c + 1) * LANES]
        v_ref[c] = v[:, c * LANES:(c + 1) * LANES]
    kt_ref[...] = k.T
    vt_ref[...] = v.T

    @pl.when(j == 0)
    def _():
        hist_ref[0:8, :] = jnp.zeros((8, gated.shape[1]), F32)

    @pl.when(j != 0)
    def _():
        hist_ref[0:8, :] = hist_ref[tm:tm + 8, :]

    hist_ref[8:8 + tm, :] = gated
    conv = (convw_ref[0:1, :] * hist_ref[6:6 + tm, :] + convw_ref[1:2, :] * hist_ref[7:7 + tm, :]
            + convw_ref[2:3, :] * gated)
    cn_ref[...] = _rmsnorm(gate_b * conv, gconv_ref[...]).astype(BF16)
    cs_ref[...] = gated[tm - (CONV_K - 1):tm, :]


def _proj_sample_kernel(x_ref, gpre_ref, win_ref, cos_ref, slo_ref, shi_ref, convw_ref,
                        gconv_ref, past_ref, q_ref, k_ref, v_ref, cn_ref, cs_ref, hist_ref,
                        *, rows, stride):
    q, k, v, gate_b, gated = _project(x_ref[...], gpre_ref[...], win_ref, cos_ref[...],
                                      slo_ref[...], shi_ref[...], HEAD_DIM ** -0.5)
    q_ref[...] = q
    k_ref[...] = k
    v_ref[...] = v
    npast = (CONV_K - 1) * stride
    hist_ref[0:npast, :] = past_ref[...]
    hist_ref[npast:npast + rows, :] = gated
    conv = (convw_ref[0:1, :] * hist_ref[0:rows, :]
            + convw_ref[1:2, :] * hist_ref[stride:stride + rows, :]
            + convw_ref[2:3, :] * gated)
    cn_ref[...] = _rmsnorm(gate_b * conv, gconv_ref[...]).astype(BF16)
    cs_ref[...] = hist_ref[rows:rows + npast, :]


def _stat_lane_group(c, hh):
    return (HEADS_PER_SLAB - 1 - hh) * (HEAD_DIM // LSE_LANES_PER_HEAD) + c


def _attend(load, store_o, store_stats, bias_ref, i, start, kv, tq):
    lane = lax.broadcasted_iota(jnp.int32, (tq, LANES), 1)
    first_head = lane < HEAD_DIM
    group = lane // LSE_LANES_PER_HEAD
    bias = bias_ref[0, :, :kv] if isinstance(start, int) else bias_ref[jnp.minimum(i, 1)]
    max_tile = jnp.zeros((tq, LANES), F32)
    den_tile = jnp.zeros((tq, LANES), F32)
    key_lane = lax.broadcasted_iota(jnp.int32, (kv, LANES), 1)
    key_first = key_lane < HEAD_DIM
    for c in range(N_SLABS):
        q2 = load(0, c, i * tq, tq).astype(BF16)
        k2 = load(1, c, start, kv).astype(BF16)
        v2 = load(2, c, start, kv).astype(BF16)
        if kv == tq:
            zero = jnp.zeros_like(k2)
            k_both = jnp.concatenate([jnp.where(key_first, k2, zero), jnp.where(key_first, zero, k2)], 0)
            v_both = jnp.concatenate([jnp.where(key_first, v2, zero), jnp.where(key_first, zero, v2)], 0)
            s_both = lax.dot_general(q2, k_both, NT_DIMS, preferred_element_type=F32)
            probs = []
            for hh in range(HEADS_PER_SLAB):
                s = s_both[:, hh * kv:(hh + 1) * kv] + bias
                m = jnp.max(s, axis=1, keepdims=True)
                p = jnp.exp2(s - m).astype(BF16)
                den = jnp.sum(p.astype(F32), axis=1, keepdims=True)
                probs.append(p)
                mine = group == _stat_lane_group(c, hh)
                max_tile = jnp.where(mine, m, max_tile)
                den_tile = jnp.where(mine, den, den_tile)
            store_o(c, jnp.dot(jnp.concatenate(probs, axis=1), v_both, preferred_element_type=F32))
        else:
            one = jnp.ones_like(v2)
            outs = []
            for hh in range(HEADS_PER_SLAB):
                own = first_head if hh == 0 else ~first_head
                qm = jnp.where(own, q2, jnp.zeros_like(q2))
                s = lax.dot_general(qm, k2, NT_DIMS, preferred_element_type=F32) + bias
                m = jnp.max(s, axis=1, keepdims=True)
                p = jnp.exp2(s - m).astype(BF16)
                v_aug = jnp.where(key_first if hh == 0 else ~key_first, v2, one)
                r = jnp.dot(p, v_aug, preferred_element_type=F32)
                outs.append(r)
                mine = group == _stat_lane_group(c, hh)
                max_tile = jnp.where(mine, m, max_tile)
                den_tile = jnp.where(mine, r, den_tile)
            store_o(c, jnp.where(first_head, outs[0], outs[1]))
    store_stats(max_tile, den_tile)


def _prompt_attn_kernel(q_ref, k_ref, v_ref, bias_ref, expand_ref, gatt_ref, an_ref, o_scr, l_scr,
                        cls_scr, oc_scr, lc_scr, *, seq, tq):
    srcs = (q_ref, k_ref, v_ref)
    (_, near_d), (_, mid_d), (_, far_d) = DILATED_PATTERNS
    inner = far_d // mid_d
    sub_mid, sub_far = seq // mid_d, seq // far_d
    nq_far = sub_far // tq

    def aligned(first, count):
        if isinstance(first, int):
            return pl.ds(first, count)
        return pl.ds(pl.multiple_of(first, tq), count)

    def near_unit(i, carry):
        start = jnp.maximum(i - 1, 0) * tq
        rows = aligned(i * tq, tq)

        def store_o(c, val):
            o_scr[0, c, rows, :] = val

        def store_stats(top, den):
            l_scr[0, 0, rows, :] = top
            l_scr[0, 1, rows, :] = den

        _attend(lambda a, c, first, count: srcs[a][c, aligned(first, count), :],
                store_o, store_stats, bias_ref, i, start, min(2 * tq, seq), tq)
        return carry

    lax.fori_loop(0, seq // tq, near_unit, 0, unroll=UNITS_PER_STEP)

    def mid_class(r, slot):
        for a in range(3):
            for c in range(N_SLABS):
                cls_scr[slot, a, c] = srcs[a][c, pl.ds(r, sub_mid, stride=mid_d), :]

        def mid_unit(i, carry):
            start = jnp.maximum(i - 1, 0) * tq if sub_mid > tq else 0
            rows = pl.ds(r + mid_d * (i * tq), tq, stride=mid_d)

            def store_o(c, val):
                o_scr[1, c, rows, :] = val

            def store_stats(top, den):
                l_scr[1, 0, rows, :] = top
                l_scr[1, 1, rows, :] = den

            _attend(lambda a, c, first, count: cls_scr[slot, a, c, aligned(first, count), :],
                    store_o, store_stats, bias_ref, i, start, min(2 * tq, sub_mid), tq)
            return carry

        def far_unit(n, carry):
            m = n >> (nq_far.bit_length() - 1)
            i = n & (nq_far - 1)
            start = jnp.maximum(i - 1, 0) * tq if sub_far > tq else 0
            rows = pl.ds(m + inner * (i * tq), tq, stride=inner)

            def store_o(c, val):
                oc_scr[slot, c, rows, :] = val

            def store_stats(top, den):
                lc_scr[slot, 0, rows, :] = top
                lc_scr[slot, 1, rows, :] = den

            _attend(lambda a, c, first, count:
                    cls_scr[slot, a, c, pl.ds(m + inner * first, count, stride=inner), :],
                    store_o, store_stats, bias_ref, i, start, min(2 * tq, sub_far), tq)
            return carry

        lax.fori_loop(0, sub_mid // tq, lambda n, carry: far_unit(n, mid_unit(n, carry)), 0,
                      unroll=UNITS_PER_STEP)
        back = pl.ds(r, sub_mid, stride=mid_d)
        for c in range(N_SLABS):
            o_scr[2, c, back, :] = oc_scr[slot, c]
        for j in range(2):
            l_scr[2, j, back, :] = lc_scr[slot, j]

    def class_step(n, carry):
        for slot in range(CLASSES_PER_STEP):
            mid_class(n * CLASSES_PER_STEP + slot, slot)
        return carry

    lax.fori_loop(0, mid_d // CLASSES_PER_STEP, class_step, 0)

    n_pat = len(DILATED_PATTERNS)

    def merge(b, carry):
        r = pl.ds(pl.multiple_of(b * tq, tq), tq)
        tops = [l_scr[ip, 0, r, :] for ip in range(n_pat)]
        top = functools.reduce(jnp.maximum, tops)
        es = [jnp.exp2(t - top) for t in tops]
        inv = 1.0 / functools.reduce(lambda x, y: x + y,
                                     [e * l_scr[ip, 1, r, :] for ip, e in enumerate(es)])
        slabs = [None] * N_SLABS
        for ip, e in enumerate(es):
            wide = jnp.dot((e * inv).astype(BF16), expand_ref[...], preferred_element_type=F32)
            for c in range(N_SLABS):
                term = wide[:, c * LANES:(c + 1) * LANES] * o_scr[ip, c, r, :]
                slabs[c] = term if slabs[c] is None else slabs[c] + term
        attn = jnp.concatenate(slabs, axis=1)
        an_ref[r, :] = _rmsnorm(attn, gatt_ref[...]).astype(BF16)
        return carry

    lax.fori_loop(0, seq // tq, merge, 0, unroll=UNITS_PER_STEP)


def _sample_scores_kernel(q_ref, kn_ref, kt_ref, bias_ref, pw_ref, *, n_t, new_pad):
    width = ATT_WIDTH
    n_pat = len(DILATED_PATTERNS)
    per = n_t * ATT_HEADS
    sub = lax.broadcasted_iota(jnp.int32, (ATT_HEADS, width), 0)
    lane = lax.broadcasted_iota(jnp.int32, (ATT_HEADS, width), 1)
    own_head = sub == lane // HEAD_DIM
    span = kt_ref.shape[-1]
    pad = jnp.zeros((new_pad - n_t, width), F32)
    for j in range(kt_ref.shape[0]):
        qf = q_ref[:, j, 0, :]
        q_tiles = [jnp.where(own_head, jnp.broadcast_to(qf[t:t + 1, :], (ATT_HEADS, width)), 0.0)
                   for t in range(n_t)]
        qbd = jnp.concatenate(q_tiles * n_pat, axis=0).astype(BF16)
        kt = kt_ref[j].reshape(width, span).astype(BF16)
        kn = jnp.concatenate([kn_ref[:, j, 0, :], pad], axis=0).astype(BF16)
        s = jnp.concatenate(
            [jnp.dot(qbd, kt, preferred_element_type=F32),
             lax.dot_general(qbd, kn, NT_DIMS, preferred_element_type=F32)], axis=1) + bias_ref[...]
        m = jnp.max(s, axis=1, keepdims=True)
        p = jnp.exp(s - m)
        den = jnp.sum(p, axis=1, keepdims=True)
        lse = m + jnp.log(den)
        lses = [lse[i * per:(i + 1) * per, :] for i in range(n_pat)]
        top = functools.reduce(jnp.maximum, lses)
        es = [jnp.exp(l - top) for l in lses]
        z = functools.reduce(lambda a, b: a + b, es)
        scale = jnp.concatenate([e / z for e in es], axis=0) / den
        pw_ref[j] = (p * scale).astype(BF16)


def _sample_values_kernel(pw_ref, vn_ref, vt_ref, gatt_ref, o_ref, *, n_t):
    width = ATT_WIDTH
    n_pat = len(DILATED_PATTERNS)
    per = n_t * ATT_HEADS
    sub = lax.broadcasted_iota(jnp.int32, (ATT_HEADS, width), 0)
    lane = lax.broadcasted_iota(jnp.int32, (ATT_HEADS, width), 1)
    own_head = sub == lane // HEAD_DIM
    span = vt_ref.shape[-1]
    pad = jnp.zeros((pw_ref.shape[-1] - span - n_t, width), F32)
    for j in range(vt_ref.shape[0]):
        vt = vt_ref[j].reshape(width, span).astype(BF16)
        vn = jnp.concatenate([vn_ref[:, j, 0, :], pad], axis=0).astype(BF16)
        o = (lax.dot_general(pw_ref[j, :, :span], vt, NT_DIMS, preferred_element_type=F32)
             + jnp.dot(pw_ref[j, :, span:], vn, preferred_element_type=F32))
        for t in range(n_t):
            acc = o[t * ATT_HEADS:(t + 1) * ATT_HEADS, :]
            for i in range(1, n_pat):
                acc = acc + o[i * per + t * ATT_HEADS:i * per + (t + 1) * ATT_HEADS, :]
            attn = jnp.sum(jnp.where(own_head, acc, 0.0), axis=0, keepdims=True)
            o_ref[t, j, :, :] = _rmsnorm(attn, gatt_ref[...])


def _post_kernel(*refs, ff_chunk, sample_n_t):
    (x_ref, an_ref, cn_ref, wout_ref, npost_ref, npre_ref, wup_ref, wdown_ref, nmlp_ref) = refs[:9]
    if sample_n_t:
        _sample_values_kernel(*refs[9:13], refs[14], n_t=sample_n_t)
    y_ref = refs[13] if sample_n_t else refs[9]
    aw = an_ref.shape[1]
    d_ff = wup_ref.shape[1]
    rows = x_ref.shape[0] // ROW_GROUPS
    groups = [slice(g * rows, (g + 1) * rows) for g in range(ROW_GROUPS)]
    mixed = [jnp.dot(an_ref[r, :].astype(BF16), wout_ref[0:aw, :], preferred_element_type=F32)
             + jnp.dot(cn_ref[r, :], wout_ref[aw:, :], preferred_element_type=F32) for r in groups]
    for r, mix in zip(groups, mixed):
        x1 = x_ref[r, :] + _rmsnorm(mix, npost_ref[...])
        h = _rmsnorm(x1, npre_ref[...]).astype(BF16)
        f = None
        for c in range(d_ff // ff_chunk):
            u = jnp.dot(h, wup_ref[:, c * ff_chunk:(c + 1) * ff_chunk], preferred_element_type=F32)
            a = jnp.square(jnp.maximum(u, 0.0)).astype(BF16)
            part = jnp.dot(a, wdown_ref[c * ff_chunk:(c + 1) * ff_chunk, :],
                           preferred_element_type=F32)
            f = part if f is None else f + part
        y_ref[r, :] = x1 + _rmsnorm(f, nmlp_ref[...])


def _const_spec(shape):
    nd = len(shape)
    return pl.BlockSpec(shape, lambda *_: (0,) * nd, pipeline_mode=pl.Buffered(1))


def _rope_tables(positions):
    half = HEAD_DIM // 2
    inv = ROPE_THETA ** (-jnp.arange(half, dtype=F32) * 2.0 / HEAD_DIM)
    ang = positions.astype(F32)[:, None] * inv[None, :]
    cos, sin, zero = jnp.cos(ang), jnp.sin(ang), jnp.zeros_like(ang)
    reps = LANES // HEAD_DIM
    cos_t = jnp.tile(jnp.concatenate([cos, cos], axis=1), (1, reps))
    sin_lo = jnp.tile(jnp.concatenate([-sin, zero], axis=1), (1, reps))
    sin_hi = jnp.tile(jnp.concatenate([zero, sin], axis=1), (1, reps))
    return cos_t, sin_lo, sin_hi


def _params(*semantics):
    return pltpu.CompilerParams(dimension_semantics=semantics, vmem_limit_bytes=VMEM_LIMIT_BYTES)


def _sample_blocks(n_t, dec_batch, span, steps):
    assert dec_batch % steps == 0, "every grid step takes the same number of sample sequences"
    bb = dec_batch // steps
    new_blk = pl.BlockSpec((n_t, bb, 1, ATT_WIDTH), lambda i: (0, i, 0, 0))
    buf_blk = pl.BlockSpec((bb, ATT_HEADS, HEAD_DIM, span), lambda i: (i, 0, 0, 0))
    new_pad = -(-n_t // LANES) * LANES
    n_score = len(DILATED_PATTERNS) * n_t * ATT_HEADS
    pw_blk = pl.BlockSpec((bb, n_score, span + new_pad), lambda i: (i, 0, 0))
    return new_blk, buf_blk, pw_blk, new_pad, n_score


def _proj_prompt(x2d, seq, gpre, win, convw, gconv, sq, skn, cache_k, past_len):
    n, d = x2d.shape
    tm = min(ROW_BLOCK, seq)
    assert seq % tm == 0 and tm % 8 == 0
    bps = seq // tm
    batch = n // seq
    steps = n // tm
    cw = convw.shape[1]
    tables = _rope_tables(jnp.arange(seq))
    row = lambda w: pl.BlockSpec((tm, w), lambda i: (i, 0))
    slabbed = pl.BlockSpec((None, N_SLABS, tm, LANES), lambda i: (i // bps, 0, i % bps, 0))
    transposed = pl.BlockSpec((None, ATT_WIDTH, tm), lambda i: (i // bps, 0, i % bps))
    slab_shape = jax.ShapeDtypeStruct((batch, N_SLABS, seq, LANES), F32)
    t_shape = jax.ShapeDtypeStruct((batch, ATT_WIDTH, seq), F32)

    n_t, dec_batch = sq.shape[0], sq.shape[1]
    span = cache_k.shape[1]
    assert span == min(MAX_SPAN, past_len)
    new_blk, buf_blk, pw_blk, new_pad, n_score = _sample_blocks(n_t, dec_batch, span, steps)
    bias = _sample_bias(n_t, past_len, span, new_pad)
    skt = jnp.transpose(cache_k, (0, 2, 3, 1))

    out_shape = (slab_shape, slab_shape, slab_shape, t_shape, t_shape,
                 jax.ShapeDtypeStruct((n, cw), BF16),
                 jax.ShapeDtypeStruct((batch, CONV_K - 1, cw), F32),
                 jax.ShapeDtypeStruct((dec_batch, n_score, span + new_pad), BF16))
    return pl.pallas_call(
        functools.partial(_proj_prompt_kernel, tm=tm, blocks_per_seq=bps, n_t=n_t, new_pad=new_pad),
        grid=(steps,),
        in_specs=[row(d), _const_spec((1, d)), _const_spec(win.shape)]
        + [_const_spec((seq, LANES))] * 3 + [_const_spec(convw.shape), _const_spec((1, cw))]
        + [new_blk, new_blk, buf_blk, _const_spec(bias.shape)],
        out_specs=[slabbed, slabbed, slabbed, transposed, transposed, row(cw),
                   pl.BlockSpec((None, CONV_K - 1, cw), lambda i: (i // bps, 0, 0)), pw_blk],
        out_shape=out_shape,
        scratch_shapes=[pltpu.VMEM((tm + 8, cw), F32)],
        compiler_params=_params("arbitrary"),
        name="proj_prompt",
    )(x2d, gpre, win, *tables, convw, gconv, sq, skn, skt, bias)


def _proj_sample(x2d, positions, stride, past2d, gpre, win, convw, gconv):
    rows, d = x2d.shape
    cw = convw.shape[1]
    tables = [jnp.repeat(t, stride, axis=0) for t in _rope_tables(positions)]
    npast = (CONV_K - 1) * stride
    full = lambda shape: pl.BlockSpec(shape, lambda i: (0,) * len(shape))
    out_shape = (
        jax.ShapeDtypeStruct((rows, ATT_WIDTH), F32), jax.ShapeDtypeStruct((rows, ATT_WIDTH), F32),
        jax.ShapeDtypeStruct((rows, ATT_WIDTH), F32), jax.ShapeDtypeStruct((rows, cw), BF16),
        jax.ShapeDtypeStruct((npast, cw), F32))
    return pl.pallas_call(
        functools.partial(_proj_sample_kernel, rows=rows, stride=stride),
        grid=(1,),
        in_specs=[full((rows, d)), full((1, d)), full(win.shape)] + [full((rows, LANES))] * 3
        + [full(convw.shape), full((1, cw)), full((npast, cw))],
        out_specs=[full((rows, ATT_WIDTH))] * 3 + [full((rows, cw)), full((npast, cw))],
        out_shape=out_shape,
        scratch_shapes=[pltpu.VMEM((rows + npast, cw), F32)],
        compiler_params=_params("arbitrary"),
        name="proj_sample",
    )(x2d, gpre, win, *tables, convw, gconv, past2d)


def _prompt_attn(q, k, v, gatt):
    batch, _, seq, _ = q.shape
    tq = Q_BLOCK
    (_, near_d), (_, mid_d), (_, far_d) = DILATED_PATTERNS
    assert near_d == 1 and far_d % mid_d == 0 and mid_d % CLASSES_PER_STEP == 0
    for window, dil in DILATED_PATTERNS:
        sub = seq // dil
        assert seq % dil == 0 and sub % tq == 0 and window % dil == 0 and window // dil <= tq
        assert (sub // tq) & (sub // tq - 1) == 0
    sub_mid = seq // mid_d
    reaches = {window // dil for window, dil in DILATED_PATTERNS}
    assert len(reaches) == 1, "one mask table serves every pattern"
    dist = np.arange(tq)[None, :, None] - np.arange(2 * tq)[None, None, :] + tq * np.arange(2)[:, None, None]
    bias = jnp.asarray(np.where((dist >= 0) & (dist <= reaches.pop()), 0.0, MASK_VALUE), F32)
    col_head = np.arange(ATT_WIDTH) // HEAD_DIM
    col_group = np.array([_stat_lane_group(h // HEADS_PER_SLAB, h % HEADS_PER_SLAB) for h in col_head])
    expand = jnp.asarray(np.arange(LANES)[:, None] == col_group[None, :] * LSE_LANES_PER_HEAD, BF16)
    n_pat = len(DILATED_PATTERNS)
    blk = pl.BlockSpec((None, N_SLABS, seq, LANES), lambda b: (b, 0, 0, 0))
    return pl.pallas_call(
        functools.partial(_prompt_attn_kernel, seq=seq, tq=tq),
        grid=(batch,),
        in_specs=[blk, blk, blk, _const_spec(bias.shape), _const_spec(expand.shape),
                  _const_spec(gatt.shape)],
        out_specs=pl.BlockSpec((seq, ATT_WIDTH), lambda b: (b, 0)),
        out_shape=jax.ShapeDtypeStruct((batch * seq, ATT_WIDTH), BF16),
        scratch_shapes=[pltpu.VMEM((n_pat, N_SLABS, seq, LANES), F32),
                        pltpu.VMEM((n_pat, 2, seq, LANES), F32),
                        pltpu.VMEM((CLASSES_PER_STEP, 3, N_SLABS, sub_mid, LANES), F32),
                        pltpu.VMEM((CLASSES_PER_STEP, N_SLABS, sub_mid, LANES), F32),
                        pltpu.VMEM((CLASSES_PER_STEP, 2, sub_mid, LANES), F32)],
        compiler_params=_params("arbitrary"),
        name="prompt_attn",
    )(q, k, v, bias, expand, gatt)


def _sample_bias(n_t, past_len, span, new_pad):
    big = 1 << 30
    rel = np.concatenate([np.arange(span) - span, np.arange(n_t),
                          np.full((new_pad - n_t,), -big)])
    rows = []
    for window, dil in DILATED_PATTERNS:
        for t in range(n_t):
            dist = t - rel
            ok = (dist >= 0) & (dist <= window) & (dist % dil == 0) & (past_len + rel >= 0)
            rows += [np.where(ok, 0.0, MASK_VALUE)] * ATT_HEADS
    return jnp.asarray(np.stack(rows), F32)


def _post(x2d, an, cn, wout, npost, npre, wup, wdown, nmlp, sample=None):
    n, d = x2d.shape
    tm = min(ROW_BLOCK, n)
    assert n % tm == 0
    steps = n // tm
    row = lambda w: pl.BlockSpec((tm, w), lambda i: (i, 0))
    args = [x2d, an, cn, wout, npost, npre, wup, wdown, nmlp]
    in_specs = [row(d), row(an.shape[1]), row(cn.shape[1]), _const_spec(wout.shape),
                _const_spec((1, d)), _const_spec((1, d)), _const_spec(wup.shape),
                _const_spec(wdown.shape), _const_spec((1, d))]
    out_specs = [row(d)]
    out_shape = [jax.ShapeDtypeStruct((n, d), F32)]
    n_t = 0
    if sample:
        pw, svn, cache_v, gatt = sample
        n_t, dec_batch = svn.shape[0], svn.shape[1]
        span = cache_v.shape[1]
        new_blk, buf_blk, pw_blk, new_pad, n_score = _sample_blocks(n_t, dec_batch, span, steps)
        assert pw.shape == (dec_batch, n_score, span + new_pad)
        svt = jnp.transpose(cache_v, (0, 2, 3, 1))
        args += [pw, svn, svt, gatt]
        in_specs += [pw_blk, new_blk, buf_blk, _const_spec(gatt.shape)]
        out_specs.append(new_blk)
        out_shape.append(jax.ShapeDtypeStruct((n_t, dec_batch, 1, ATT_WIDTH), F32))
    outs = pl.pallas_call(
        functools.partial(_post_kernel, ff_chunk=min(FF_CHUNK, wup.shape[1]), sample_n_t=n_t),
        grid=(steps,),
        in_specs=in_specs,
        out_specs=out_specs,
        out_shape=out_shape,
        compiler_params=_params("arbitrary"),
        name="post",
    )(*args)
    return outs if sample else outs[0]


def kernel(x_prompt, x_sample, cache_k, cache_v, state_conv, n_att_pre, n_att_post, w_in, conv_w,
           g_att, g_conv, w_out, n_mlp_pre, n_mlp_post, w_up, w_down):
    depth = w_in.shape[0]
    batch, seq, d = x_prompt.shape
    dec_batch, dec_seq, _ = x_sample.shape
    past_len = PAST_LEN
    keep = min(MAX_SPAN, seq)
    assert keep == seq, "the prompt's window buffer is its whole key/value sequence"
    cw = conv_w.shape[2]

    yp = x_prompt.reshape(batch * seq, d)
    ys = x_sample.swapaxes(0, 1).reshape(dec_seq * dec_batch, d)
    outs = [[] for _ in range(6)]
    for l in range(depth):
        win, wout = w_in[l].astype(BF16), w_out[l].astype(BF16)
        wup, wdown = w_up[l].astype(BF16), w_down[l].astype(BF16)
        gpre, npost = n_att_pre[l][None, :], n_att_post[l][None, :]
        npre, nmlp = n_mlp_pre[l][None, :], n_mlp_post[l][None, :]
        gatt, gconv = g_att[l][None, :], g_conv[l][None, :]

        past = state_conv[l].swapaxes(0, 1).reshape((CONV_K - 1) * dec_batch, cw)
        qs, ks, vs, cns, css = _proj_sample(ys, past_len + jnp.arange(dec_seq), dec_batch, past,
                                            gpre, win, conv_w[l], gconv)
        tm4 = lambda a: a.reshape(dec_seq, dec_batch, 1, ATT_WIDTH)
        q, k, v, kt, vt, cn, cs, pw = _proj_prompt(yp, seq, gpre, win, conv_w[l], gconv,
                                                   tm4(qs), tm4(ks), cache_k[l], past_len)
        an = _prompt_attn(q, k, v, gatt)
        yp, ans = _post(yp, an, cn, wout, npost, npre, wup, wdown, nmlp,
                        sample=(pw, tm4(vs), cache_v[l], gatt))
        per_head = lambda a: a.reshape(batch, ATT_HEADS, HEAD_DIM, seq).transpose(0, 3, 1, 2)
        outs[0].append(per_head(kt))
        outs[1].append(per_head(vt))
        outs[2].append(cs)
        ys = _post(ys, ans.reshape(dec_seq * dec_batch, ATT_WIDTH), cns, wout, npost, npre,
                   wup, wdown, nmlp)
        bm = lambda a: a.reshape(dec_seq, dec_batch, ATT_HEADS, HEAD_DIM).swapaxes(0, 1)
        outs[3].append(bm(ks))
        outs[4].append(bm(vs))
        outs[5].append(css.reshape(CONV_K - 1, dec_batch, cw).swapaxes(0, 1))

    y_prompt = yp.reshape(batch, seq, d)
    y_sample = ys.reshape(dec_seq, dec_batch, d).swapaxes(0, 1)
    return (y_prompt, y_sample) + tuple(jnp.stack(o) for o in outs)
```

```python
import functools

import numpy as np
import jax
import jax.numpy as jnp
from jax import lax
from jax.experimental import pallas as pl
from jax.experimental.pallas import tpu as pltpu

HEAD_DIM = 64
ATT_HEADS = 8
ATT_WIDTH = ATT_HEADS * HEAD_DIM
CONV_K = 3
DILATED_PATTERNS = ((128, 1), (512, 4), (2048, 16))
MAX_SPAN = max(w for w, _ in DILATED_PATTERNS)
PAST_LEN = 8192
ROPE_THETA = 10000.0
NORM_EPS = 1e-6

LANES = 128
HEADS_PER_SLAB = LANES // HEAD_DIM
N_SLABS = ATT_WIDTH // LANES
LSE_LANES_PER_HEAD = LANES // ATT_HEADS
MASK_VALUE = -1e30
LOG2_E = 1.4426950408889634
VMEM_LIMIT_BYTES = 60 * 1024 * 1024

ROW_BLOCK = 512
Q_BLOCK = 128
UNITS_PER_STEP = 16
CLASSES_PER_STEP = 2
FF_CHUNK = 1024
ROW_GROUPS = 2

BF16 = jnp.bfloat16
F32 = jnp.float32
NT_DIMS = (((1,), (1,)), ((), ()))


def _rmsnorm(x, g):
    return x * lax.rsqrt(jnp.mean(x * x, axis=-1, keepdims=True) + NORM_EPS) * g


def _rope(x, cos, sin_lo, sin_hi):
    half = HEAD_DIM // 2
    outs = []
    for c in range(N_SLABS):
        xs = x[:, c * LANES:(c + 1) * LANES]
        outs.append(xs * cos + pltpu.roll(xs, LANES - half, 1) * sin_lo
                    + pltpu.roll(xs, half, 1) * sin_hi)
    return jnp.concatenate(outs, axis=1)


def _in_projection(x, gpre, win_ref):
    h = _rmsnorm(x, gpre).astype(BF16)
    return [jnp.dot(h, win_ref[:, c * ATT_WIDTH:(c + 1) * ATT_WIDTH], preferred_element_type=F32)
            for c in range(win_ref.shape[1] // ATT_WIDTH)]


def _split_projection(cols, cos, sin_lo, sin_hi, q_scale):
    q = _rope(cols[0], cos, sin_lo, sin_hi) * q_scale
    k = _rope(cols[1], cos, sin_lo, sin_hi)
    return q, k, cols[2], cols[3], cols[4] * cols[5]


def _proj_prompt_kernel(x_ref, gpre_ref, win_ref, cos_ref, slo_ref, shi_ref, convw_ref,
                        gconv_ref, sq_ref, skn_ref, skt_ref, sbias_ref,
                        q_ref, k_ref, v_ref, kt_ref, vt_ref, cn_ref, cs_ref, spw_ref,
                        hist_ref, *, tm, blocks_per_seq, n_t, new_pad):
    _sample_scores_kernel(sq_ref, skn_ref, skt_ref, sbias_ref, spw_ref, n_t=n_t, new_pad=new_pad)
    j = pl.program_id(0) % blocks_per_seq
    rows = tm // ROW_GROUPS
    groups = [slice(g * rows, (g + 1) * rows) for g in range(ROW_GROUPS)]
    raw = [_in_projection(x_ref[r, :], gpre_ref[...], win_ref) for r in groups]

    hist_ref[0:8, :] = jnp.where(j == 0, 0.0, hist_ref[tm:tm + 8, :])
    for g, r in enumerate(groups):
        t0 = pl.multiple_of(j * tm + g * rows, rows)
        q, k, v, gate_b, gated = _split_projection(
            raw[g], cos_ref[pl.ds(t0, rows), :], slo_ref[pl.ds(t0, rows), :],
            shi_ref[pl.ds(t0, rows), :], HEAD_DIM ** -0.5 * LOG2_E)
        for c in range(N_SLABS):
            q_ref[c, r, :] = q[:, c * LANES:(c + 1) * LANES]
            k_ref[c, r, :] = k[:, c * LANES:(c + 1) * LANES]
            v_ref[c, r, :] = v[:, c * LANES:(c + 1) * LANES]
        kt_ref[:, r] = k.T
        vt_ref[:, r] = v.T
        lo = 8 + g * rows
        hist_ref[lo:lo + rows, :] = gated
        conv = (convw_ref[0:1, :] * hist_ref[lo - 2:lo - 2 + rows, :]
                + convw_ref[1:2, :] * hist_ref[lo - 1:lo - 1 + rows, :] + convw_ref[2:3, :] * gated)
        cn_ref[r, :] = _rmsnorm(gate_b * conv, gconv_ref[...]).astype(BF16)
    cs_ref[...] = gated[rows - (CONV_K - 1):rows, :]


def _proj_sample_kernel(x_ref, gpre_ref, win_ref, cos_ref, slo_ref, shi_ref, convw_ref,
                        gconv_ref, past_ref, q_ref, k_ref, v_ref, cn_ref, cs_ref, hist_ref,
                        *, rows, stride):
    q, k, v, gate_b, gated = _split_projection(
        _in_projection(x_ref[...], gpre_ref[...], win_ref), cos_ref[...], slo_ref[...],
        shi_ref[...], HEAD_DIM ** -0.5)
    q_ref[...] = q
    k_ref[...] = k
    v_ref[...] = v
    npast = (CONV_K - 1) * stride
    hist_ref[0:npast, :] = past_ref[...]
    hist_ref[npast:npast + rows, :] = gated
    conv = (convw_ref[0:1, :] * hist_ref[0:rows, :]
            + convw_ref[1:2, :] * hist_ref[stride:stride + rows, :]
            + convw_ref[2:3, :] * gated)
    cn_ref[...] = _rmsnorm(gate_b * conv, gconv_ref[...]).astype(BF16)
    cs_ref[...] = hist_ref[rows:rows + npast, :]


def _stat_lane_group(c, hh):
    return (HEADS_PER_SLAB - 1 - hh) * (HEAD_DIM // LSE_LANES_PER_HEAD) + c


def _attend(load, store_o, store_stats, bias_ref, i, start, kv, tq):
    lane = lax.broadcasted_iota(jnp.int32, (tq, LANES), 1)
    first_head = lane < HEAD_DIM
    group = lane // LSE_LANES_PER_HEAD
    bias = bias_ref[0, :, :kv] if isinstance(start, int) else bias_ref[jnp.minimum(i, 1)]
    max_tile = jnp.zeros((tq, LANES), F32)
    den_tile = jnp.zeros((tq, LANES), F32)
    key_lane = lax.broadcasted_iota(jnp.int32, (kv, LANES), 1)
    key_first = key_lane < HEAD_DIM
    for c in range(N_SLABS):
        q2 = load(0, c, i * tq, tq).astype(BF16)
        k2 = load(1, c, start, kv).astype(BF16)
        v2 = load(2, c, start, kv).astype(BF16)
        if kv == tq:
            zero = jnp.zeros_like(k2)
            k_both = jnp.concatenate([jnp.where(key_first, k2, zero), jnp.where(key_first, zero, k2)], 0)
            v_both = jnp.concatenate([jnp.where(key_first, v2, zero), jnp.where(key_first, zero, v2)], 0)
            s_both = lax.dot_general(q2, k_both, NT_DIMS, preferred_element_type=F32)
            probs = []
            for hh in range(HEADS_PER_SLAB):
                s = s_both[:, hh * kv:(hh + 1) * kv] + bias
                m = jnp.max(s, axis=1, keepdims=True)
                p = jnp.exp2(s - m).astype(BF16)
                den = jnp.sum(p.astype(F32), axis=1, keepdims=True)
                probs.append(p)
                mine = group == _stat_lane_group(c, hh)
                max_tile = jnp.where(mine, m, max_tile)
                den_tile = jnp.where(mine, den, den_tile)
            store_o(c, jnp.dot(jnp.concatenate(probs, axis=1), v_both, preferred_element_type=F32))
        else:
            one = jnp.ones_like(v2)
            outs = []
            for hh in range(HEADS_PER_SLAB):
                own = first_head if hh == 0 else ~first_head
                qm = jnp.where(own, q2, jnp.zeros_like(q2))
                s = lax.dot_general(qm, k2, NT_DIMS, preferred_element_type=F32) + bias
                m = jnp.max(s, axis=1, keepdims=True)
                p = jnp.exp2(s - m).astype(BF16)
                v_aug = jnp.where(key_first if hh == 0 else ~key_first, v2, one)
                r = jnp.dot(p, v_aug, preferred_element_type=F32)
                outs.append(r)
                mine = group == _stat_lane_group(c, hh)
                max_tile = jnp.where(mine, m, max_tile)
                den_tile = jnp.where(mine, r, den_tile)
            store_o(c, jnp.where(first_head, outs[0], outs[1]))
    store_stats(max_tile, den_tile)


def _prompt_attn_kernel(q_ref, k_ref, v_ref, bias_ref, expand_ref, gatt_ref, an_ref, o_scr, l_scr,
                        cls_scr, oc_scr, lc_scr, *, seq, tq):
    srcs = (q_ref, k_ref, v_ref)
    (_, near_d), (_, mid_d), (_, far_d) = DILATED_PATTERNS
    inner = far_d // mid_d
    sub_mid, sub_far = seq // mid_d, seq // far_d
    nq_far = sub_far // tq

    def aligned(first, count):
        if isinstance(first, int):
            return pl.ds(first, count)
        return pl.ds(pl.multiple_of(first, tq), count)

    def near_unit(i, carry):
        start = jnp.maximum(i - 1, 0) * tq
        rows = aligned(i * tq, tq)

        def store_o(c, val):
            o_scr[0, c, rows, :] = val

        def store_stats(top, den):
            l_scr[0, 0, rows, :] = top
            l_scr[0, 1, rows, :] = den

        _attend(lambda a, c, first, count: srcs[a][c, aligned(first, count), :],
                store_o, store_stats, bias_ref, i, start, min(2 * tq, seq), tq)
        return carry

    lax.fori_loop(0, seq // tq, near_unit, 0, unroll=UNITS_PER_STEP)

    def mid_class(r, slot):
        for a in range(3):
            for c in range(N_SLABS):
                cls_scr[slot, a, c] = srcs[a][c, pl.ds(r, sub_mid, stride=mid_d), :]

        def mid_unit(i, carry):
            start = jnp.maximum(i - 1, 0) * tq if sub_mid > tq else 0
            rows = pl.ds(r + mid_d * (i * tq), tq, stride=mid_d)

            def store_o(c, val):
                o_scr[1, c, rows, :] = val

            def store_stats(top, den):
                l_scr[1, 0, rows, :] = top
                l_scr[1, 1, rows, :] = den

            _attend(lambda a, c, first, count: cls_scr[slot, a, c, aligned(first, count), :],
                    store_o, store_stats, bias_ref, i, start, min(2 * tq, sub_mid), tq)
            return carry

        def far_unit(n, carry):
            m = n >> (nq_far.bit_length() - 1)
            i = n & (nq_far - 1)
            start = jnp.maximum(i - 1, 0) * tq if sub_far > tq else 0
            rows = pl.ds(m + inner * (i * tq), tq, stride=inner)

            def store_o(c, val):
                oc_scr[slot, c, rows, :] = val

            def store_stats(top, den):
                lc_scr[slot, 0, rows, :] = top
                lc_scr[slot, 1, rows, :] = den

            _attend(lambda a, c, first, count:
                    cls_scr[slot, a, c, pl.ds(m + inner * first, count, stride=inner), :],
                    store_o, store_stats, bias_ref, i, start, min(2 * tq, sub_far), tq)
            return carry

        lax.fori_loop(0, sub_mid // tq, lambda n, carry: far_unit(n, mid_unit(n, carry)), 0,
                      unroll=UNITS_PER_STEP)
        back = pl.ds(r, sub_mid, stride=mid_d)
        for c in range(N_SLABS):
            o_scr[2, c, back, :] = oc_scr[slot, c]
        for j in range(2):
            l_scr[2, j, back, :] = lc_scr[slot, j]

    def class_step(n, carry):
        for slot in range(CLASSES_PER_STEP):
            mid_class(n * CLASSES_PER_STEP + slot, slot)
        return carry

    lax.fori_loop(0, mid_d // CLASSES_PER_STEP, class_step, 0)

    n_pat = len(DILATED_PATTERNS)

    def merge(b, carry):
        r = pl.ds(pl.multiple_of(b * tq, tq), tq)
        tops = [l_scr[ip, 0, r, :] for ip in range(n_pat)]
        top = functools.reduce(jnp.maximum, tops)
        es = [jnp.exp2(t - top) for t in tops]
        inv = 1.0 / functools.reduce(lambda x, y: x + y,
                                     [e * l_scr[ip, 1, r, :] for ip, e in enumerate(es)])
        slabs = [None] * N_SLABS
        for ip, e in enumerate(es):
            wide = jnp.dot((e * inv).astype(BF16), expand_ref[...], preferred_element_type=F32)
            for c in range(N_SLABS):
                term = wide[:, c * LANES:(c + 1) * LANES] * o_scr[ip, c, r, :]
                slabs[c] = term if slabs[c] is None else slabs[c] + term
        attn = jnp.concatenate(slabs, axis=1)
        an_ref[r, :] = _rmsnorm(attn, gatt_ref[...]).astype(BF16)
        return carry

    lax.fori_loop(0, seq // tq, merge, 0, unroll=UNITS_PER_STEP)


def _sample_scores_kernel(q_ref, kn_ref, kt_ref, bias_ref, pw_ref, *, n_t, new_pad):
    width = ATT_WIDTH
    n_pat = len(DILATED_PATTERNS)
    per = n_t * ATT_HEADS
    sub = lax.broadcasted_iota(jnp.int32, (ATT_HEADS, width), 0)
    lane = lax.broadcasted_iota(jnp.int32, (ATT_HEADS, width), 1)
    own_head = sub == lane // HEAD_DIM
    span = kt_ref.shape[-1]
    pad = jnp.zeros((new_pad - n_t, width), F32)
    for j in range(kt_ref.shape[0]):
        qf = q_ref[:, j, 0, :]
        q_tiles = [jnp.where(own_head, jnp.broadcast_to(qf[t:t + 1, :], (ATT_HEADS, width)), 0.0)
                   for t in range(n_t)]
        qbd = jnp.concatenate(q_tiles * n_pat, axis=0).astype(BF16)
        kt = kt_ref[j].reshape(width, span).astype(BF16)
        kn = jnp.concatenate([kn_ref[:, j, 0, :], pad], axis=0).astype(BF16)
        s = jnp.concatenate(
            [jnp.dot(qbd, kt, preferred_element_type=F32),
             lax.dot_general(qbd, kn, NT_DIMS, preferred_element_type=F32)], axis=1) + bias_ref[...]
        m = jnp.max(s, axis=1, keepdims=True)
        p = jnp.exp(s - m)
        den = jnp.sum(p, axis=1, keepdims=True)
        lse = m + jnp.log(den)
        lses = [lse[i * per:(i + 1) * per, :] for i in range(n_pat)]
        top = functools.reduce(jnp.maximum, lses)
        es = [jnp.exp(l - top) for l in lses]
        z = functools.reduce(lambda a, b: a + b, es)
        scale = jnp.concatenate([e / z for e in es], axis=0) / den
        pw_ref[j] = (p * scale).astype(BF16)


def _sample_values_kernel(pw_ref, vn_ref, vt_ref, gatt_ref, o_ref, *, n_t):
    width = ATT_WIDTH
    n_pat = len(DILATED_PATTERNS)
    per = n_t * ATT_HEADS
    sub = lax.broadcasted_iota(jnp.int32, (ATT_HEADS, width), 0)
    lane = lax.broadcasted_iota(jnp.int32, (ATT_HEADS, width), 1)
    own_head = sub == lane // HEAD_DIM
    span = vt_ref.shape[-1]
    pad = jnp.zeros((pw_ref.shape[-1] - span - n_t, width), F32)
    for j in range(vt_ref.shape[0]):
        vt = vt_ref[j].reshape(width, span).astype(BF16)
        vn = jnp.concatenate([vn_ref[:, j, 0, :], pad], axis=0).astype(BF16)
        o = (lax.dot_general(pw_ref[j, :, :span], vt, NT_DIMS, preferred_element_type=F32)
             + jnp.dot(pw_ref[j, :, span:], vn, preferred_element_type=F32))
        for t in range(n_t):
            acc = o[t * ATT_HEADS:(t + 1) * ATT_HEADS, :]
            for i in range(1, n_pat):
                acc = acc + o[i * per + t * ATT_HEADS:i * per + (t + 1) * ATT_HEADS, :]
            attn = jnp.sum(jnp.where(own_head, acc, 0.0), axis=0, keepdims=True)
            o_ref[t, j, :, :] = _rmsnorm(attn, gatt_ref[...])


def _post_kernel(*refs, ff_chunk, sample_n_t):
    (x_ref, an_ref, cn_ref, wout_ref, npost_ref, npre_ref, wup_ref, wdown_ref, nmlp_ref) = refs[:9]
    if sample_n_t:
        _sample_values_kernel(*refs[9:13], refs[14], n_t=sample_n_t)
    y_ref = refs[13] if sample_n_t else refs[9]
    aw = an_ref.shape[1]
    d_ff = wup_ref.shape[1]
    rows = x_ref.shape[0] // ROW_GROUPS
    groups = [slice(g * rows, (g + 1) * rows) for g in range(ROW_GROUPS)]
    mixed = [jnp.dot(an_ref[r, :].astype(BF16), wout_ref[0:aw, :], preferred_element_type=F32)
             + jnp.dot(cn_ref[r, :], wout_ref[aw:, :], preferred_element_type=F32) for r in groups]
    for r, mix in zip(groups, mixed):
        x1 = x_ref[r, :] + _rmsnorm(mix, npost_ref[...])
        h = _rmsnorm(x1, npre_ref[...]).astype(BF16)
        f = None
        for c in range(d_ff // ff_chunk):
            u = jnp.dot(h, wup_ref[:, c * ff_chunk:(c + 1) * ff_chunk], preferred_element_type=F32)
            a = jnp.square(jnp.maximum(u, 0.0)).astype(BF16)
            part = jnp.dot(a, wdown_ref[c * ff_chunk:(c + 1) * ff_chunk, :],
                           preferred_element_type=F32)
            f = part if f is None else f + part
        y_ref[r, :] = x1 + _rmsnorm(f, nmlp_ref[...])


def _const_spec(shape):
    nd = len(shape)
    return pl.BlockSpec(shape, lambda *_: (0,) * nd, pipeline_mode=pl.Buffered(1))


def _rope_tables(positions):
    half = HEAD_DIM // 2
    inv = ROPE_THETA ** (-jnp.arange(half, dtype=F32) * 2.0 / HEAD_DIM)
    ang = positions.astype(F32)[:, None] * inv[None, :]
    cos, sin, zero = jnp.cos(ang), jnp.sin(ang), jnp.zeros_like(ang)
    reps = LANES // HEAD_DIM
    cos_t = jnp.tile(jnp.concatenate([cos, cos], axis=1), (1, reps))
    sin_lo = jnp.tile(jnp.concatenate([-sin, zero], axis=1), (1, reps))
    sin_hi = jnp.tile(jnp.concatenate([zero, sin], axis=1), (1, reps))
    return cos_t, sin_lo, sin_hi


def _params(*semantics):
    return pltpu.CompilerParams(dimension_semantics=semantics, vmem_limit_bytes=VMEM_LIMIT_BYTES)


def _sample_blocks(n_t, dec_batch, span, steps):
    assert dec_batch % steps == 0, "every grid step takes the same number of sample sequences"
    bb = dec_batch // steps
    new_blk = pl.BlockSpec((n_t, bb, 1, ATT_WIDTH), lambda i: (0, i, 0, 0))
    buf_blk = pl.BlockSpec((bb, ATT_HEADS, HEAD_DIM, span), lambda i: (i, 0, 0, 0))
    new_pad = -(-n_t // LANES) * LANES
    n_score = len(DILATED_PATTERNS) * n_t * ATT_HEADS
    pw_blk = pl.BlockSpec((bb, n_score, span + new_pad), lambda i: (i, 0, 0))
    return new_blk, buf_blk, pw_blk, new_pad, n_score


def _proj_prompt(x2d, seq, gpre, win, convw, gconv, sq, skn, cache_k, past_len):
    n, d = x2d.shape
    tm = min(ROW_BLOCK, seq)
    assert seq % tm == 0 and tm % 8 == 0
    bps = seq // tm
    batch = n // seq
    steps = n // tm
    cw = convw.shape[1]
    tables = _rope_tables(jnp.arange(seq))
    row = lambda w: pl.BlockSpec((tm, w), lambda i: (i, 0))
    slabbed = pl.BlockSpec((None, N_SLABS, tm, LANES), lambda i: (i // bps, 0, i % bps, 0))
    transposed = pl.BlockSpec((None, ATT_WIDTH, tm), lambda i: (i // bps, 0, i % bps))
    slab_shape = jax.ShapeDtypeStruct((batch, N_SLABS, seq, LANES), F32)
    t_shape = jax.ShapeDtypeStruct((batch, ATT_WIDTH, seq), F32)

    n_t, dec_batch = sq.shape[0], sq.shape[1]
    span = cache_k.shape[1]
    assert span == min(MAX_SPAN, past_len)
    new_blk, buf_blk, pw_blk, new_pad, n_score = _sample_blocks(n_t, dec_batch, span, steps)
    bias = _sample_bias(n_t, past_len, span, new_pad)
    skt = jnp.transpose(cache_k, (0, 2, 3, 1))

    out_shape = (slab_shape, slab_shape, slab_shape, t_shape, t_shape,
                 jax.ShapeDtypeStruct((n, cw), BF16),
                 jax.ShapeDtypeStruct((batch, CONV_K - 1, cw), F32),
                 jax.ShapeDtypeStruct((dec_batch, n_score, span + new_pad), BF16))
    return pl.pallas_call(
        functools.partial(_proj_prompt_kernel, tm=tm, blocks_per_seq=bps, n_t=n_t, new_pad=new_pad),
        grid=(steps,),
        in_specs=[row(d), _const_spec((1, d)), _const_spec(win.shape)]
        + [_const_spec((seq, LANES))] * 3 + [_const_spec(convw.shape), _const_spec((1, cw))]
        + [new_blk, new_blk, buf_blk, _const_spec(bias.shape)],
        out_specs=[slabbed, slabbed, slabbed, transposed, transposed, row(cw),
                   pl.BlockSpec((None, CONV_K - 1, cw), lambda i: (i // bps, 0, 0)), pw_blk],
        out_shape=out_shape,
        scratch_shapes=[pltpu.VMEM((tm + 8, cw), F32)],
        compiler_params=_params("arbitrary"),
        name="proj_prompt",
    )(x2d, gpre, win, *tables, convw, gconv, sq, skn, skt, bias)


def _proj_sample(x2d, positions, stride, past2d, gpre, win, convw, gconv):
    rows, d = x2d.shape
    cw = convw.shape[1]
    tables = [jnp.repeat(t, stride, axis=0) for t in _rope_tables(positions)]
    npast = (CONV_K - 1) * stride
    full = lambda shape: pl.BlockSpec(shape, lambda i: (0,) * len(shape))
    out_shape = (
        jax.ShapeDtypeStruct((rows, ATT_WIDTH), F32), jax.ShapeDtypeStruct((rows, ATT_WIDTH), F32),
        jax.ShapeDtypeStruct((rows, ATT_WIDTH), F32), jax.ShapeDtypeStruct((rows, cw), BF16),
        jax.ShapeDtypeStruct((npast, cw), F32))
    return pl.pallas_call(
        functools.partial(_proj_sample_kernel, rows=rows, stride=stride),
        grid=(1,),
        in_specs=[full((rows, d)), full((1, d)), full(win.shape)] + [full((rows, LANES))] * 3
        + [full(convw.shape), full((1, cw)), full((npast, cw))],
        out_specs=[full((rows, ATT_WIDTH))] * 3 + [full((rows, cw)), full((npast, cw))],
        out_shape=out_shape,
        scratch_shapes=[pltpu.VMEM((rows + npast, cw), F32)],
        compiler_params=_params("arbitrary"),
        name="proj_sample",
    )(x2d, gpre, win, *tables, convw, gconv, past2d)


def _prompt_attn(q, k, v, gatt):
    batch, _, seq, _ = q.shape
    tq = Q_BLOCK
    (_, near_d), (_, mid_d), (_, far_d) = DILATED_PATTERNS
    assert near_d == 1 and far_d % mid_d == 0 and mid_d % CLASSES_PER_STEP == 0
    for window, dil in DILATED_PATTERNS:
        sub = seq // dil
        assert seq % dil == 0 and sub % tq == 0 and window % dil == 0 and window // dil <= tq
        assert (sub // tq) & (sub // tq - 1) == 0
    sub_mid = seq // mid_d
    reaches = {window // dil for window, dil in DILATED_PATTERNS}
    assert len(reaches) == 1, "one mask table serves every pattern"
    dist = np.arange(tq)[None, :, None] - np.arange(2 * tq)[None, None, :] + tq * np.arange(2)[:, None, None]
    bias = jnp.asarray(np.where((dist >= 0) & (dist <= reaches.pop()), 0.0, MASK_VALUE), F32)
    col_head = np.arange(ATT_WIDTH) // HEAD_DIM
    col_group = np.array([_stat_lane_group(h // HEADS_PER_SLAB, h % HEADS_PER_SLAB) for h in col_head])
    expand = jnp.asarray(np.arange(LANES)[:, None] == col_group[None, :] * LSE_LANES_PER_HEAD, BF16)
    n_pat = len(DILATED_PATTERNS)
    blk = pl.BlockSpec((None, N_SLABS, seq, LANES), lambda b: (b, 0, 0, 0))
    return pl.pallas_call(
        functools.partial(_prompt_attn_kernel, seq=seq, tq=tq),
        grid=(batch,),
        in_specs=[blk, blk, blk, _const_spec(bias.shape), _const_spec(expand.shape),
                  _const_spec(gatt.shape)],
        out_specs=pl.BlockSpec((seq, ATT_WIDTH), lambda b: (b, 0)),
        out_shape=jax.ShapeDtypeStruct((batch * seq, ATT_WIDTH), BF16),
        scratch_shapes=[pltpu.VMEM((n_pat, N_SLABS, seq, LANES), F32),
                        pltpu.VMEM((n_pat, 2, seq, LANES), F32),
                        pltpu.VMEM((CLASSES_PER_STEP, 3, N_SLABS, sub_mid, LANES), F32),
                        pltpu.VMEM((CLASSES_PER_STEP, N_SLABS, sub_mid, LANES), F32),
                        pltpu.VMEM((CLASSES_PER_STEP, 2, sub_mid, LANES), F32)],
        compiler_params=_params("arbitrary"),
        name="prompt_attn",
    )(q, k, v, bias, expand, gatt)


def _sample_bias(n_t, past_len, span, new_pad):
    big = 1 << 30
    rel = np.concatenate([np.arange(span) - span, np.arange(n_t),
                          np.full((new_pad - n_t,), -big)])
    rows = []
    for window, dil in DILATED_PATTERNS:
        for t in range(n_t):
            dist = t - rel
            ok = (dist >= 0) & (dist <= window) & (dist % dil == 0) & (past_len + rel >= 0)
            rows += [np.where(ok, 0.0, MASK_VALUE)] * ATT_HEADS
    return jnp.asarray(np.stack(rows), F32)


def _post(x2d, an, cn, wout, npost, npre, wup, wdown, nmlp, sample=None):
    n, d = x2d.shape
    tm = min(ROW_BLOCK, n)
    assert n % tm == 0
    steps = n // tm
    row = lambda w: pl.BlockSpec((tm, w), lambda i: (i, 0))
    args = [x2d, an, cn, wout, npost, npre, wup, wdown, nmlp]
    in_specs = [row(d), row(an.shape[1]), row(cn.shape[1]), _const_spec(wout.shape),
                _const_spec((1, d)), _const_spec((1, d)), _const_spec(wup.shape),
                _const_spec(wdown.shape), _const_spec((1, d))]
    out_specs = [row(d)]
    out_shape = [jax.ShapeDtypeStruct((n, d), F32)]
    n_t = 0
    if sample:
        pw, svn, cache_v, gatt = sample
        n_t, dec_batch = svn.shape[0], svn.shape[1]
        span = cache_v.shape[1]
        new_blk, buf_blk, pw_blk, new_pad, n_score = _sample_blocks(n_t, dec_batch, span, steps)
        assert pw.shape == (dec_batch, n_score, span + new_pad)
        svt = jnp.transpose(cache_v, (0, 2, 3, 1))
        args += [pw, svn, svt, gatt]
        in_specs += [pw_blk, new_blk, buf_blk, _const_spec(gatt.shape)]
        out_specs.append(new_blk)
        out_shape.append(jax.ShapeDtypeStruct((n_t, dec_batch, 1, ATT_WIDTH), F32))
    outs = pl.pallas_call(
        functools.partial(_post_kernel, ff_chunk=min(FF_CHUNK, wup.shape[1]), sample_n_t=n_t),
        grid=(steps,),
        in_specs=in_specs,
        out_specs=out_specs,
        out_shape=out_shape,
        compiler_params=_params("arbitrary"),
        name="post",
    )(*args)
    return outs if sample else outs[0]


def kernel(x_prompt, x_sample, cache_k, cache_v, state_conv, n_att_pre, n_att_post, w_in, conv_w,
           g_att, g_conv, w_out, n_mlp_pre, n_mlp_post, w_up, w_down):
    depth = w_in.shape[0]
    batch, seq, d = x_prompt.shape
    dec_batch, dec_seq, _ = x_sample.shape
    past_len = PAST_LEN
    keep = min(MAX_SPAN, seq)
    assert keep == seq, "the prompt's window buffer is its whole key/value sequence"
    cw = conv_w.shape[2]

    yp = x_prompt.reshape(batch * seq, d)
    ys = x_sample.swapaxes(0, 1).reshape(dec_seq * dec_batch, d)
    outs = [[] for _ in range(6)]
    for l in range(depth):
        win, wout = w_in[l].astype(BF16), w_out[l].astype(BF16)
        wup, wdown = w_up[l].astype(BF16), w_down[l].astype(BF16)
        gpre, npost = n_att_pre[l][None, :], n_att_post[l][None, :]
        npre, nmlp = n_mlp_pre[l][None, :], n_mlp_post[l][None, :]
        gatt, gconv = g_att[l][None, :], g_conv[l][None, :]

        past = state_conv[l].swapaxes(0, 1).reshape((CONV_K - 1) * dec_batch, cw)
        qs, ks, vs, cns, css = _proj_sample(ys, past_len + jnp.arange(dec_seq), dec_batch, past,
                                            gpre, win, conv_w[l], gconv)
        tm4 = lambda a: a.reshape(dec_seq, dec_batch, 1, ATT_WIDTH)
        q, k, v, kt, vt, cn, cs, pw = _proj_prompt(yp, seq, gpre, win, conv_w[l], gconv,
                                                   tm4(qs), tm4(ks), cache_k[l], past_len)
        an = _prompt_attn(q, k, v, gatt)
        yp, ans = _post(yp, an, cn, wout, npost, npre, wup, wdown, nmlp,
                        sample=(pw, tm4(vs), cache_v[l], gatt))
        per_head = lambda a: a.reshape(batch, ATT_HEADS, HEAD_DIM, seq).transpose(0, 3, 1, 2)
        outs[0].append(per_head(kt))
        outs[1].append(per_head(vt))
        outs[2].append(cs)
        ys = _post(ys, ans.reshape(dec_seq * dec_batch, ATT_WIDTH), cns, wout, npost, npre,
                   wup, wdown, nmlp)
        bm = lambda a: a.reshape(dec_seq, dec_batch, ATT_HEADS, HEAD_DIM).swapaxes(0, 1)
        outs[3].append(bm(ks))
        outs[4].append(bm(vs))
        outs[5].append(css.reshape(CONV_K - 1, dec_batch, cw).swapaxes(0, 1))

    y_prompt = yp.reshape(batch, seq, d)
    y_sample = ys.reshape(dec_seq, dec_batch, d).swapaxes(0, 1)
    return (y_prompt, y_sample) + tuple(jnp.stack(o) for o in outs)
```

```python
import functools

import numpy as np
import jax
import jax.numpy as jnp
from jax import lax
from jax.experimental import pallas as pl
from jax.experimental.pallas import tpu as pltpu

HEAD_DIM = 64
ATT_HEADS = 8
ATT_WIDTH = ATT_HEADS * HEAD_DIM
CONV_K = 3
DILATED_PATTERNS = ((128, 1), (512, 4), (2048, 16))
MAX_SPAN = max(w for w, _ in DILATED_PATTERNS)
PAST_LEN = 8192
ROPE_THETA = 10000.0
NORM_EPS = 1e-6

LANES = 128
HEADS_PER_SLAB = LANES // HEAD_DIM
N_SLABS = ATT_WIDTH // LANES
LSE_LANES_PER_HEAD = LANES // ATT_HEADS
MASK_VALUE = -1e30
LOG2_E = 1.4426950408889634
VMEM_LIMIT_BYTES = 60 * 1024 * 1024

ROW_BLOCK = 512
Q_BLOCK = 128
UNITS_PER_STEP = 16
CLASSES_PER_STEP = 2
FF_CHUNK = 1024
ROW_GROUPS = 2

BF16 = jnp.bfloat16
F32 = jnp.float32
NT_DIMS = (((1,), (1,)), ((), ()))


def _rmsnorm(x, g):
    return x * lax.rsqrt(jnp.mean(x * x, axis=-1, keepdims=True) + NORM_EPS) * g


def _rope(x, cos, sin_lo, sin_hi):
    half = HEAD_DIM // 2
    outs = []
    for c in range(N_SLABS):
        xs = x[:, c * LANES:(c + 1) * LANES]
        outs.append(xs * cos + pltpu.roll(xs, LANES - half, 1) * sin_lo
                    + pltpu.roll(xs, half, 1) * sin_hi)
    return jnp.concatenate(outs, axis=1)


def _in_projection(x, gpre, win_ref):
    h = _rmsnorm(x, gpre).astype(BF16)
    return [jnp.dot(h, win_ref[:, c * ATT_WIDTH:(c + 1) * ATT_WIDTH], preferred_element_type=F32)
            for c in range(win_ref.shape[1] // ATT_WIDTH)]


def _split_projection(cols, cos, sin_lo, sin_hi, q_scale):
    q = _rope(cols[0], cos, sin_lo, sin_hi) * q_scale
    k = _rope(cols[1], cos, sin_lo, sin_hi)
    return q, k, cols[2], cols[3], cols[4] * cols[5]


def _proj_prompt_kernel(x_ref, gpre_ref, win_ref, cos_ref, slo_ref, shi_ref, convw_ref,
                        gconv_ref, sq_ref, skn_ref, skt_ref, sbias_ref,
                        q_ref, k_ref, v_ref, kt_ref, vt_ref, cn_ref, cs_ref, spw_ref,
                        hist_ref, *, tm, blocks_per_seq, n_t, new_pad):
    _sample_scores_kernel(sq_ref, skn_ref, skt_ref, sbias_ref, spw_ref, n_t=n_t, new_pad=new_pad)
    j = pl.program_id(0) % blocks_per_seq
    rows = tm // ROW_GROUPS
    groups = [slice(g * rows, (g + 1) * rows) for g in range(ROW_GROUPS)]
    raw = [_in_projection(x_ref[r, :], gpre_ref[...], win_ref) for r in groups]

    hist_ref[0:8, :] = jnp.where(j == 0, 0.0, hist_ref[tm:tm + 8, :])
    for g, r in enumerate(groups):
        t0 = pl.multiple_of(j * tm + g * rows, rows)
        q, k, v, gate_b, gated = _split_projection(
            raw[g], cos_ref[pl.ds(t0, rows), :], slo_ref[pl.ds(t0, rows), :],
            shi_ref[pl.ds(t0, rows), :], HEAD_DIM ** -0.5 * LOG2_E)
        for c in range(N_SLABS):
            q_ref[c, r, :] = q[:, c * LANES:(c + 1) * LANES]
            k_ref[c, r, :] = k[:, c * LANES:(c + 1) * LANES]
            v_ref[c, r, :] = v[:, c * LANES:(c + 1) * LANES]
        kt_ref[:, r] = k.T
        vt_ref[:, r] = v.T
        lo = 8 + g * rows
        hist_ref[lo:lo + rows, :] = gated
        conv = (convw_ref[0:1, :] * hist_ref[lo - 2:lo - 2 + rows, :]
                + convw_ref[1:2, :] * hist_ref[lo - 1:lo - 1 + rows, :] + convw_ref[2:3, :] * gated)
        cn_ref[r, :] = _rmsnorm(gate_b * conv, gconv_ref[...]).astype(BF16)
    cs_ref[...] = gated[rows - (CONV_K - 1):rows, :]


def _proj_sample_kernel(x_ref, gpre_ref, win_ref, cos_ref, slo_ref, shi_ref, convw_ref,
                        gconv_ref, past_ref, q_ref, k_ref, v_ref, cn_ref, cs_ref, hist_ref,
                        *, rows, stride):
    q, k, v, gate_b, gated = _split_projection(
        _in_projection(x_ref[...], gpre_ref[...], win_ref), cos_ref[...], slo_ref[...],
        shi_ref[...], HEAD_DIM ** -0.5)
    q_ref[...] = q
    k_ref[...] = k
    v_ref[...] = v
    npast = (CONV_K - 1) * stride
    hist_ref[0:npast, :] = past_ref[...]
    hist_ref[npast:npast + rows, :] = gated
    conv = (convw_ref[0:1, :] * hist_ref[0:rows, :]
            + convw_ref[1:2, :] * hist_ref[stride:stride + rows, :]
            + convw_ref[2:3, :] * gated)
    cn_ref[...] = _rmsnorm(gate_b * conv, gconv_ref[...]).astype(BF16)
    cs_ref[...] = hist_ref[rows:rows + npast, :]


def _stat_lane_group(c, hh):
    return (HEADS_PER_SLAB - 1 - hh) * (HEAD_DIM // LSE_LANES_PER_HEAD) + c


def _attend(load, store_o, store_stats, bias_ref, i, start, kv, tq):
    lane = lax.broadcasted_iota(jnp.int32, (tq, LANES), 1)
    first_head = lane < HEAD_DIM
    group = lane // LSE_LANES_PER_HEAD
    bias = bias_ref[0, :, :kv] if isinstance(start, int) else bias_ref[jnp.minimum(i, 1)]
    max_tile = jnp.zeros((tq, LANES), F32)
    den_tile = jnp.zeros((tq, LANES), F32)
    key_lane = lax.broadcasted_iota(jnp.int32, (kv, LANES), 1)
    key_first = key_lane < HEAD_DIM
    for c in range(N_SLABS):
        q2 = load(0, c, i * tq, tq).astype(BF16)
        k2 = load(1, c, start, kv).astype(BF16)
        v2 = load(2, c, start, kv).astype(BF16)
        if kv == tq:
            zero = jnp.zeros_like(k2)
            k_both = jnp.concatenate([jnp.where(key_first, k2, zero), jnp.where(key_first, zero, k2)], 0)
            v_both = jnp.concatenate([jnp.where(key_first, v2, zero), jnp.where(key_first, zero, v2)], 0)
            s_both = lax.dot_general(q2, k_both, NT_DIMS, preferred_element_type=F32)
            probs = []
            for hh in range(HEADS_PER_SLAB):
                s = s_both[:, hh * kv:(hh + 1) * kv] + bias
                m = jnp.max(s, axis=1, keepdims=True)
                p = jnp.exp2(s - m).astype(BF16)
                den = jnp.sum(p.astype(F32), axis=1, keepdims=True)
                probs.append(p)
                mine = group == _stat_lane_group(c, hh)
                max_tile = jnp.where(mine, m, max_tile)
                den_tile = jnp.where(mine, den, den_tile)
            store_o(c, jnp.dot(jnp.concatenate(probs, axis=1), v_both, preferred_element_type=F32))
        else:
            one = jnp.ones_like(v2)
            outs = []
            for hh in range(HEADS_PER_SLAB):
                own = first_head if hh == 0 else ~first_head
                qm = jnp.where(own, q2, jnp.zeros_like(q2))
                s = lax.dot_general(qm, k2, NT_DIMS, preferred_element_type=F32) + bias
                m = jnp.max(s, axis=1, keepdims=True)
                p = jnp.exp2(s - m).astype(BF16)
                v_aug = jnp.where(key_first if hh == 0 else ~key_first, v2, one)
                r = jnp.dot(p, v_aug, preferred_element_type=F32)
                outs.append(r)
                mine = group == _stat_lane_group(c, hh)
                max_tile = jnp.where(mine, m, max_tile)
                den_tile = jnp.where(mine, r, den_tile)
            store_o(c, jnp.where(first_head, outs[0], outs[1]))
    store_stats(max_tile, den_tile)


def _prompt_attn_kernel(q_ref, k_ref, v_ref, bias_ref, expand_ref, gatt_ref, an_ref, o_scr, l_scr,
                        cls_scr, oc_scr, lc_scr, *, seq, tq):
    srcs = (q_ref, k_ref, v_ref)
    (_, near_d), (_, mid_d), (_, far_d) = DILATED_PATTERNS
    inner = far_d // mid_d
    sub_mid, sub_far = seq // mid_d, seq // far_d
    nq_far = sub_far // tq

    def aligned(first, count):
        if isinstance(first, int):
            return pl.ds(first, count)
        return pl.ds(pl.multiple_of(first, tq), count)

    def near_unit(i, carry):
        start = jnp.maximum(i - 1, 0) * tq
        rows = aligned(i * tq, tq)

        def store_o(c, val):
            o_scr[0, c, rows, :] = val

        def store_stats(top, den):
            l_scr[0, 0, rows, :] = top
            l_scr[0, 1, rows, :] = den

        _attend(lambda a, c, first, count: srcs[a][c, aligned(first, count), :],
                store_o, store_stats, bias_ref, i, start, min(2 * tq, seq), tq)
        return carry

    lax.fori_loop(0, seq // tq, near_unit, 0, unroll=UNITS_PER_STEP)

    def mid_class(r, slot):
        for a in range(3):
            for c in range(N_SLABS):
                cls_scr[slot, a, c] = srcs[a][c, pl.ds(r, sub_mid, stride=mid_d), :]

        def mid_unit(i, carry):
            start = jnp.maximum(i - 1, 0) * tq if sub_mid > tq else 0
            rows = pl.ds(r + mid_d * (i * tq), tq, stride=mid_d)

            def store_o(c, val):
                o_scr[1, c, rows, :] = val

            def store_stats(top, den):
                l_scr[1, 0, rows, :] = top
                l_scr[1, 1, rows, :] = den

            _attend(lambda a, c, first, count: cls_scr[slot, a, c, aligned(first, count), :],
                    store_o, store_stats, bias_ref, i, start, min(2 * tq, sub_mid), tq)
            return carry

        def far_unit(n, carry):
            m = n >> (nq_far.bit_length() - 1)
            i = n & (nq_far - 1)
            start = jnp.maximum(i - 1, 0) * tq if sub_far > tq else 0
            rows = pl.ds(m + inner * (i * tq), tq, stride=inner)

            def store_o(c, val):
                oc_scr[slot, c, rows, :] = val

            def store_stats(top, den):
                lc_scr[slot, 0, rows, :] = top
                lc_scr[slot, 1, rows, :] = den

            _attend(lambda a, c, first, count:
                    cls_scr[slot, a, c, pl.ds(m + inner * first, count, stride=inner), :],
                    store_o, store_stats, bias_ref, i, start, min(2 * tq, sub_far), tq)
            return carry

        lax.fori_loop(0, sub_mid // tq, lambda n, carry: far_unit(n, mid_unit(n, carry)), 0,
                      unroll=UNITS_PER_STEP)
        back = pl.ds(r, sub_mid, stride=mid_d)
        for c in range(N_SLABS):
            o_scr[2, c, back, :] = oc_scr[slot, c]
        for j in range(2):
            l_scr[2, j, back, :] = lc_scr[slot, j]

    def class_step(n, carry):
        for slot in range(CLASSES_PER_STEP):
            mid_class(n * CLASSES_PER_STEP + slot, slot)
        return carry

    lax.fori_loop(0, mid_d // CLASSES_PER_STEP, class_step, 0)

    n_pat = len(DILATED_PATTERNS)

    def merge(b, carry):
        r = pl.ds(pl.multiple_of(b * tq, tq), tq)
        tops = [l_scr[ip, 0, r, :] for ip in range(n_pat)]
        top = functools.reduce(jnp.maximum, tops)
        es = [jnp.exp2(t - top) for t in tops]
        inv = 1.0 / functools.reduce(lambda x, y: x + y,
                                     [e * l_scr[ip, 1, r, :] for ip, e in enumerate(es)])
        slabs = [None] * N_SLABS
        for ip, e in enumerate(es):
            wide = jnp.dot((e * inv).astype(BF16), expand_ref[...], preferred_element_type=F32)
            for c in range(N_SLABS):
                term = wide[:, c * LANES:(c + 1) * LANES] * o_scr[ip, c, r, :]
                slabs[c] = term if slabs[c] is None else slabs[c] + term
        attn = jnp.concatenate(slabs, axis=1)
        an_ref[r, :] = _rmsnorm(attn, gatt_ref[...]).astype(BF16)
        return carry

    lax.fori_loop(0, seq // tq, merge, 0, unroll=UNITS_PER_STEP)


def _sample_weights(qf, kn_rows, kt3, bias, n_t, new_pad):
    width = ATT_WIDTH
    n_pat = len(DILATED_PATTERNS)
    per = n_t * ATT_HEADS
    sub = lax.broadcasted_iota(jnp.int32, (ATT_HEADS, width), 0)
    lane = lax.broadcasted_iota(jnp.int32, (ATT_HEADS, width), 1)
    own_head = sub == lane // HEAD_DIM
    span = kt3.shape[-1]
    pad = jnp.zeros((new_pad - n_t, width), F32)
    q_tiles = [jnp.where(own_head, jnp.broadcast_to(qf[t:t + 1, :], (ATT_HEADS, width)), 0.0)
               for t in range(n_t)]
    qbd = jnp.concatenate(q_tiles * n_pat, axis=0).astype(BF16)
    kt = kt3.reshape(width, span).astype(BF16)
    kn = jnp.concatenate([kn_rows, pad], axis=0).astype(BF16)
    s = jnp.concatenate(
        [jnp.dot(qbd, kt, preferred_element_type=F32),
         lax.dot_general(qbd, kn, NT_DIMS, preferred_element_type=F32)], axis=1) + bias
    m = jnp.max(s, axis=1, keepdims=True)
    p = jnp.exp(s - m)
    den = jnp.sum(p, axis=1, keepdims=True)
    lse = m + jnp.log(den)
    lses = [lse[i * per:(i + 1) * per, :] for i in range(n_pat)]
    top = functools.reduce(jnp.maximum, lses)
    es = [jnp.exp(l - top) for l in lses]
    z = functools.reduce(lambda a, b: a + b, es)
    scale = jnp.concatenate([e / z for e in es], axis=0) / den
    return (p * scale).astype(BF16)


def _sample_output(pw, vn_rows, vt3, gatt, n_t):
    width = ATT_WIDTH
    n_pat = len(DILATED_PATTERNS)
    per = n_t * ATT_HEADS
    sub = lax.broadcasted_iota(jnp.int32, (ATT_HEADS, width), 0)
    lane = lax.broadcasted_iota(jnp.int32, (ATT_HEADS, width), 1)
    own_head = sub == lane // HEAD_DIM
    span = vt3.shape[-1]
    pad = jnp.zeros((pw.shape[-1] - span - n_t, width), F32)
    vt = vt3.reshape(width, span).astype(BF16)
    vn = jnp.concatenate([vn_rows, pad], axis=0).astype(BF16)
    o = (lax.dot_general(pw[:, :span], vt, NT_DIMS, preferred_element_type=F32)
         + jnp.dot(pw[:, span:], vn, preferred_element_type=F32))
    rows = []
    for t in range(n_t):
        acc = o[t * ATT_HEADS:(t + 1) * ATT_HEADS, :]
        for i in range(1, n_pat):
            acc = acc + o[i * per + t * ATT_HEADS:i * per + (t + 1) * ATT_HEADS, :]
        attn = jnp.sum(jnp.where(own_head, acc, 0.0), axis=0, keepdims=True)
        rows.append(_rmsnorm(attn, gatt))
    return rows


def _sample_scores_kernel(q_ref, kn_ref, kt_ref, bias_ref, pw_ref, *, n_t, new_pad):
    for j in range(kt_ref.shape[0]):
        pw_ref[j] = _sample_weights(q_ref[:, j, 0, :], kn_ref[:, j, 0, :], kt_ref[j], bias_ref[...],
                                    n_t, new_pad)


def _sample_values_kernel(pw_ref, vn_ref, vt_ref, gatt_ref, q_ref, kn_ref, kt_ref, bias_ref, o_ref,
                          *, n_t, new_pad):
    ready = pw_ref.shape[0]
    for j in range(vt_ref.shape[0]):
        if j < ready:
            pw = pw_ref[j]
        else:
            pw = _sample_weights(q_ref[:, j - ready, 0, :], kn_ref[:, j - ready, 0, :],
                                 kt_ref[j - ready], bias_ref[...], n_t, new_pad)
        rows = _sample_output(pw, vn_ref[:, j, 0, :], vt_ref[j], gatt_ref[...], n_t)
        for t in range(n_t):
            o_ref[t, j, :, :] = rows[t]


def _post_kernel(*refs, ff_chunk, sample_n_t, sample_new_pad):
    (x_ref, an_ref, cn_ref, wout_ref, npost_ref, npre_ref, wup_ref, wdown_ref, nmlp_ref) = refs[:9]
    if sample_n_t:
        _sample_values_kernel(*refs[9:17], refs[18], n_t=sample_n_t, new_pad=sample_new_pad)
    y_ref = refs[17] if sample_n_t else refs[9]
    aw = an_ref.shape[1]
    d_ff = wup_ref.shape[1]
    rows = x_ref.shape[0] // ROW_GROUPS
    groups = [slice(g * rows, (g + 1) * rows) for g in range(ROW_GROUPS)]
    mixed = [jnp.dot(an_ref[r, :].astype(BF16), wout_ref[0:aw, :], preferred_element_type=F32)
             + jnp.dot(cn_ref[r, :], wout_ref[aw:, :], preferred_element_type=F32) for r in groups]
    for r, mix in zip(groups, mixed):
        x1 = x_ref[r, :] + _rmsnorm(mix, npost_ref[...])
        h = _rmsnorm(x1, npre_ref[...]).astype(BF16)
        f = None
        for c in range(d_ff // ff_chunk):
            u = jnp.dot(h, wup_ref[:, c * ff_chunk:(c + 1) * ff_chunk], preferred_element_type=F32)
            a = jnp.square(jnp.maximum(u, 0.0)).astype(BF16)
            part = jnp.dot(a, wdown_ref[c * ff_chunk:(c + 1) * ff_chunk, :],
                           preferred_element_type=F32)
            f = part if f is None else f + part
        y_ref[r, :] = x1 + _rmsnorm(f, nmlp_ref[...])


def _const_spec(shape):
    nd = len(shape)
    return pl.BlockSpec(shape, lambda *_: (0,) * nd, pipeline_mode=pl.Buffered(1))


def _rope_tables(positions):
    half = HEAD_DIM // 2
    inv = ROPE_THETA ** (-jnp.arange(half, dtype=F32) * 2.0 / HEAD_DIM)
    ang = positions.astype(F32)[:, None] * inv[None, :]
    cos, sin, zero = jnp.cos(ang), jnp.sin(ang), jnp.zeros_like(ang)
    reps = LANES // HEAD_DIM
    cos_t = jnp.tile(jnp.concatenate([cos, cos], axis=1), (1, reps))
    sin_lo = jnp.tile(jnp.concatenate([-sin, zero], axis=1), (1, reps))
    sin_hi = jnp.tile(jnp.concatenate([zero, sin], axis=1), (1, reps))
    return cos_t, sin_lo, sin_hi


def _params(*semantics):
    return pltpu.CompilerParams(dimension_semantics=semantics, vmem_limit_bytes=VMEM_LIMIT_BYTES)


def _sample_blocks(n_t, dec_batch, span, steps):
    assert dec_batch % (2 * steps) == 0, "every grid step takes the same, even number of sequences"
    bb = dec_batch // steps
    half = bb // 2
    new_pad = -(-n_t // LANES) * LANES
    n_score = len(DILATED_PATTERNS) * n_t * ATT_HEADS

    def rows(count, index):
        return pl.BlockSpec((n_t, count, 1, ATT_WIDTH), lambda i: (0, index(i), 0, 0))

    def buffers(count, index):
        return pl.BlockSpec((count, ATT_HEADS, HEAD_DIM, span), lambda i: (index(i), 0, 0, 0))

    first, second = (lambda i: 2 * i), (lambda i: 2 * i + 1)
    return dict(
        rows_all=rows(bb, lambda i: i), bufs_all=buffers(bb, lambda i: i),
        rows_first=rows(half, first), bufs_first=buffers(half, first),
        rows_second=rows(half, second), bufs_second=buffers(half, second),
        weights=pl.BlockSpec((half, n_score, span + new_pad), lambda i: (i, 0, 0)),
        weights_shape=(steps * half, n_score, span + new_pad), new_pad=new_pad)


def _proj_prompt(x2d, seq, gpre, win, convw, gconv, sq, skn, cache_k, past_len):
    n, d = x2d.shape
    tm = min(ROW_BLOCK, seq)
    assert seq % tm == 0 and tm % 8 == 0
    bps = seq // tm
    batch = n // seq
    steps = n // tm
    cw = convw.shape[1]
    tables = _rope_tables(jnp.arange(seq))
    row = lambda w: pl.BlockSpec((tm, w), lambda i: (i, 0))
    slabbed = pl.BlockSpec((None, N_SLABS, tm, LANES), lambda i: (i // bps, 0, i % bps, 0))
    transposed = pl.BlockSpec((None, ATT_WIDTH, tm), lambda i: (i // bps, 0, i % bps))
    slab_shape = jax.ShapeDtypeStruct((batch, N_SLABS, seq, LANES), F32)
    t_shape = jax.ShapeDtypeStruct((batch, ATT_WIDTH, seq), F32)

    n_t, dec_batch = sq.shape[0], sq.shape[1]
    span = cache_k.shape[1]
    assert span == min(MAX_SPAN, past_len)
    blocks = _sample_blocks(n_t, dec_batch, span, steps)
    new_pad = blocks["new_pad"]
    bias = _sample_bias(n_t, past_len, span, new_pad)
    skt = jnp.transpose(cache_k, (0, 2, 3, 1))

    out_shape = (slab_shape, slab_shape, slab_shape, t_shape, t_shape,
                 jax.ShapeDtypeStruct((n, cw), BF16),
                 jax.ShapeDtypeStruct((batch, CONV_K - 1, cw), F32),
                 jax.ShapeDtypeStruct(blocks["weights_shape"], BF16))
    return pl.pallas_call(
        functools.partial(_proj_prompt_kernel, tm=tm, blocks_per_seq=bps, n_t=n_t, new_pad=new_pad),
        grid=(steps,),
        in_specs=[row(d), _const_spec((1, d)), _const_spec(win.shape)]
        + [_const_spec((seq, LANES))] * 3 + [_const_spec(convw.shape), _const_spec((1, cw))]
        + [blocks["rows_first"], blocks["rows_first"], blocks["bufs_first"],
           _const_spec(bias.shape)],
        out_specs=[slabbed, slabbed, slabbed, transposed, transposed, row(cw),
                   pl.BlockSpec((None, CONV_K - 1, cw), lambda i: (i // bps, 0, 0)),
                   blocks["weights"]],
        out_shape=out_shape,
        scratch_shapes=[pltpu.VMEM((tm + 8, cw), F32)],
        compiler_params=_params("arbitrary"),
        name="proj_prompt",
    )(x2d, gpre, win, *tables, convw, gconv, sq, skn, skt, bias)


def _proj_sample(x2d, positions, stride, past2d, gpre, win, convw, gconv):
    rows, d = x2d.shape
    cw = convw.shape[1]
    tables = [jnp.repeat(t, stride, axis=0) for t in _rope_tables(positions)]
    npast = (CONV_K - 1) * stride
    full = lambda shape: pl.BlockSpec(shape, lambda i: (0,) * len(shape))
    out_shape = (
        jax.ShapeDtypeStruct((rows, ATT_WIDTH), F32), jax.ShapeDtypeStruct((rows, ATT_WIDTH), F32),
        jax.ShapeDtypeStruct((rows, ATT_WIDTH), F32), jax.ShapeDtypeStruct((rows, cw), BF16),
        jax.ShapeDtypeStruct((npast, cw), F32))
    return pl.pallas_call(
        functools.partial(_proj_sample_kernel, rows=rows, stride=stride),
        grid=(1,),
        in_specs=[full((rows, d)), full((1, d)), full(win.shape)] + [full((rows, LANES))] * 3
        + [full(convw.shape), full((1, cw)), full((npast, cw))],
        out_specs=[full((rows, ATT_WIDTH))] * 3 + [full((rows, cw)), full((npast, cw))],
        out_shape=out_shape,
        scratch_shapes=[pltpu.VMEM((rows + npast, cw), F32)],
        compiler_params=_params("arbitrary"),
        name="proj_sample",
    )(x2d, gpre, win, *tables, convw, gconv, past2d)


def _prompt_attn(q, k, v, gatt):
    batch, _, seq, _ = q.shape
    tq = Q_BLOCK
    (_, near_d), (_, mid_d), (_, far_d) = DILATED_PATTERNS
    assert near_d == 1 and far_d % mid_d == 0 and mid_d % CLASSES_PER_STEP == 0
    for window, dil in DILATED_PATTERNS:
        sub = seq // dil
        assert seq % dil == 0 and sub % tq == 0 and window % dil == 0 and window // dil <= tq
        assert (sub // tq) & (sub // tq - 1) == 0
    sub_mid = seq // mid_d
    reaches = {window // dil for window, dil in DILATED_PATTERNS}
    assert len(reaches) == 1, "one mask table serves every pattern"
    dist = np.arange(tq)[None, :, None] - np.arange(2 * tq)[None, None, :] + tq * np.arange(2)[:, None, None]
    bias = jnp.asarray(np.where((dist >= 0) & (dist <= reaches.pop()), 0.0, MASK_VALUE), F32)
    col_head = np.arange(ATT_WIDTH) // HEAD_DIM
    col_group = np.array([_stat_lane_group(h // HEADS_PER_SLAB, h % HEADS_PER_SLAB) for h in col_head])
    expand = jnp.asarray(np.arange(LANES)[:, None] == col_group[None, :] * LSE_LANES_PER_HEAD, BF16)
    n_pat = len(DILATED_PATTERNS)
    blk = pl.BlockSpec((None, N_SLABS, seq, LANES), lambda b: (b, 0, 0, 0))
    return pl.pallas_call(
        functools.partial(_prompt_attn_kernel, seq=seq, tq=tq),
        grid=(batch,),
        in_specs=[blk, blk, blk, _const_spec(bias.shape), _const_spec(expand.shape),
                  _const_spec(gatt.shape)],
        out_specs=pl.BlockSpec((seq, ATT_WIDTH), lambda b: (b, 0)),
        out_shape=jax.ShapeDtypeStruct((batch * seq, ATT_WIDTH), BF16),
        scratch_shapes=[pltpu.VMEM((n_pat, N_SLABS, seq, LANES), F32),
                        pltpu.VMEM((n_pat, 2, seq, LANES), F32),
                        pltpu.VMEM((CLASSES_PER_STEP, 3, N_SLABS, sub_mid, LANES), F32),
                        pltpu.VMEM((CLASSES_PER_STEP, N_SLABS, sub_mid, LANES), F32),
                        pltpu.VMEM((CLASSES_PER_STEP, 2, sub_mid, LANES), F32)],
        compiler_params=_params("arbitrary"),
        name="prompt_attn",
    )(q, k, v, bias, expand, gatt)


def _sample_bias(n_t, past_len, span, new_pad):
    big = 1 << 30
    rel = np.concatenate([np.arange(span) - span, np.arange(n_t),
                          np.full((new_pad - n_t,), -big)])
    rows = []
    for window, dil in DILATED_PATTERNS:
        for t in range(n_t):
            dist = t - rel
            ok = (dist >= 0) & (dist <= window) & (dist % dil == 0) & (past_len + rel >= 0)
            rows += [np.where(ok, 0.0, MASK_VALUE)] * ATT_HEADS
    return jnp.asarray(np.stack(rows), F32)


def _post(x2d, an, cn, wout, npost, npre, wup, wdown, nmlp, sample=None):
    n, d = x2d.shape
    tm = min(ROW_BLOCK, n)
    assert n % tm == 0
    steps = n // tm
    row = lambda w: pl.BlockSpec((tm, w), lambda i: (i, 0))
    args = [x2d, an, cn, wout, npost, npre, wup, wdown, nmlp]
    in_specs = [row(d), row(an.shape[1]), row(cn.shape[1]), _const_spec(wout.shape),
                _const_spec((1, d)), _const_spec((1, d)), _const_spec(wup.shape),
                _const_spec(wdown.shape), _const_spec((1, d))]
    out_specs = [row(d)]
    out_shape = [jax.ShapeDtypeStruct((n, d), F32)]
    n_t = new_pad = 0
    if sample:
        pw, sq, skn, svn, cache_k, cache_v, past_len, gatt = sample
        n_t, dec_batch = svn.shape[0], svn.shape[1]
        span = cache_v.shape[1]
        blocks = _sample_blocks(n_t, dec_batch, span, steps)
        new_pad = blocks["new_pad"]
        assert pw.shape == blocks["weights_shape"]
        bias = _sample_bias(n_t, past_len, span, new_pad)
        skt = jnp.transpose(cache_k, (0, 2, 3, 1))
        svt = jnp.transpose(cache_v, (0, 2, 3, 1))
        args += [pw, svn, svt, gatt, sq, skn, skt, bias]
        in_specs += [blocks["weights"], blocks["rows_all"], blocks["bufs_all"],
                     _const_spec(gatt.shape), blocks["rows_second"], blocks["rows_second"],
                     blocks["bufs_second"], _const_spec(bias.shape)]
        out_specs.append(blocks["rows_all"])
        out_shape.append(jax.ShapeDtypeStruct((n_t, dec_batch, 1, ATT_WIDTH), F32))
    outs = pl.pallas_call(
        functools.partial(_post_kernel, ff_chunk=min(FF_CHUNK, wup.shape[1]), sample_n_t=n_t,
                          sample_new_pad=new_pad),
        grid=(steps,),
        in_specs=in_specs,
        out_specs=out_specs,
        out_shape=out_shape,
        compiler_params=_params("arbitrary"),
        name="post",
    )(*args)
    return outs if sample else outs[0]


def kernel(x_prompt, x_sample, cache_k, cache_v, state_conv, n_att_pre, n_att_post, w_in, conv_w,
           g_att, g_conv, w_out, n_mlp_pre, n_mlp_post, w_up, w_down):
    depth = w_in.shape[0]
    batch, seq, d = x_prompt.shape
    dec_batch, dec_seq, _ = x_sample.shape
    past_len = PAST_LEN
    keep = min(MAX_SPAN, seq)
    assert keep == seq, "the prompt's window buffer is its whole key/value sequence"
    cw = conv_w.shape[2]

    yp = x_prompt.reshape(batch * seq, d)
    ys = x_sample.swapaxes(0, 1).reshape(dec_seq * dec_batch, d)
    outs = [[] for _ in range(6)]
    for l in range(depth):
        win, wout = w_in[l].astype(BF16), w_out[l].astype(BF16)
        wup, wdown = w_up[l].astype(BF16), w_down[l].astype(BF16)
        gpre, npost = n_att_pre[l][None, :], n_att_post[l][None, :]
        npre, nmlp = n_mlp_pre[l][None, :], n_mlp_post[l][None, :]
        gatt, gconv = g_att[l][None, :], g_conv[l][None, :]

        past = state_conv[l].swapaxes(0, 1).reshape((CONV_K - 1) * dec_batch, cw)
        qs, ks, vs, cns, css = _proj_sample(ys, past_len + jnp.arange(dec_seq), dec_batch, past,
                                            gpre, win, conv_w[l], gconv)
        tm4 = lambda a: a.reshape(dec_seq, dec_batch, 1, ATT_WIDTH)
        q, k, v, kt, vt, cn, cs, pw = _proj_prompt(yp, seq, gpre, win, conv_w[l], gconv,
                                                   tm4(qs), tm4(ks), cache_k[l], past_len)
        an = _prompt_attn(q, k, v, gatt)
        yp, ans = _post(yp, an, cn, wout, npost, npre, wup, wdown, nmlp,
                        sample=(pw, tm4(qs), tm4(ks), tm4(vs), cache_k[l], cache_v[l], past_len,
                                gatt))
        per_head = lambda a: a.reshape(batch, ATT_HEADS, HEAD_DIM, seq).transpose(0, 3, 1, 2)
        outs[0].append(per_head(kt))
        outs[1].append(per_head(vt))
        outs[2].append(cs)
        ys = _post(ys, ans.reshape(dec_seq * dec_batch, ATT_WIDTH), cns, wout, npost, npre,
                   wup, wdown, nmlp)
        bm = lambda a: a.reshape(dec_seq, dec_batch, ATT_HEADS, HEAD_DIM).swapaxes(0, 1)
        outs[3].append(bm(ks))
        outs[4].append(bm(vs))
        outs[5].append(css.reshape(CONV_K - 1, dec_batch, cw).swapaxes(0, 1))

    y_prompt = yp.reshape(batch, seq, d)
    y_sample = ys.reshape(dec_seq, dec_batch, d).swapaxes(0, 1)
    return (y_prompt, y_sample) + tuple(jnp.stack(o) for o in outs)
```

```python
import functools

import numpy as np
import jax
import jax.numpy as jnp
from jax import lax
from jax.experimental import pallas as pl
from jax.experimental.pallas import tpu as pltpu

HEAD_DIM = 64
ATT_HEADS = 8
ATT_WIDTH = ATT_HEADS * HEAD_DIM
CONV_K = 3
DILATED_PATTERNS = ((128, 1), (512, 4), (2048, 16))
MAX_SPAN = max(w for w, _ in DILATED_PATTERNS)
PAST_LEN = 8192
ROPE_THETA = 10000.0
NORM_EPS = 1e-6

LANES = 128
HEADS_PER_SLAB = LANES // HEAD_DIM
N_SLABS = ATT_WIDTH // LANES
LSE_LANES_PER_HEAD = LANES // ATT_HEADS
MASK_VALUE = -1e30
LOG2_E = 1.4426950408889634
VMEM_LIMIT_BYTES = 60 * 1024 * 1024

ROW_BLOCK = 512
Q_BLOCK = 128
UNITS_PER_STEP = 16
CLASSES_PER_STEP = 2
FF_CHUNK = 1024
ROW_GROUPS = 2

BF16 = jnp.bfloat16
F32 = jnp.float32
NT_DIMS = (((1,), (1,)), ((), ()))


def _rmsnorm(x, g):
    return x * lax.rsqrt(jnp.mean(x * x, axis=-1, keepdims=True) + NORM_EPS) * g


def _rope(x, cos, sin_lo, sin_hi):
    half = HEAD_DIM // 2
    outs = []
    for c in range(N_SLABS):
        xs = x[:, c * LANES:(c + 1) * LANES]
        outs.append(xs * cos + pltpu.roll(xs, LANES - half, 1) * sin_lo
                    + pltpu.roll(xs, half, 1) * sin_hi)
    return jnp.concatenate(outs, axis=1)


def _in_projection(x, gpre, win_ref):
    h = _rmsnorm(x, gpre).astype(BF16)
    return [jnp.dot(h, win_ref[:, c * ATT_WIDTH:(c + 1) * ATT_WIDTH], preferred_element_type=F32)
            for c in range(win_ref.shape[1] // ATT_WIDTH)]


def _split_projection(cols, cos, sin_lo, sin_hi, q_scale):
    q = _rope(cols[0], cos, sin_lo, sin_hi) * q_scale
    k = _rope(cols[1], cos, sin_lo, sin_hi)
    return q, k, cols[2], cols[3], cols[4] * cols[5]


def _proj_prompt_kernel(x_ref, gpre_ref, win_ref, cos_ref, slo_ref, shi_ref, convw_ref,
                        gconv_ref, sq_ref, skn_ref, skt_ref, sbias_ref,
                        q_ref, k_ref, v_ref, kt_ref, vt_ref, cn_ref, cs_ref, spw_ref,
                        hist_ref, *, tm, blocks_per_seq, n_t, new_pad):
    _sample_scores_kernel(sq_ref, skn_ref, skt_ref, sbias_ref, spw_ref, n_t=n_t, new_pad=new_pad)
    j = pl.program_id(0) % blocks_per_seq
    rows = tm // ROW_GROUPS
    groups = [slice(g * rows, (g + 1) * rows) for g in range(ROW_GROUPS)]
    raw = [_in_projection(x_ref[r, :], gpre_ref[...], win_ref) for r in groups]

    hist_ref[0:8, :] = jnp.where(j == 0, 0.0, hist_ref[tm:tm + 8, :])
    for g, r in enumerate(groups):
        t0 = pl.multiple_of(j * tm + g * rows, rows)
        q, k, v, gate_b, gated = _split_projection(
            raw[g], cos_ref[pl.ds(t0, rows), :], slo_ref[pl.ds(t0, rows), :],
            shi_ref[pl.ds(t0, rows), :], HEAD_DIM ** -0.5 * LOG2_E)
        for c in range(N_SLABS):
            q_ref[c, r, :] = q[:, c * LANES:(c + 1) * LANES]
            k_ref[c, r, :] = k[:, c * LANES:(c + 1) * LANES]
            v_ref[c, r, :] = v[:, c * LANES:(c + 1) * LANES]
        kt_ref[:, r] = k.T
        vt_ref[:, r] = v.T
        lo = 8 + g * rows
        hist_ref[lo:lo + rows, :] = gated
        conv = (convw_ref[0:1, :] * hist_ref[lo - 2:lo - 2 + rows, :]
                + convw_ref[1:2, :] * hist_ref[lo - 1:lo - 1 + rows, :] + convw_ref[2:3, :] * gated)
        cn_ref[r, :] = _rmsnorm(gate_b * conv, gconv_ref[...]).astype(BF16)
    cs_ref[...] = gated[rows - (CONV_K - 1):rows, :]


def _proj_sample_kernel(x_ref, gpre_ref, win_ref, cos_ref, slo_ref, shi_ref, convw_ref,
                        gconv_ref, past_ref, q_ref, k_ref, v_ref, cn_ref, cs_ref, hist_ref,
                        *, rows, stride):
    q, k, v, gate_b, gated = _split_projection(
        _in_projection(x_ref[...], gpre_ref[...], win_ref), cos_ref[...], slo_ref[...],
        shi_ref[...], HEAD_DIM ** -0.5)
    q_ref[...] = q
    k_ref[...] = k
    v_ref[...] = v
    npast = (CONV_K - 1) * stride
    hist_ref[0:npast, :] = past_ref[...]
    hist_ref[npast:npast + rows, :] = gated
    conv = (convw_ref[0:1, :] * hist_ref[0:rows, :]
            + convw_ref[1:2, :] * hist_ref[stride:stride + rows, :]
            + convw_ref[2:3, :] * gated)
    cn_ref[...] = _rmsnorm(gate_b * conv, gconv_ref[...]).astype(BF16)
    cs_ref[...] = hist_ref[rows:rows + npast, :]


def _stat_lane_group(c, hh):
    return (HEADS_PER_SLAB - 1 - hh) * (HEAD_DIM // LSE_LANES_PER_HEAD) + c


def _attend(load, store_o, store_stats, bias_ref, i, start, kv, tq):
    lane = lax.broadcasted_iota(jnp.int32, (tq, LANES), 1)
    first_head = lane < HEAD_DIM
    group = lane // LSE_LANES_PER_HEAD
    bias = bias_ref[0, :, :kv] if isinstance(start, int) else bias_ref[jnp.minimum(i, 1)]
    max_tile = jnp.zeros((tq, LANES), F32)
    den_tile = jnp.zeros((tq, LANES), F32)
    key_lane = lax.broadcasted_iota(jnp.int32, (kv, LANES), 1)
    key_first = key_lane < HEAD_DIM
    for c in range(N_SLABS):
        q2 = load(0, c, i * tq, tq).astype(BF16)
        k2 = load(1, c, start, kv).astype(BF16)
        v2 = load(2, c, start, kv).astype(BF16)
        if kv == tq:
            zero = jnp.zeros_like(k2)
            k_both = jnp.concatenate([jnp.where(key_first, k2, zero), jnp.where(key_first, zero, k2)], 0)
            v_both = jnp.concatenate([jnp.where(key_first, v2, zero), jnp.where(key_first, zero, v2)], 0)
            s_both = lax.dot_general(q2, k_both, NT_DIMS, preferred_element_type=F32)
            probs = []
            for hh in range(HEADS_PER_SLAB):
                s = s_both[:, hh * kv:(hh + 1) * kv] + bias
                m = jnp.max(s, axis=1, keepdims=True)
                p = jnp.exp2(s - m).astype(BF16)
                den = jnp.sum(p.astype(F32), axis=1, keepdims=True)
                probs.append(p)
                mine = group == _stat_lane_group(c, hh)
                max_tile = jnp.where(mine, m, max_tile)
                den_tile = jnp.where(mine, den, den_tile)
            store_o(c, jnp.dot(jnp.concatenate(probs, axis=1), v_both, preferred_element_type=F32))
        else:
            one = jnp.ones_like(v2)
            outs = []
            for hh in range(HEADS_PER_SLAB):
                own = first_head if hh == 0 else ~first_head
                qm = jnp.where(own, q2, jnp.zeros_like(q2))
                s = lax.dot_general(qm, k2, NT_DIMS, preferred_element_type=F32) + bias
                m = jnp.max(s, axis=1, keepdims=True)
                p = jnp.exp2(s - m).astype(BF16)
                v_aug = jnp.where(key_first if hh == 0 else ~key_first, v2, one)
                r = jnp.dot(p, v_aug, preferred_element_type=F32)
                outs.append(r)
                mine = group == _stat_lane_group(c, hh)
                max_tile = jnp.where(mine, m, max_tile)
                den_tile = jnp.where(mine, r, den_tile)
            store_o(c, jnp.where(first_head, outs[0], outs[1]))
    store_stats(max_tile, den_tile)


def _prompt_attn_kernel(q_ref, k_ref, v_ref, bias_ref, expand_ref, gatt_ref, an_ref, o_scr, l_scr,
                        cls_scr, oc_scr, lc_scr, *, seq, tq):
    srcs = (q_ref, k_ref, v_ref)
    (_, near_d), (_, mid_d), (_, far_d) = DILATED_PATTERNS
    inner = far_d // mid_d
    sub_mid, sub_far = seq // mid_d, seq // far_d
    nq_far = sub_far // tq

    def aligned(first, count):
        if isinstance(first, int):
            return pl.ds(first, count)
        return pl.ds(pl.multiple_of(first, tq), count)

    def near_unit(i, carry):
        start = jnp.maximum(i - 1, 0) * tq
        rows = aligned(i * tq, tq)

        def store_o(c, val):
            o_scr[0, c, rows, :] = val

        def store_stats(top, den):
            l_scr[0, 0, rows, :] = top
            l_scr[0, 1, rows, :] = den

        _attend(lambda a, c, first, count: srcs[a][c, aligned(first, count), :],
                store_o, store_stats, bias_ref, i, start, min(2 * tq, seq), tq)
        return carry

    lax.fori_loop(0, seq // tq, near_unit, 0, unroll=UNITS_PER_STEP)

    def mid_class(r, slot):
        for a in range(3):
            for c in range(N_SLABS):
                cls_scr[slot, a, c] = srcs[a][c, pl.ds(r, sub_mid, stride=mid_d), :]

        def mid_unit(i, carry):
            start = jnp.maximum(i - 1, 0) * tq if sub_mid > tq else 0
            rows = pl.ds(r + mid_d * (i * tq), tq, stride=mid_d)

            def store_o(c, val):
                o_scr[1, c, rows, :] = val

            def store_stats(top, den):
                l_scr[1, 0, rows, :] = top
                l_scr[1, 1, rows, :] = den

            _attend(lambda a, c, first, count: cls_scr[slot, a, c, aligned(first, count), :],
                    store_o, store_stats, bias_ref, i, start, min(2 * tq, sub_mid), tq)
            return carry

        def far_unit(n, carry):
            m = n >> (nq_far.bit_length() - 1)
            i = n & (nq_far - 1)
            start = jnp.maximum(i - 1, 0) * tq if sub_far > tq else 0
            rows = pl.ds(m + inner * (i * tq), tq, stride=inner)

            def store_o(c, val):
                oc_scr[slot, c, rows, :] = val

            def store_stats(top, den):
                lc_scr[slot, 0, rows, :] = top
                lc_scr[slot, 1, rows, :] = den

            _attend(lambda a, c, first, count:
                    cls_scr[slot, a, c, pl.ds(m + inner * first, count, stride=inner), :],
                    store_o, store_stats, bias_ref, i, start, min(2 * tq, sub_far), tq)
            return carry

        lax.fori_loop(0, sub_mid // tq, lambda n, carry: far_unit(n, mid_unit(n, carry)), 0,
                      unroll=UNITS_PER_STEP)
        back = pl.ds(r, sub_mid, stride=mid_d)
        for c in range(N_SLABS):
            o_scr[2, c, back, :] = oc_scr[slot, c]
        for j in range(2):
            l_scr[2, j, back, :] = lc_scr[slot, j]

    def class_step(n, carry):
        for slot in range(CLASSES_PER_STEP):
            mid_class(n * CLASSES_PER_STEP + slot, slot)
        return carry

    lax.fori_loop(0, mid_d // CLASSES_PER_STEP, class_step, 0)

    n_pat = len(DILATED_PATTERNS)

    def merge(b, carry):
        r = pl.ds(pl.multiple_of(b * tq, tq), tq)
        tops = [l_scr[ip, 0, r, :] for ip in range(n_pat)]
        top = functools.reduce(jnp.maximum, tops)
        es = [jnp.exp2(t - top) for t in tops]
        inv = 1.0 / functools.reduce(lambda x, y: x + y,
                                     [e * l_scr[ip, 1, r, :] for ip, e in enumerate(es)])
        slabs = [None] * N_SLABS
        for ip, e in enumerate(es):
            wide = jnp.dot((e * inv).astype(BF16), expand_ref[...], preferred_element_type=F32)
            for c in range(N_SLABS):
                term = wide[:, c * LANES:(c + 1) * LANES] * o_scr[ip, c, r, :]
                slabs[c] = term if slabs[c] is None else slabs[c] + term
        attn = jnp.concatenate(slabs, axis=1)
        an_ref[r, :] = _rmsnorm(attn, gatt_ref[...]).astype(BF16)
        return carry

    lax.fori_loop(0, seq // tq, merge, 0, unroll=UNITS_PER_STEP)


def _sample_weights(qf, kn_rows, kt3, bias, n_t, new_pad):
    width = ATT_WIDTH
    n_pat = len(DILATED_PATTERNS)
    per = n_t * ATT_HEADS
    sub = lax.broadcasted_iota(jnp.int32, (ATT_HEADS, width), 0)
    lane = lax.broadcasted_iota(jnp.int32, (ATT_HEADS, width), 1)
    own_head = sub == lane // HEAD_DIM
    span = kt3.shape[-1]
    pad = jnp.zeros((new_pad - n_t, width), F32)
    q_tiles = [jnp.where(own_head, jnp.broadcast_to(qf[t:t + 1, :], (ATT_HEADS, width)), 0.0)
               for t in range(n_t)]
    qbd = jnp.concatenate(q_tiles * n_pat, axis=0).astype(BF16)
    kt = kt3.reshape(width, span).astype(BF16)
    kn = jnp.concatenate([kn_rows, pad], axis=0).astype(BF16)
    s = jnp.concatenate(
        [jnp.dot(qbd, kt, preferred_element_type=F32),
         lax.dot_general(qbd, kn, NT_DIMS, preferred_element_type=F32)], axis=1) + bias
    m = jnp.max(s, axis=1, keepdims=True)
    p = jnp.exp(s - m)
    den = jnp.sum(p, axis=1, keepdims=True)
    lse = m + jnp.log(den)
    lses = [lse[i * per:(i + 1) * per, :] for i in range(n_pat)]
    top = functools.reduce(jnp.maximum, lses)
    es = [jnp.exp(l - top) for l in lses]
    z = functools.reduce(lambda a, b: a + b, es)
    scale = jnp.concatenate([e / z for e in es], axis=0) / den
    return (p * scale).astype(BF16)


def _sample_output(pw, vn_rows, vt3, gatt, n_t):
    width = ATT_WIDTH
    n_pat = len(DILATED_PATTERNS)
    per = n_t * ATT_HEADS
    sub = lax.broadcasted_iota(jnp.int32, (ATT_HEADS, width), 0)
    lane = lax.broadcasted_iota(jnp.int32, (ATT_HEADS, width), 1)
    own_head = sub == lane // HEAD_DIM
    span = vt3.shape[-1]
    pad = jnp.zeros((pw.shape[-1] - span - n_t, width), F32)
    vt = vt3.reshape(width, span).astype(BF16)
    vn = jnp.concatenate([vn_rows, pad], axis=0).astype(BF16)
    o = (lax.dot_general(pw[:, :span], vt, NT_DIMS, preferred_element_type=F32)
         + jnp.dot(pw[:, span:], vn, preferred_element_type=F32))
    rows = []
    for t in range(n_t):
        acc = o[t * ATT_HEADS:(t + 1) * ATT_HEADS, :]
        for i in range(1, n_pat):
            acc = acc + o[i * per + t * ATT_HEADS:i * per + (t + 1) * ATT_HEADS, :]
        attn = jnp.sum(jnp.where(own_head, acc, 0.0), axis=0, keepdims=True)
        rows.append(_rmsnorm(attn, gatt))
    return rows


def _sample_scores_kernel(q_ref, kn_ref, kt_ref, bias_ref, pw_ref, *, n_t, new_pad):
    for j in range(kt_ref.shape[0]):
        pw_ref[j] = _sample_weights(q_ref[:, j, 0, :], kn_ref[:, j, 0, :], kt_ref[j], bias_ref[...],
                                    n_t, new_pad)


def _sample_values_start(pw_ref, vn_ref, vt_ref, gatt_ref, q_ref, kn_ref, kt_ref, bias_ref, o_ref,
                         *, n_t, new_pad):
    ready, n_seq = pw_ref.shape[0], vt_ref.shape[0]
    made = [_sample_weights(q_ref[:, j, 0, :], kn_ref[:, j, 0, :], kt_ref[j], bias_ref[...],
                            n_t, new_pad) for j in range(n_seq - ready)]

    def finish(j, pw):
        rows = _sample_output(pw, vn_ref[:, j, 0, :], vt_ref[j], gatt_ref[...], n_t)
        for t in range(n_t):
            o_ref[t, j, :, :] = rows[t]

    for j in range(ready):
        finish(j, pw_ref[j])

    def later():
        for j, pw in enumerate(made):
            finish(ready + j, pw)

    return later


def _post_kernel(*refs, ff_chunk, sample_n_t, sample_new_pad):
    (x_ref, an_ref, cn_ref, wout_ref, npost_ref, npre_ref, wup_ref, wdown_ref, nmlp_ref) = refs[:9]
    y_ref = refs[17] if sample_n_t else refs[9]
    aw = an_ref.shape[1]
    d_ff = wup_ref.shape[1]
    rows = x_ref.shape[0] // ROW_GROUPS
    groups = [slice(g * rows, (g + 1) * rows) for g in range(ROW_GROUPS)]
    if sample_n_t:
        later = _sample_values_start(*refs[9:17], refs[18], n_t=sample_n_t, new_pad=sample_new_pad)
    mixed = [jnp.dot(an_ref[r, :].astype(BF16), wout_ref[0:aw, :], preferred_element_type=F32)
             + jnp.dot(cn_ref[r, :], wout_ref[aw:, :], preferred_element_type=F32) for r in groups]
    if sample_n_t:
        later()
    for r, mix in zip(groups, mixed):
        x1 = x_ref[r, :] + _rmsnorm(mix, npost_ref[...])
        h = _rmsnorm(x1, npre_ref[...]).astype(BF16)
        f = None
        for c in range(d_ff // ff_chunk):
            u = jnp.dot(h, wup_ref[:, c * ff_chunk:(c + 1) * ff_chunk], preferred_element_type=F32)
            a = jnp.square(jnp.maximum(u, 0.0)).astype(BF16)
            part = jnp.dot(a, wdown_ref[c * ff_chunk:(c + 1) * ff_chunk, :],
                           preferred_element_type=F32)
            f = part if f is None else f + part
        y_ref[r, :] = x1 + _rmsnorm(f, nmlp_ref[...])


def _const_spec(shape):
    nd = len(shape)
    return pl.BlockSpec(shape, lambda *_: (0,) * nd, pipeline_mode=pl.Buffered(1))


def _rope_tables(positions):
    half = HEAD_DIM // 2
    inv = ROPE_THETA ** (-jnp.arange(half, dtype=F32) * 2.0 / HEAD_DIM)
    ang = positions.astype(F32)[:, None] * inv[None, :]
    cos, sin, zero = jnp.cos(ang), jnp.sin(ang), jnp.zeros_like(ang)
    reps = LANES // HEAD_DIM
    cos_t = jnp.tile(jnp.concatenate([cos, cos], axis=1), (1, reps))
    sin_lo = jnp.tile(jnp.concatenate([-sin, zero], axis=1), (1, reps))
    sin_hi = jnp.tile(jnp.concatenate([zero, sin], axis=1), (1, reps))
    return cos_t, sin_lo, sin_hi


def _params(*semantics):
    return pltpu.CompilerParams(dimension_semantics=semantics, vmem_limit_bytes=VMEM_LIMIT_BYTES)


def _sample_blocks(n_t, dec_batch, span, steps):
    assert dec_batch % (2 * steps) == 0, "every grid step takes the same, even number of sequences"
    bb = dec_batch // steps
    half = bb // 2
    new_pad = -(-n_t // LANES) * LANES
    n_score = len(DILATED_PATTERNS) * n_t * ATT_HEADS

    def rows(count, index):
        return pl.BlockSpec((n_t, count, 1, ATT_WIDTH), lambda i: (0, index(i), 0, 0))

    def buffers(count, index):
        return pl.BlockSpec((count, ATT_HEADS, HEAD_DIM, span), lambda i: (index(i), 0, 0, 0))

    first, second = (lambda i: 2 * i), (lambda i: 2 * i + 1)
    return dict(
        rows_all=rows(bb, lambda i: i), bufs_all=buffers(bb, lambda i: i),
        rows_first=rows(half, first), bufs_first=buffers(half, first),
        rows_second=rows(half, second), bufs_second=buffers(half, second),
        weights=pl.BlockSpec((half, n_score, span + new_pad), lambda i: (i, 0, 0)),
        weights_shape=(steps * half, n_score, span + new_pad), new_pad=new_pad)


def _proj_prompt(x2d, seq, gpre, win, convw, gconv, sq, skn, cache_k, past_len):
    n, d = x2d.shape
    tm = min(ROW_BLOCK, seq)
    assert seq % tm == 0 and tm % 8 == 0
    bps = seq // tm
    batch = n // seq
    steps = n // tm
    cw = convw.shape[1]
    tables = _rope_tables(jnp.arange(seq))
    row = lambda w: pl.BlockSpec((tm, w), lambda i: (i, 0))
    slabbed = pl.BlockSpec((None, N_SLABS, tm, LANES), lambda i: (i // bps, 0, i % bps, 0))
    transposed = pl.BlockSpec((None, ATT_WIDTH, tm), lambda i: (i // bps, 0, i % bps))
    slab_shape = jax.ShapeDtypeStruct((batch, N_SLABS, seq, LANES), F32)
    t_shape = jax.ShapeDtypeStruct((batch, ATT_WIDTH, seq), F32)

    n_t, dec_batch = sq.shape[0], sq.shape[1]
    span = cache_k.shape[1]
    assert span == min(MAX_SPAN, past_len)
    blocks = _sample_blocks(n_t, dec_batch, span, steps)
    new_pad = blocks["new_pad"]
    bias = _sample_bias(n_t, past_len, span, new_pad)
    skt = jnp.transpose(cache_k, (0, 2, 3, 1))

    out_shape = (slab_shape, slab_shape, slab_shape, t_shape, t_shape,
                 jax.ShapeDtypeStruct((n, cw), BF16),
                 jax.ShapeDtypeStruct((batch, CONV_K - 1, cw), F32),
                 jax.ShapeDtypeStruct(blocks["weights_shape"], BF16))
    return pl.pallas_call(
        functools.partial(_proj_prompt_kernel, tm=tm, blocks_per_seq=bps, n_t=n_t, new_pad=new_pad),
        grid=(steps,),
        in_specs=[row(d), _const_spec((1, d)), _const_spec(win.shape)]
        + [_const_spec((seq, LANES))] * 3 + [_const_spec(convw.shape), _const_spec((1, cw))]
        + [blocks["rows_first"], blocks["rows_first"], blocks["bufs_first"],
           _const_spec(bias.shape)],
        out_specs=[slabbed, slabbed, slabbed, transposed, transposed, row(cw),
                   pl.BlockSpec((None, CONV_K - 1, cw), lambda i: (i // bps, 0, 0)),
                   blocks["weights"]],
        out_shape=out_shape,
        scratch_shapes=[pltpu.VMEM((tm + 8, cw), F32)],
        compiler_params=_params("arbitrary"),
        name="proj_prompt",
    )(x2d, gpre, win, *tables, convw, gconv, sq, skn, skt, bias)


def _proj_sample(x2d, positions, stride, past2d, gpre, win, convw, gconv):
    rows, d = x2d.shape
    cw = convw.shape[1]
    tables = [jnp.repeat(t, stride, axis=0) for t in _rope_tables(positions)]
    npast = (CONV_K - 1) * stride
    full = lambda shape: pl.BlockSpec(shape, lambda i: (0,) * len(shape))
    out_shape = (
        jax.ShapeDtypeStruct((rows, ATT_WIDTH), F32), jax.ShapeDtypeStruct((rows, ATT_WIDTH), F32),
        jax.ShapeDtypeStruct((rows, ATT_WIDTH), F32), jax.ShapeDtypeStruct((rows, cw), BF16),
        jax.ShapeDtypeStruct((npast, cw), F32))
    return pl.pallas_call(
        functools.partial(_proj_sample_kernel, rows=rows, stride=stride),
        grid=(1,),
        in_specs=[full((rows, d)), full((1, d)), full(win.shape)] + [full((rows, LANES))] * 3
        + [full(convw.shape), full((1, cw)), full((npast, cw))],
        out_specs=[full((rows, ATT_WIDTH))] * 3 + [full((rows, cw)), full((npast, cw))],
        out_shape=out_shape,
        scratch_shapes=[pltpu.VMEM((rows + npast, cw), F32)],
        compiler_params=_params("arbitrary"),
        name="proj_sample",
    )(x2d, gpre, win, *tables, convw, gconv, past2d)


def _prompt_attn(q, k, v, gatt):
    batch, _, seq, _ = q.shape
    tq = Q_BLOCK
    (_, near_d), (_, mid_d), (_, far_d) = DILATED_PATTERNS
    assert near_d == 1 and far_d % mid_d == 0 and mid_d % CLASSES_PER_STEP == 0
    for window, dil in DILATED_PATTERNS:
        sub = seq // dil
        assert seq % dil == 0 and sub % tq == 0 and window % dil == 0 and window // dil <= tq
        assert (sub // tq) & (sub // tq - 1) == 0
    sub_mid = seq // mid_d
    reaches = {window // dil for window, dil in DILATED_PATTERNS}
    assert len(reaches) == 1, "one mask table serves every pattern"
    dist = np.arange(tq)[None, :, None] - np.arange(2 * tq)[None, None, :] + tq * np.arange(2)[:, None, None]
    bias = jnp.asarray(np.where((dist >= 0) & (dist <= reaches.pop()), 0.0, MASK_VALUE), F32)
    col_head = np.arange(ATT_WIDTH) // HEAD_DIM
    col_group = np.array([_stat_lane_group(h // HEADS_PER_SLAB, h % HEADS_PER_SLAB) for h in col_head])
    expand = jnp.asarray(np.arange(LANES)[:, None] == col_group[None, :] * LSE_LANES_PER_HEAD, BF16)
    n_pat = len(DILATED_PATTERNS)
    blk = pl.BlockSpec((None, N_SLABS, seq, LANES), lambda b: (b, 0, 0, 0))
    return pl.pallas_call(
        functools.partial(_prompt_attn_kernel, seq=seq, tq=tq),
        grid=(batch,),
        in_specs=[blk, blk, blk, _const_spec(bias.shape), _const_spec(expand.shape),
                  _const_spec(gatt.shape)],
        out_specs=pl.BlockSpec((seq, ATT_WIDTH), lambda b: (b, 0)),
        out_shape=jax.ShapeDtypeStruct((batch * seq, ATT_WIDTH), BF16),
        scratch_shapes=[pltpu.VMEM((n_pat, N_SLABS, seq, LANES), F32),
                        pltpu.VMEM((n_pat, 2, seq, LANES), F32),
                        pltpu.VMEM((CLASSES_PER_STEP, 3, N_SLABS, sub_mid, LANES), F32),
                        pltpu.VMEM((CLASSES_PER_STEP, N_SLABS, sub_mid, LANES), F32),
                        pltpu.VMEM((CLASSES_PER_STEP, 2, sub_mid, LANES), F32)],
        compiler_params=_params("arbitrary"),
        name="prompt_attn",
    )(q, k, v, bias, expand, gatt)


def _sample_bias(n_t, past_len, span, new_pad):
    big = 1 << 30
    rel = np.concatenate([np.arange(span) - span, np.arange(n_t),
                          np.full((new_pad - n_t,), -big)])
    rows = []
    for window, dil in DILATED_PATTERNS:
        for t in range(n_t):
            dist = t - rel
            ok = (dist >= 0) & (dist <= window) & (dist % dil == 0) & (past_len + rel >= 0)
            rows += [np.where(ok, 0.0, MASK_VALUE)] * ATT_HEADS
    return jnp.asarray(np.stack(rows), F32)


def _post(x2d, an, cn, wout, npost, npre, wup, wdown, nmlp, sample=None):
    n, d = x2d.shape
    tm = min(ROW_BLOCK, n)
    assert n % tm == 0
    steps = n // tm
    row = lambda w: pl.BlockSpec((tm, w), lambda i: (i, 0))
    args = [x2d, an, cn, wout, npost, npre, wup, wdown, nmlp]
    in_specs = [row(d), row(an.shape[1]), row(cn.shape[1]), _const_spec(wout.shape),
                _const_spec((1, d)), _const_spec((1, d)), _const_spec(wup.shape),
                _const_spec(wdown.shape), _const_spec((1, d))]
    out_specs = [row(d)]
    out_shape = [jax.ShapeDtypeStruct((n, d), F32)]
    n_t = new_pad = 0
    if sample:
        pw, sq, skn, svn, cache_k, cache_v, past_len, gatt = sample
        n_t, dec_batch = svn.shape[0], svn.shape[1]
        span = cache_v.shape[1]
        blocks = _sample_blocks(n_t, dec_batch, span, steps)
        new_pad = blocks["new_pad"]
        assert pw.shape == blocks["weights_shape"]
        bias = _sample_bias(n_t, past_len, span, new_pad)
        skt = jnp.transpose(cache_k, (0, 2, 3, 1))
        svt = jnp.transpose(cache_v, (0, 2, 3, 1))
        args += [pw, svn, svt, gatt, sq, skn, skt, bias]
        in_specs += [blocks["weights"], blocks["rows_all"], blocks["bufs_all"],
                     _const_spec(gatt.shape), blocks["rows_second"], blocks["rows_second"],
                     blocks["bufs_second"], _const_spec(bias.shape)]
        out_specs.append(blocks["rows_all"])
        out_shape.append(jax.ShapeDtypeStruct((n_t, dec_batch, 1, ATT_WIDTH), F32))
    outs = pl.pallas_call(
        functools.partial(_post_kernel, ff_chunk=min(FF_CHUNK, wup.shape[1]), sample_n_t=n_t,
                          sample_new_pad=new_pad),
        grid=(steps,),
        in_specs=in_specs,
        out_specs=out_specs,
        out_shape=out_shape,
        compiler_params=_params("arbitrary"),
        name="post",
    )(*args)
    return outs if sample else outs[0]


def kernel(x_prompt, x_sample, cache_k, cache_v, state_conv, n_att_pre, n_att_post, w_in, conv_w,
           g_att, g_conv, w_out, n_mlp_pre, n_mlp_post, w_up, w_down):
    depth = w_in.shape[0]
    batch, seq, d = x_prompt.shape
    dec_batch, dec_seq, _ = x_sample.shape
    past_len = PAST_LEN
    keep = min(MAX_SPAN, seq)
    assert keep == seq, "the prompt's window buffer is its whole key/value sequence"
    cw = conv_w.shape[2]

    yp = x_prompt.reshape(batch * seq, d)
    ys = x_sample.swapaxes(0, 1).reshape(dec_seq * dec_batch, d)
    outs = [[] for _ in range(6)]
    for l in range(depth):
        win, wout = w_in[l].astype(BF16), w_out[l].astype(BF16)
        wup, wdown = w_up[l].astype(BF16), w_down[l].astype(BF16)
        gpre, npost = n_att_pre[l][None, :], n_att_post[l][None, :]
        npre, nmlp = n_mlp_pre[l][None, :], n_mlp_post[l][None, :]
        gatt, gconv = g_att[l][None, :], g_conv[l][None, :]

        past = state_conv[l].swapaxes(0, 1).reshape((CONV_K - 1) * dec_batch, cw)
        qs, ks, vs, cns, css = _proj_sample(ys, past_len + jnp.arange(dec_seq), dec_batch, past,
                                            gpre, win, conv_w[l], gconv)
        tm4 = lambda a: a.reshape(dec_seq, dec_batch, 1, ATT_WIDTH)
        q, k, v, kt, vt, cn, cs, pw = _proj_prompt(yp, seq, gpre, win, conv_w[l], gconv,
                                                   tm4(qs), tm4(ks), cache_k[l], past_len)
        an = _prompt_attn(q, k, v, gatt)
        yp, ans = _post(yp, an, cn, wout, npost, npre, wup, wdown, nmlp,
                        sample=(pw, tm4(qs), tm4(ks), tm4(vs), cache_k[l], cache_v[l], past_len,
                                gatt))
        per_head = lambda a: a.reshape(batch, ATT_HEADS, HEAD_DIM, seq).transpose(0, 3, 1, 2)
        outs[0].append(per_head(kt))
        outs[1].append(per_head(vt))
        outs[2].append(cs)
        ys = _post(ys, ans.reshape(dec_seq * dec_batch, ATT_WIDTH), cns, wout, npost, npre,
                   wup, wdown, nmlp)
        bm = lambda a: a.reshape(dec_seq, dec_batch, ATT_HEADS, HEAD_DIM).swapaxes(0, 1)
        outs[3].append(bm(ks))
        outs[4].append(bm(vs))
        outs[5].append(css.reshape(CONV_K - 1, dec_batch, cw).swapaxes(0, 1))

    y_prompt = yp.reshape(batch, seq, d)
    y_sample = ys.reshape(dec_seq, dec_batch, d).swapaxes(0, 1)
    return (y_prompt, y_sample) + tuple(jnp.stack(o) for o in outs)
```
